```python
import numpy as np
import jax
import jax.numpy as jnp
from jax import lax

D_MODEL = 4096
BATCH = 1
SEQ = 16384
DEPTH = 4

CHUNK = 64
HEAD_DIM = 128
N_BRANCH = 4
BRANCH_W = 1024
Q_BLOCK = 128
ROPE_THETA = 10000.0
EPS = 1e-6
NEG_INF = -1e30
MAX_STREAM_OFFSET = 4096

A_HEADS = 8
A_PREV_CHUNKS = 8
REL_CLIP = 256

B_Q_HEADS = 8
B_KV_HEADS = 2
B_WINDOW = 128
B_PREV_CHUNKS = B_WINDOW // CHUNK

C_HEADS = 8

D_HEADS = 8
D_Q_LORA = 896
D_KV_LORA = 256
D_NOPE = 128
D_ROPE = 64
D_V = 128
D_QK = D_NOPE + D_ROPE

N_EXPERTS = 32
TOP_K = 4
D_EXPERT = 128
SWIGLU_LIMIT = 7.0
SWIGLU_ALPHA = 1.702

IN_SIZES = (A_HEADS * HEAD_DIM, A_HEADS * HEAD_DIM, A_HEADS * HEAD_DIM,
            B_Q_HEADS * HEAD_DIM, B_KV_HEADS * HEAD_DIM, B_KV_HEADS * HEAD_DIM,
            C_HEADS * HEAD_DIM, C_HEADS * HEAD_DIM, C_HEADS * HEAD_DIM, C_HEADS,
            D_Q_LORA, D_KV_LORA, D_ROPE)
IN_COLS = sum(IN_SIZES)

kernel_name = 'hybrid_chunk_causal_gated_moe_trunk'


def rms_norm(x, g):
    xf = x.astype(jnp.float32)
    y = xf * lax.rsqrt(jnp.mean(xf * xf, axis=-1, keepdims=True) + EPS)
    return (y * g.astype(jnp.float32)).astype(x.dtype)


def rope(x, pos):
    d = x.shape[-1]
    half = d // 2
    inv_freq = ROPE_THETA ** (-jnp.arange(half, dtype=jnp.float32) / half)
    ang = pos.astype(jnp.float32)[:, :, None] * inv_freq
    cos = jnp.cos(ang)[:, :, None, :]
    sin = jnp.sin(ang)[:, :, None, :]
    xf = x.astype(jnp.float32)
    x1, x2 = xf[..., :half], xf[..., half:]
    return jnp.concatenate([x1 * cos - x2 * sin, x2 * cos + x1 * sin], axis=-1).astype(x.dtype)


def split_cols(t, sizes):
    out, start = [], 0
    for n in sizes:
        out.append(t[..., start:start + n])
        start += n
    return out


def to_heads(t, n_heads):
    b, s, _ = t.shape
    return t.reshape(b, s, n_heads, -1)


def chunk_band(t, n_prev):
    b, s = t.shape[:2]
    nc = s // CHUNK
    tc = t.reshape((b, nc, CHUNK) + t.shape[2:])
    tp = jnp.pad(tc, [(0, 0), (n_prev, 0)] + [(0, 0)] * (tc.ndim - 2))
    idx = jnp.arange(nc)[:, None] + jnp.arange(n_prev + 1)[None, :]
    band = tp[:, idx]
    return band.reshape((b, nc, (n_prev + 1) * CHUNK) + t.shape[2:])


def band_valid(nc, n_prev):
    src_chunk = jnp.arange(nc)[:, None] + jnp.arange(n_prev + 1)[None, :] - n_prev
    return jnp.repeat(src_chunk >= 0, CHUNK, axis=1)


def chunked_relpos_attention(q, k, v, rel_table):
    b, s, h, d = q.shape
    nc = s // CHUNK
    band = (A_PREV_CHUNKS + 1) * CHUNK
    qc = q.reshape(b, nc, CHUNK, h, d)
    kb = chunk_band(k, A_PREV_CHUNKS)
    vb = chunk_band(v, A_PREV_CHUNKS)
    logits = jnp.einsum('bcqhd,bckhd->bhcqk', qc, kb).astype(jnp.float32) * (d ** -0.5)
    dist = jnp.arange(CHUNK)[:, None] + A_PREV_CHUNKS * CHUNK - jnp.arange(band)[None, :]
    rel_idx = jnp.clip(dist, -REL_CLIP, REL_CLIP) + REL_CLIP
    bias = rel_table[rel_idx].astype(jnp.float32).transpose(2, 0, 1)
    logits = logits + bias[None, :, None]
    valid = band_valid(nc, A_PREV_CHUNKS)
    logits = jnp.where(valid[None, None, :, None, :], logits, NEG_INF)
    p = jax.nn.softmax(logits, axis=-1).astype(v.dtype)
    out = jnp.einsum('bhcqk,bckhd->bcqhd', p, vb)
    return out.reshape(b, s, h * d)


def sliding_window_sink_gqa(q, k, v, sinks):
    b, s, hq, d = q.shape
    hk = k.shape[2]
    g = hq // hk
    nc = s // CHUNK
    qc = q.reshape(b, nc, CHUNK, hk, g, d)
    kb = chunk_band(k, B_PREV_CHUNKS)
    vb = chunk_band(v, B_PREV_CHUNKS)
    logits = jnp.einsum('bcqkgd,bcskd->bkgcqs', qc, kb).astype(jnp.float32) * (d ** -0.5)
    valid = band_valid(nc, B_PREV_CHUNKS)
    logits = jnp.where(valid[None, None, None, :, None, :], logits, NEG_INF)
    sink = jnp.broadcast_to(sinks.astype(jnp.float32).reshape(1, hk, g, 1, 1, 1),
                            logits.shape[:-1] + (1,))
    p = jax.nn.softmax(jnp.concatenate([logits, sink], axis=-1), axis=-1)[..., :-1]
    out = jnp.einsum('bkgcqs,bcskd->bcqkgd', p.astype(v.dtype), vb)
    return out.reshape(b, s, hq * d)


def blocked_attention(q, k, v, scale, causal_unit, cum_logf=None):
    b, s, h, dq = q.shape
    dv = v.shape[-1]
    nb = s // Q_BLOCK
    key_unit = jnp.arange(s) // causal_unit
    q_blocks = q.reshape(b, nb, Q_BLOCK, h, dq).swapaxes(0, 1)
    blk_ids = jnp.arange(nb)
    if cum_logf is None:
        xs = (q_blocks, blk_ids)
        f_key = None
    else:
        f_key = cum_logf.transpose(0, 2, 1)[:, :, None, :]
        xs = (q_blocks, blk_ids, cum_logf.reshape(b, nb, Q_BLOCK, h).swapaxes(0, 1))

    def one_block(args):
        q_blk, i = args[0], args[1]
        logits = jnp.einsum('bqhd,bkhd->bhqk', q_blk, k).astype(jnp.float32) * scale
        if f_key is not None:
            logits = logits + args[2].transpose(0, 2, 1)[..., None] - f_key
        q_unit = (i * Q_BLOCK + jnp.arange(Q_BLOCK)) // causal_unit
        allowed = key_unit[None, :] <= q_unit[:, None]
        logits = jnp.where(allowed, logits, NEG_INF)
        p = jax.nn.softmax(logits, axis=-1).astype(v.dtype)
        return jnp.einsum('bhqk,bkhd->bqhd', p, v)

    out = lax.map(one_block, xs)
    return out.swapaxes(0, 1).reshape(b, s, h * dv)


def mla_qkv(q_lat, kv_lat, k_rope, pos, q_a_norm, w_q_b, kv_a_norm, w_kv_b, q_norm, k_norm):
    b, s, _ = q_lat.shape
    q = (rms_norm(q_lat, q_a_norm) @ w_q_b).reshape(b, s, D_HEADS, D_QK)
    kv = (rms_norm(kv_lat, kv_a_norm) @ w_kv_b).reshape(b, s, D_HEADS, D_NOPE + D_V)
    k_nope, v = kv[..., :D_NOPE], kv[..., D_NOPE:]
    k = jnp.concatenate([k_nope, jnp.broadcast_to(k_rope[:, :, None, :], (b, s, D_HEADS, D_ROPE))], axis=-1)
    q = rms_norm(q, q_norm)
    k = rms_norm(k, k_norm)
    q = jnp.concatenate([q[..., :D_NOPE], rope(q[..., D_NOPE:], pos)], axis=-1)
    k = jnp.concatenate([k[..., :D_NOPE], rope(k[..., D_NOPE:], pos)], axis=-1)
    return q, k, v


def moe_ffn(h, router_w, router_b, w_gu, b_gu, w_dn, b_dn):
    logits = (h @ router_w).astype(jnp.float32) + router_b.astype(jnp.float32)
    top_vals, top_idx = lax.top_k(logits, TOP_K)
    top_w = jax.nn.softmax(top_vals, axis=-1)
    comb = jnp.einsum('bsk,bske->bse', top_w,
                      jax.nn.one_hot(top_idx, N_EXPERTS, dtype=jnp.float32)).astype(h.dtype)
    gu = jnp.einsum('bsd,edf->bsef', h, w_gu) + b_gu
    glu = jnp.minimum(gu[..., :D_EXPERT], SWIGLU_LIMIT)
    lin = jnp.clip(gu[..., D_EXPERT:], -SWIGLU_LIMIT, SWIGLU_LIMIT)
    act = glu * jax.nn.sigmoid(SWIGLU_ALPHA * glu) * (lin + 1.0)
    return jnp.einsum('bsef,efd->bsd', act * comb[..., None], w_dn) + comb @ b_dn


def setup_inputs(seed: int = 0) -> dict:
    key = jax.random.key(seed)
    ks = iter(jax.random.split(key, 40))
    L, D = DEPTH, D_MODEL

    def nrm(shape, scale):
        return scale * jax.random.normal(next(ks), shape, jnp.float32)

    def gain(shape):
        return 1.0 + nrm(shape, 0.05)

    offset = jax.random.randint(next(ks), (BATCH, 1), 0, MAX_STREAM_OFFSET, dtype=jnp.int32)
    return {
        'x': nrm((BATCH, SEQ, D), 1.0),
        'c': nrm((BATCH, D), 1.0),
        'positions': (offset + jnp.arange(SEQ, dtype=jnp.int32)[None, :]).astype(jnp.int32),
        'ada_w': nrm((D, 6 * D), 0.5 * D ** -0.5),
        'ada_b': nrm((6 * D,), 0.02),
        'ada_layer': nrm((L, 6, D), 0.1),
        'norm1_g': gain((L, D)),
        'norm2_g': gain((L, D)),
        'w_in': nrm((L, D, IN_COLS), D ** -0.5),
        'a_q_norm': gain((L, HEAD_DIM)),
        'a_k_norm': gain((L, HEAD_DIM)),
        'a_rel_bias': nrm((L, 2 * REL_CLIP + 1, A_HEADS), 0.5),
        'b_q_norm': gain((L, HEAD_DIM)),
        'b_k_norm': gain((L, HEAD_DIM)),
        'b_sinks': nrm((L, B_Q_HEADS), 1.0),
        'c_q_norm': gain((L, HEAD_DIM)),
        'c_k_norm': gain((L, HEAD_DIM)),
        'c_f_bias': 3.0 + nrm((L, C_HEADS), 0.5),
        'd_q_a_norm': gain((L, D_Q_LORA)),
        'd_w_q_b': nrm((L, D_Q_LORA, D_HEADS * D_QK), D_Q_LORA ** -0.5),
        'd_kv_a_norm': gain((L, D_KV_LORA)),
        'd_w_kv_b': nrm((L, D_KV_LORA, D_HEADS * (D_NOPE + D_V)), D_KV_LORA ** -0.5),
        'd_q_norm': gain((L, D_QK)),
        'd_k_norm': gain((L, D_QK)),
        'w_branch': nrm((L, N_BRANCH, BRANCH_W, D), BRANCH_W ** -0.5),
        'w_gate': nrm((L, N_BRANCH, D, D), D ** -0.5),
        'b_gate': nrm((L, N_BRANCH, D), 0.02),
        'w_out': nrm((L, D, D), D ** -0.5),
        'router_w': nrm((L, D, N_EXPERTS), D ** -0.5),
        'router_b': nrm((L, N_EXPERTS), 0.01),
        'w_gu': nrm((L, N_EXPERTS, D, 2 * D_EXPERT), D ** -0.5),
        'b_gu': nrm((L, N_EXPERTS, 2 * D_EXPERT), 0.02),
        'w_dn': nrm((L, N_EXPERTS, D_EXPERT, D), D_EXPERT ** -0.5),
        'b_dn': nrm((L, N_EXPERTS, D), 0.02),
    }


def reference(x, c, positions, ada_w, ada_b, ada_layer, norm1_g, norm2_g, w_in,
              a_q_norm, a_k_norm, a_rel_bias, b_q_norm, b_k_norm, b_sinks,
              c_q_norm, c_k_norm, c_f_bias, d_q_a_norm, d_w_q_b, d_kv_a_norm, d_w_kv_b,
              d_q_norm, d_k_norm, w_branch, w_gate, b_gate, w_out,
              router_w, router_b, w_gu, b_gu, w_dn, b_dn):
    b, s, d_model = x.shape
    base_mod = (jax.nn.silu(c) @ ada_w + ada_b).reshape(b, 6, d_model)
    for l in range(DEPTH):
        mod = base_mod + ada_layer[l]
        shift1, scale1, gate1, shift2, scale2, gate2 = (mod[:, j, None, :] for j in range(6))

        h = rms_norm(x, norm1_g[l]) * (1.0 + scale1) + shift1
        (qA, kA, vA, qB, kB, vB, qC, kC, vC, fC, q_lat, kv_lat, k_rope) = split_cols(h @ w_in[l], IN_SIZES)

        o_a = chunked_relpos_attention(rms_norm(to_heads(qA, A_HEADS), a_q_norm[l]),
                                       rms_norm(to_heads(kA, A_HEADS), a_k_norm[l]),
                                       to_heads(vA, A_HEADS), a_rel_bias[l])

        o_b = sliding_window_sink_gqa(rope(rms_norm(to_heads(qB, B_Q_HEADS), b_q_norm[l]), positions),
                                      rope(rms_norm(to_heads(kB, B_KV_HEADS), b_k_norm[l]), positions),
                                      to_heads(vB, B_KV_HEADS), b_sinks[l])

        log_f = jax.nn.log_sigmoid(fC.astype(jnp.float32) + c_f_bias[l].astype(jnp.float32))
        cum_logf = jnp.cumsum(log_f, axis=1)
        o_c = blocked_attention(rms_norm(to_heads(qC, C_HEADS), c_q_norm[l]),
                                rms_norm(to_heads(kC, C_HEADS), c_k_norm[l]),
                                to_heads(vC, C_HEADS), HEAD_DIM ** -0.5, 1, cum_logf)

        qD, kD, vD = mla_qkv(q_lat, kv_lat, k_rope, positions, d_q_a_norm[l], d_w_q_b[l],
                             d_kv_a_norm[l], d_w_kv_b[l], d_q_norm[l], d_k_norm[l])
        o_d = blocked_attention(qD, kD, vD, D_QK ** -0.5, CHUNK)

        merged = jnp.zeros_like(x)
        for i, o in enumerate((o_a, o_b, o_c, o_d)):
            g = jax.nn.sigmoid((h @ w_gate[l, i]).astype(jnp.float32)
                               + b_gate[l, i].astype(jnp.float32)).astype(x.dtype)
            merged = merged + g * (o @ w_branch[l, i])
        x = x + gate1 * (merged @ w_out[l])

        h2 = rms_norm(x, norm2_g[l]) * (1.0 + scale2) + shift2
        x = x + gate2 * moe_ffn(h2, router_w[l], router_b[l], w_gu[l], b_gu[l], w_dn[l], b_dn[l])
    return x
```

```python
import functools
import math

import numpy as np
import jax
import jax.numpy as jnp
from jax import lax
from jax.experimental import pallas as pl
from jax.experimental.pallas import tpu as pltpu

F32 = jnp.float32
BF16 = jnp.bfloat16

CHUNK = 64
HEAD_DIM = 128
N_BRANCH = 4
BRANCH_W = 1024
ROPE_THETA = 10000.0
EPS = 1e-6
NEG_INF = -1e30
A_HEADS = 8
A_PREV_CHUNKS = 8
REL_CLIP = 256
B_Q_HEADS = 8
B_KV_HEADS = 2
B_PREV_CHUNKS = 2
C_HEADS = 8
D_HEADS = 8
D_Q_LORA = 896
D_KV_LORA = 256
D_NOPE = 128
D_ROPE = 64
D_V = 128
D_QK = D_NOPE + D_ROPE
N_EXPERTS = 32
TOP_K = 4
D_EXPERT = 128
SWIGLU_LIMIT = 7.0
SWIGLU_ALPHA = 1.702

LANES = 128
V7X_VMEM_BYTES = 64 * 1024 * 1024

LOG2E = math.log2(math.e)

OFF_QA, OFF_KA, OFF_VA = 0, 1024, 2048
OFF_QB, OFF_KB, OFF_VB = 3072, 4096, 4352
OFF_QC, OFF_KC, OFF_VC = 4608, 5632, 6656
OFF_QLAT = 7680
OFF_KVLAT = OFF_QLAT + D_Q_LORA
OFF_KROPE = OFF_KVLAT + D_KV_LORA
OFF_F = OFF_KROPE + LANES
N_P = 9216

AUG = 2 * HEAD_DIM
BAND_Q = 256
A_WIN = BAND_Q + A_PREV_CHUNKS * CHUNK
B_WIN = BAND_Q + B_PREV_CHUNKS * CHUNK


def _cparams(sems, vmem_mb):
    return pltpu.CompilerParams(dimension_semantics=sems,
                                vmem_limit_bytes=vmem_mb * 1024 * 1024)


def _dot(a, b):
    return jnp.dot(a, b, preferred_element_type=F32)


def _dot_nt(a, b):
    return lax.dot_general(a, b, (((1,), (1,)), ((), ())), preferred_element_type=F32)


def _adaln_kernel(c_ref, w_ref, b_ref, o_ref):
    c = c_ref[...]
    s = c * jax.nn.sigmoid(c)
    o_ref[...] = jnp.sum(w_ref[...] * s, axis=0, keepdims=True) + b_ref[...]


def _adaln(c_col, ada_w, ada_b):
    d, n = ada_w.shape
    tn = min(512, n)
    return pl.pallas_call(
        _adaln_kernel,
        grid=(n // tn,),
        in_specs=[pl.BlockSpec((d, 1), lambda j: (0, 0)),
                  pl.BlockSpec((d, tn), lambda j: (0, j)),
                  pl.BlockSpec((1, tn), lambda j: (0, j))],
        out_specs=pl.BlockSpec((1, tn), lambda j: (0, j)),
        out_shape=jax.ShapeDtypeStruct((1, n), F32),
        compiler_params=_cparams(("parallel",), 40),
        name="adaln",
    )(c_col, ada_w, ada_b.reshape(1, n))


def _rope_table_kernel(pos_ref, fb_ref, fd_ref, cb_ref, sb_ref, cd_ref, sd_ref):
    pos = pos_ref[...]
    lane = lax.broadcasted_iota(jnp.int32, (1, LANES), 1)
    ab = pos * fb_ref[...]
    sb = jnp.sin(ab)
    cb_ref[...] = jnp.cos(ab)
    sb_ref[...] = jnp.where(lane < HEAD_DIM // 2, -sb, sb)
    ad = pos * fd_ref[...]
    sd = jnp.sin(ad)
    cd_ref[...] = jnp.where(lane < D_ROPE, jnp.cos(ad), 1.0)
    sd_ref[...] = jnp.where(lane < D_ROPE // 2, -sd, jnp.where(lane < D_ROPE, sd, 0.0))


def _rope_tables(pos_col):
    s = pos_col.shape[0]
    tm = min(512, s)
    lane = np.arange(LANES)
    half_b = HEAD_DIM // 2
    fb = ROPE_THETA ** (-(lane % half_b).astype(np.float64) / half_b)
    half_d = D_ROPE // 2
    fd = np.where(lane < D_ROPE, ROPE_THETA ** (-(lane % half_d).astype(np.float64) / half_d), 0.0)
    fb = jnp.asarray(fb, F32).reshape(1, LANES)
    fd = jnp.asarray(fd, F32).reshape(1, LANES)
    tab = jax.ShapeDtypeStruct((s, LANES), F32)
    row = pl.BlockSpec((tm, LANES), lambda i: (i, 0))
    const = pl.BlockSpec((1, LANES), lambda i: (0, 0))
    return pl.pallas_call(
        _rope_table_kernel,
        grid=(s // tm,),
        in_specs=[pl.BlockSpec((tm, 1), lambda i: (i, 0)), const, const],
        out_specs=[row, row, row, row],
        out_shape=[tab, tab, tab, tab],
        compiler_params=_cparams(("parallel",), 32),
        name="rope_tables",
    )(pos_col, fb, fd)


def _mod_norm(x, g, sc, sh):
    ms = jnp.mean(x * x, axis=-1, keepdims=True)
    y = x * lax.rsqrt(ms + EPS)
    return (y * g) * (1.0 + sc) + sh


def _norm_kernel(x_ref, g_ref, sc_ref, sh_ref, h_ref):
    h_ref[...] = _mod_norm(x_ref[...], g_ref[...], sc_ref[...], sh_ref[...]).astype(h_ref.dtype)


def _norm_router_kernel(x_ref, g_ref, sc_ref, sh_ref, rw_ref, rb_ref, h_ref, comb_ref):
    h = _mod_norm(x_ref[...], g_ref[...], sc_ref[...], sh_ref[...])
    h_ref[...] = h.astype(h_ref.dtype)
    logits = jnp.dot(h, rw_ref[...], precision=lax.Precision.HIGHEST,
                     preferred_element_type=F32) + rb_ref[...]
    lane = lax.broadcasted_iota(jnp.int32, logits.shape, 1).astype(F32)
    work = logits
    vals, sels = [], []
    for _ in range(TOP_K):
        m = jnp.max(work, axis=-1, keepdims=True)
        idx = jnp.min(jnp.where(work == m, lane, float(LANES)), axis=-1, keepdims=True)
        sel = lane == idx
        vals.append(m)
        sels.append(sel)
        work = jnp.where(sel, -3.0e38, work)
    es = [jnp.exp(v - vals[0]) for v in vals]
    inv = 1.0 / (es[0] + es[1] + es[2] + es[3])
    comb = jnp.zeros_like(logits)
    for e, sel in zip(es, sels):
        comb = comb + jnp.where(sel, e * inv, 0.0)
    comb_ref[...] = comb


def _norm(x, g, sc, sh, router=None):
    s, d = x.shape
    tm = min(256, s)
    row = pl.BlockSpec((tm, d), lambda i: (i, 0))
    vec = pl.BlockSpec((1, d), lambda i: (0, 0))
    if router is None:
        return pl.pallas_call(
            _norm_kernel, grid=(s // tm,),
            in_specs=[row, vec, vec, vec], out_specs=row,
            out_shape=jax.ShapeDtypeStruct((s, d), BF16),
            compiler_params=_cparams(("parallel",), 32), name="norm",
        )(x, g, sc, sh)
    rw, rb = router
    return pl.pallas_call(
        _norm_router_kernel, grid=(s // tm,),
        in_specs=[row, vec, vec, vec,
                  pl.BlockSpec((d, LANES), lambda i: (0, 0)),
                  pl.BlockSpec((1, LANES), lambda i: (0, 0))],
        out_specs=[row, pl.BlockSpec((tm, LANES), lambda i: (i, 0))],
        out_shape=[jax.ShapeDtypeStruct((s, d), BF16), jax.ShapeDtypeStruct((s, LANES), F32)],
        compiler_params=_cparams(("parallel",), 40), name="norm_router",
    )(x, g, sc, sh, rw, rb)


def _mm_kernel(a_ref, b_ref, o_ref):
    o_ref[...] = _dot(a_ref[...], b_ref[...]).astype(o_ref.dtype)


def _matmul(a, b, out_dtype, tm=1024, tn=1024):
    m, k = a.shape
    n = b.shape[1]
    tm, tn = min(tm, m), min(tn, n)
    return pl.pallas_call(
        _mm_kernel, grid=(m // tm, n // tn),
        in_specs=[pl.BlockSpec((tm, k), lambda i, j: (i, 0)),
                  pl.BlockSpec((k, tn), lambda i, j: (0, j))],
        out_specs=pl.BlockSpec((tm, tn), lambda i, j: (i, j)),
        out_shape=jax.ShapeDtypeStruct((m, n), out_dtype),
        compiler_params=_cparams(("parallel", "parallel"), 48), name="in_proj",
    )(a, b)


def _prep_kernel(p_ref, gains_ref, gql_ref, gkvl_ref, fb_ref, cb_ref, sb_ref,
                 qa_ref, ka_ref, va_ref, qb_ref, kb_ref, vb_ref,
                 qc_ref, kc_ref, vc_ref, ql_ref, kvl_ref, kr_ref, carry_ref):
    tm = p_ref.shape[0]
    qscale = HEAD_DIM ** -0.5 * LOG2E

    @pl.when(pl.program_id(0) == 0)
    def _():
        carry_ref[...] = jnp.zeros_like(carry_ref)

    def head_norm(off, g_row, scale):
        x = p_ref[:, off:off + HEAD_DIM].astype(F32)
        ms = jnp.mean(x * x, axis=-1, keepdims=True)
        return x * lax.rsqrt(ms + EPS) * (gains_ref[g_row:g_row + 1, :] * scale)

    cb, sb = cb_ref[...], sb_ref[...]

    def rope(y):
        return y * cb + pltpu.roll(y, HEAD_DIM // 2, 1) * sb

    for h in range(A_HEADS):
        c = h * HEAD_DIM
        qa_ref[:, c:c + HEAD_DIM] = head_norm(OFF_QA + c, 0, qscale).astype(BF16)
        ka_ref[:, c:c + HEAD_DIM] = head_norm(OFF_KA + c, 1, 1.0).astype(BF16)
    va_ref[...] = p_ref[:, OFF_VA:OFF_VA + A_HEADS * HEAD_DIM]

    for h in range(B_Q_HEADS):
        c = h * HEAD_DIM
        qb_ref[:, c:c + HEAD_DIM] = rope(head_norm(OFF_QB + c, 2, qscale)).astype(BF16)
    for h in range(B_KV_HEADS):
        c = h * HEAD_DIM
        kb_ref[:, c:c + HEAD_DIM] = rope(head_norm(OFF_KB + c, 3, 1.0)).astype(BF16)
    vb_ref[...] = p_ref[:, OFF_VB:OFF_VB + B_KV_HEADS * HEAD_DIM]

    z = p_ref[:, OFF_F:OFF_F + LANES].astype(F32) + fb_ref[...]
    logf = jnp.minimum(z, 0.0) - jnp.log1p(jnp.exp(-jnp.abs(z)))
    r_i = lax.broadcasted_iota(jnp.int32, (tm, tm), 0)
    c_i = lax.broadcasted_iota(jnp.int32, (tm, tm), 1)
    tri = jnp.where(r_i >= c_i, 1.0, 0.0).astype(F32)
    cum = jnp.dot(tri, logf, precision=lax.Precision.HIGHEST,
                  preferred_element_type=F32) + carry_ref[...]
    carry_ref[...] = cum[tm - 1:tm, :]
    cum2 = cum * LOG2E
    hi = cum2.astype(BF16).astype(F32)
    r1 = cum2 - hi
    mid = r1.astype(BF16).astype(F32)
    lo = r1 - mid
    lane = lax.broadcasted_iota(jnp.int32, (tm, LANES), 1)
    for h in range(C_HEADS):
        c = h * HEAD_DIM
        a = h * AUG
        hi_h, mid_h, lo_h = hi[:, h:h + 1], mid[:, h:h + 1], lo[:, h:h + 1]
        aug_q = jnp.where(lane == 0, hi_h, jnp.where(lane == 1, mid_h, jnp.where(
            lane == 2, lo_h, jnp.where(lane < 6, 1.0, 0.0))))
        aug_k = jnp.where(lane < 3, 1.0, jnp.where(lane == 3, -hi_h, jnp.where(
            lane == 4, -mid_h, jnp.where(lane == 5, -lo_h, 0.0))))
        qc_ref[:, a:a + HEAD_DIM] = head_norm(OFF_QC + c, 4, qscale).astype(BF16)
        qc_ref[:, a + HEAD_DIM:a + AUG] = aug_q.astype(BF16)
        kc_ref[:, a:a + HEAD_DIM] = head_norm(OFF_KC + c, 5, 1.0).astype(BF16)
        kc_ref[:, a + HEAD_DIM:a + AUG] = aug_k.astype(BF16)
    vc_ref[...] = p_ref[:, OFF_VC:OFF_VC + C_HEADS * HEAD_DIM]

    xq = p_ref[:, OFF_QLAT:OFF_QLAT + D_Q_LORA].astype(F32)
    ms = jnp.mean(xq * xq, axis=-1, keepdims=True)
    ql_ref[...] = (xq * lax.rsqrt(ms + EPS) * gql_ref[...]).astype(BF16)
    xkv = p_ref[:, OFF_KVLAT:OFF_KVLAT + D_KV_LORA].astype(F32)
    ms = jnp.mean(xkv * xkv, axis=-1, keepdims=True)
    kvl_ref[...] = (xkv * lax.rsqrt(ms + EPS) * gkvl_ref[...]).astype(BF16)
    kr_ref[...] = p_ref[:, OFF_KROPE:OFF_KROPE + LANES].astype(F32)


def _prep(p, gains, gql, gkvl, fb, cos_b, sin_b):
    s = p.shape[0]
    tm = min(256, s)

    def row(w):
        return pl.BlockSpec((tm, w), lambda i: (i, 0))

    def const(r, w):
        return pl.BlockSpec((r, w), lambda i: (0, 0))

    widths = [1024, 1024, 1024, 1024, 256, 256, C_HEADS * AUG, C_HEADS * AUG, 1024,
              D_Q_LORA, D_KV_LORA]
    out_shape = [jax.ShapeDtypeStruct((s, w), BF16) for w in widths]
    out_shape.append(jax.ShapeDtypeStruct((s, LANES), F32))
    out_specs = [row(w) for w in widths] + [row(LANES)]
    return pl.pallas_call(
        _prep_kernel, grid=(s // tm,),
        in_specs=[row(N_P), const(8, LANES), const(1, D_Q_LORA), const(1, D_KV_LORA),
                  const(1, LANES), row(LANES), row(LANES)],
        out_specs=out_specs, out_shape=out_shape,
        scratch_shapes=[pltpu.VMEM((1, LANES), F32)],
        compiler_params=_cparams(("arbitrary",), 48), name="prep",
    )(p, gains, gql, gkvl, fb, cos_b, sin_b)


def _rope_d(y, cd, sd):
    lane = lax.broadcasted_iota(jnp.int32, y.shape, 1)
    half = D_ROPE // 2
    partner = jnp.where(lane < half, pltpu.roll(y, LANES - half, 1), pltpu.roll(y, half, 1))
    return y * cd + partner * sd


def _prep_d_kernel(ql_ref, kvl_ref, kr_ref, wq_ref, wkv_ref, gq_ref, gk_ref, cd_ref, sd_ref,
                   qd_ref, kd_ref, vd_ref):
    qscale = D_QK ** -0.5 * LOG2E
    cd, sd = cd_ref[...], sd_ref[...]
    q = _dot(ql_ref[...], wq_ref[...])
    kv = _dot(kvl_ref[...], wkv_ref[...])
    kr = kr_ref[...]
    ss_r = jnp.sum(kr * kr, axis=-1, keepdims=True)
    gq = gq_ref[...]
    gk = gk_ref[...]
    for h in range(D_HEADS):
        a = h * AUG
        qn, qr = q[:, a:a + HEAD_DIM], q[:, a + HEAD_DIM:a + AUG]
        ms = (jnp.sum(qn * qn, axis=-1, keepdims=True)
              + jnp.sum(qr * qr, axis=-1, keepdims=True)) * (1.0 / D_QK)
        r = lax.rsqrt(ms + EPS) * qscale
        qd_ref[:, a:a + HEAD_DIM] = (qn * r * gq[:, :HEAD_DIM]).astype(BF16)
        qd_ref[:, a + HEAD_DIM:a + AUG] = _rope_d(qr * r * gq[:, HEAD_DIM:], cd, sd).astype(BF16)
        kn = kv[:, h * D_NOPE:(h + 1) * D_NOPE]
        ms = (jnp.sum(kn * kn, axis=-1, keepdims=True) + ss_r) * (1.0 / D_QK)
        r = lax.rsqrt(ms + EPS)
        kd_ref[:, a:a + HEAD_DIM] = (kn * r * gk[:, :HEAD_DIM]).astype(BF16)
        kd_ref[:, a + HEAD_DIM:a + AUG] = _rope_d(kr * r * gk[:, HEAD_DIM:], cd, sd).astype(BF16)
    vd_ref[...] = kv[:, D_HEADS * D_NOPE:].astype(BF16)


def _prep_d(ql, kvl, kr, wq, wkv, gq, gk, cos_d, sin_d):
    s = ql.shape[0]
    tm = min(256, s)

    def row(w):
        return pl.BlockSpec((tm, w), lambda i: (i, 0))

    def const(r, w):
        return pl.BlockSpec((r, w), lambda i: (0, 0))

    return pl.pallas_call(
        _prep_d_kernel, grid=(s // tm,),
        in_specs=[row(D_Q_LORA), row(D_KV_LORA), row(LANES),
                  const(D_Q_LORA, D_HEADS * AUG), const(D_KV_LORA, 2 * D_HEADS * D_NOPE),
                  const(1, AUG), const(1, AUG), row(LANES), row(LANES)],
        out_specs=[row(D_HEADS * AUG), row(D_HEADS * AUG), row(D_HEADS * D_V)],
        out_shape=[jax.ShapeDtypeStruct((s, D_HEADS * AUG), BF16),
                   jax.ShapeDtypeStruct((s, D_HEADS * AUG), BF16),
                   jax.ShapeDtypeStruct((s, D_HEADS * D_V), BF16)],
        compiler_params=_cparams(("parallel",), 40), name="prep_d",
    )(ql, kvl, kr, wq, wkv, gq, gk, cos_d, sin_d)


def _flash_kernel(q_ref, k_ref, v_ref, o_ref, m_ref, l_ref, acc_ref, *, blk, unit):
    i = pl.program_id(1)
    q = q_ref[...]
    m_ref[...] = jnp.full_like(m_ref, NEG_INF)
    l_ref[...] = jnp.zeros_like(l_ref)
    acc_ref[...] = jnp.zeros_like(acc_ref)

    def step(j, diagonal):
        start = pl.multiple_of(j * blk, blk)
        k = k_ref[pl.ds(start, blk), :]
        v = v_ref[pl.ds(start, blk), :]
        s = _dot_nt(q, k)
        if diagonal:
            qi = lax.broadcasted_iota(jnp.int32, s.shape, 0)
            ki = lax.broadcasted_iota(jnp.int32, s.shape, 1)
            if unit > 1:
                shift = unit.bit_length() - 1
                qi, ki = qi >> shift, ki >> shift
            s = jnp.where(ki <= qi, s, NEG_INF)
        m_prev = m_ref[...]
        m_new = jnp.maximum(m_prev, jnp.max(s, axis=-1, keepdims=True))
        alpha = jnp.exp2(m_prev - m_new)
        p = jnp.exp2(s - m_new)
        l_ref[...] = alpha * l_ref[...] + jnp.sum(p, axis=-1, keepdims=True)
        acc_ref[...] = alpha * acc_ref[...] + _dot(p.astype(BF16), v)
        m_ref[...] = m_new

    def body(j, carry):
        step(j, False)
        return carry

    lax.fori_loop(0, i, body, 0)
    step(i, True)
    o_ref[...] = (acc_ref[...] / l_ref[...]).astype(o_ref.dtype)


def _flash(q, k, v, n_heads, unit):
    s = q.shape[0]
    blk = min(512, s)
    dv = v.shape[1] // n_heads
    return pl.pallas_call(
        functools.partial(_flash_kernel, blk=blk, unit=unit),
        grid=(n_heads, s // blk),
        in_specs=[pl.BlockSpec((blk, AUG), lambda h, i: (i, h)),
                  pl.BlockSpec((s, AUG), lambda h, i: (0, h)),
                  pl.BlockSpec((s, dv), lambda h, i: (0, h))],
        out_specs=pl.BlockSpec((blk, dv), lambda h, i: (i, h)),
        out_shape=jax.ShapeDtypeStruct((s, n_heads * dv), BF16),
        scratch_shapes=[pltpu.VMEM((blk, 1), F32), pltpu.VMEM((blk, 1), F32),
                        pltpu.VMEM((blk, dv), F32)],
        compiler_params=_cparams(("parallel", "arbitrary"), 48), name=f"flash_u{unit}",
    )(q, k, v)


def _attn_a_kernel(q_ref, k_ref, v_ref, t_ref, o_ref):
    i = pl.program_id(1)
    start = pl.multiple_of(jnp.maximum(i * BAND_Q - A_PREV_CHUNKS * CHUNK, 0), BAND_Q)
    k = k_ref[pl.ds(start, A_WIN), :]
    v = v_ref[pl.ds(start, A_WIN), :]
    s = _dot_nt(q_ref[...], k) + t_ref[...]
    m = jnp.max(s, axis=-1, keepdims=True)
    p = jnp.exp2(s - m)
    l = jnp.sum(p, axis=-1, keepdims=True)
    o_ref[...] = (_dot(p.astype(BF16), v) / l).astype(o_ref.dtype)


def _attn_a(q, k, v, table):
    s = q.shape[0]
    nb = table.shape[0] - 1
    return pl.pallas_call(
        _attn_a_kernel, grid=(A_HEADS, s // BAND_Q),
        in_specs=[pl.BlockSpec((BAND_Q, HEAD_DIM), lambda h, i: (i, h)),
                  pl.BlockSpec((s, HEAD_DIM), lambda h, i: (0, h)),
                  pl.BlockSpec((s, HEAD_DIM), lambda h, i: (0, h)),
                  pl.BlockSpec((None, None, BAND_Q, A_WIN),
                               lambda h, i: (jnp.minimum(i, nb), h, 0, 0))],
        out_specs=pl.BlockSpec((BAND_Q, HEAD_DIM), lambda h, i: (i, h)),
        out_shape=jax.ShapeDtypeStruct((s, A_HEADS * HEAD_DIM), BF16),
        compiler_params=_cparams(("parallel", "arbitrary"), 40), name="attn_a",
    )(q, k, v, table)


def _a_table(rel_bias):
    nb = A_PREV_CHUNKS * CHUNK // BAND_Q + 1
    t = (np.arange(nb)[:, None, None] * BAND_Q + np.arange(BAND_Q)[None, :, None])
    w = np.arange(A_WIN)[None, None, :]
    dchunk = t // CHUNK - w // CHUNK
    valid = (dchunk >= 0) & (dchunk <= A_PREV_CHUNKS)
    idx = np.clip(t - w, -REL_CLIP, REL_CLIP) + REL_CLIP
    bias = rel_bias.astype(F32)[jnp.asarray(idx)] * LOG2E
    tab = jnp.where(jnp.asarray(valid)[..., None], bias, NEG_INF)
    return tab.transpose(0, 3, 1, 2)


def _attn_b_kernel(sink_ref, q_ref, k_ref, v_ref, o_ref):
    g = pl.program_id(0)
    i = pl.program_id(1)
    group = B_Q_HEADS // B_KV_HEADS
    start = pl.multiple_of(jnp.maximum(i * BAND_Q - B_PREV_CHUNKS * CHUNK, 0), LANES)
    k = k_ref[pl.ds(start, B_WIN), :]
    v = v_ref[pl.ds(start, B_WIN), :]
    shift = CHUNK.bit_length() - 1
    t = (i * BAND_Q + lax.broadcasted_iota(jnp.int32, (BAND_Q, B_WIN), 0)) >> shift
    kp = (start + lax.broadcasted_iota(jnp.int32, (BAND_Q, B_WIN), 1)) >> shift
    valid = jnp.abs(t - kp - 1) <= 1
    for hi in range(group):
        c = hi * HEAD_DIM
        s = jnp.where(valid, _dot_nt(q_ref[:, c:c + HEAD_DIM], k), NEG_INF)
        sink = sink_ref[g * group + hi] * LOG2E
        m = jnp.maximum(jnp.max(s, axis=-1, keepdims=True), sink)
        p = jnp.exp2(s - m)
        l = jnp.sum(p, axis=-1, keepdims=True) + jnp.exp2(sink - m)
        o_ref[:, c:c + HEAD_DIM] = (_dot(p.astype(BF16), v) / l).astype(o_ref.dtype)


def _attn_b(q, k, v, sinks):
    s = q.shape[0]
    gw = (B_Q_HEADS // B_KV_HEADS) * HEAD_DIM
    return pl.pallas_call(
        _attn_b_kernel, grid=(B_KV_HEADS, s // BAND_Q),
        in_specs=[pl.BlockSpec(memory_space=pltpu.SMEM),
                  pl.BlockSpec((BAND_Q, gw), lambda g, i: (i, g)),
                  pl.BlockSpec((s, HEAD_DIM), lambda g, i: (0, g)),
                  pl.BlockSpec((s, HEAD_DIM), lambda g, i: (0, g))],
        out_specs=pl.BlockSpec((BAND_Q, gw), lambda g, i: (i, g)),
        out_shape=jax.ShapeDtypeStruct((s, B_Q_HEADS * HEAD_DIM), BF16),
        compiler_params=_cparams(("parallel", "arbitrary"), 40), name="attn_b",
    )(sinks, q, k, v)


def _merge_kernel(h_ref, o_ref, wg_ref, bg_ref, wb_ref, out_ref, acc_ref):
    b = pl.program_id(2)
    gate = jax.nn.sigmoid(_dot(h_ref[...], wg_ref[...]) + bg_ref[...])
    val = gate * _dot(o_ref[...], wb_ref[...])

    @pl.when(b == 0)
    def _():
        acc_ref[...] = val

    @pl.when(b > 0)
    def _():
        acc_ref[...] += val

    @pl.when(b == N_BRANCH - 1)
    def _():
        out_ref[...] = acc_ref[...].astype(out_ref.dtype)


def _merge(h, o_all, wg, bg, wb):
    s, d = h.shape
    tm, tn = min(1024, s), min(512, d)
    return pl.pallas_call(
        _merge_kernel, grid=(s // tm, d // tn, N_BRANCH),
        in_specs=[pl.BlockSpec((tm, d), lambda i, j, b: (i, 0)),
                  pl.BlockSpec((None, tm, BRANCH_W), lambda i, j, b: (b, i, 0)),
                  pl.BlockSpec((None, d, tn), lambda i, j, b: (b, 0, j)),
                  pl.BlockSpec((None, 1, tn), lambda i, j, b: (b, 0, j)),
                  pl.BlockSpec((None, BRANCH_W, tn), lambda i, j, b: (b, 0, j))],
        out_specs=pl.BlockSpec((tm, tn), lambda i, j, b: (i, j)),
        out_shape=jax.ShapeDtypeStruct((s, d), BF16),
        scratch_shapes=[pltpu.VMEM((tm, tn), F32)],
        compiler_params=_cparams(("parallel", "parallel", "arbitrary"), 56), name="merge",
    )(h, o_all, wg, bg, wb)


def _resid_kernel(a_ref, w_ref, x_ref, g_ref, o_ref):
    o_ref[...] = x_ref[...] + g_ref[...] * _dot(a_ref[...], w_ref[...])


def _resid_moe_kernel(a_ref, w_ref, c_ref, bd_ref, x_ref, g_ref, o_ref):
    y = _dot(a_ref[...], w_ref[...]) + _dot(c_ref[...].astype(BF16), bd_ref[...])
    o_ref[...] = x_ref[...] + g_ref[...] * y


def _resid(a, w, x, gate, comb=None, b_dn=None):
    s, k = a.shape
    d = w.shape[1]
    tm, tn = min(1024, s), min(512, d)
    a_spec = pl.BlockSpec((tm, k), lambda i, j: (i, 0))
    w_spec = pl.BlockSpec((k, tn), lambda i, j: (0, j))
    x_spec = pl.BlockSpec((tm, tn), lambda i, j: (i, j))
    g_spec = pl.BlockSpec((1, tn), lambda i, j: (0, j))
    if comb is None:
        kern, ins = _resid_kernel, (a, w, x, gate)
        in_specs = [a_spec, w_spec, x_spec, g_spec]
    else:
        kern, ins = _resid_moe_kernel, (a, w, comb, b_dn, x, gate)
        in_specs = [a_spec, w_spec, pl.BlockSpec((tm, LANES), lambda i, j: (i, 0)),
                    pl.BlockSpec((LANES, tn), lambda i, j: (0, j)), x_spec, g_spec]
    return pl.pallas_call(
        kern, grid=(s // tm, d // tn), in_specs=in_specs, out_specs=x_spec,
        out_shape=jax.ShapeDtypeStruct((s, d), F32),
        compiler_params=_cparams(("parallel", "parallel"), 48), name="resid",
    )(*ins)


def _moe_up_kernel(h_ref, w_ref, b_ref, c_ref, o_ref, *, eb):
    j = pl.program_id(1)
    h = h_ref[...]
    comb = c_ref[...]
    lane = lax.broadcasted_iota(jnp.int32, comb.shape, 1)
    for e in range(eb):
        gu = _dot(h, w_ref[e]) + b_ref[e]
        glu = jnp.minimum(gu[:, :D_EXPERT], SWIGLU_LIMIT)
        lin = jnp.clip(gu[:, D_EXPERT:], -SWIGLU_LIMIT, SWIGLU_LIMIT)
        act = glu * jax.nn.sigmoid(SWIGLU_ALPHA * glu) * (lin + 1.0)
        ce = jnp.sum(jnp.where(lane == j * eb + e, comb, 0.0), axis=-1, keepdims=True)
        o_ref[:, e * D_EXPERT:(e + 1) * D_EXPERT] = (act * ce).astype(o_ref.dtype)


def _moe_up(h, w_gu, b_gu, comb):
    s, d = h.shape
    tm, eb = min(1024, s), 4
    return pl.pallas_call(
        functools.partial(_moe_up_kernel, eb=eb), grid=(s // tm, N_EXPERTS // eb),
        in_specs=[pl.BlockSpec((tm, d), lambda i, j: (i, 0)),
                  pl.BlockSpec((eb, d, 2 * D_EXPERT), lambda i, j: (j, 0, 0)),
                  pl.BlockSpec((eb, 1, 2 * D_EXPERT), lambda i, j: (j, 0, 0)),
                  pl.BlockSpec((tm, LANES), lambda i, j: (i, 0))],
        out_specs=pl.BlockSpec((tm, eb * D_EXPERT), lambda i, j: (i, j)),
        out_shape=jax.ShapeDtypeStruct((s, N_EXPERTS * D_EXPERT), BF16),
        compiler_params=_cparams(("parallel", "parallel"), 48), name="moe_up",
    )(h, w_gu, b_gu, comb)


def _reorder_w_in(w):
    d = w.shape[0]
    f0 = OFF_QLAT
    q0 = f0 + C_HEADS
    zeros = lambda n: jnp.zeros((d, n), w.dtype)
    return jnp.concatenate(
        [w[:, :f0], w[:, q0:], zeros(LANES - D_ROPE), w[:, f0:q0],
         zeros(N_P - OFF_F - C_HEADS)], axis=1)


def _reorder_w_q_b(w):
    r = w.shape[0]
    w = w.reshape(r, D_HEADS, D_QK)
    w = jnp.pad(w, ((0, 0), (0, 0), (0, AUG - D_QK)))
    return w.reshape(r, D_HEADS * AUG)


def _reorder_w_kv_b(w):
    r = w.shape[0]
    w = w.reshape(r, D_HEADS, D_NOPE + D_V)
    return jnp.concatenate([w[:, :, :D_NOPE].reshape(r, -1), w[:, :, D_NOPE:].reshape(r, -1)], axis=1)


def _pad_lanes(v, width, value=0.0):
    return jnp.pad(v, (0, width - v.shape[0]), constant_values=value).reshape(1, width)


def kernel(x, c, positions, ada_w, ada_b, ada_layer, norm1_g, norm2_g, w_in, a_q_norm, a_k_norm, a_rel_bias, b_q_norm, b_k_norm, b_sinks, c_q_norm, c_k_norm, c_f_bias, d_q_a_norm, d_w_q_b, d_kv_a_norm, d_w_kv_b, d_q_norm, d_k_norm, w_branch, w_gate, b_gate, w_out, router_w, router_b, w_gu, b_gu, w_dn, b_dn):
    batch, s, d = x.shape
    assert batch == 1, "kernels are written for a single sequence"
    depth = w_in.shape[0]
    xs = x.reshape(s, d)

    base_mod = _adaln(c.reshape(d, 1), ada_w, ada_b).reshape(6, d)
    cos_b, sin_b, cos_d, sin_d = _rope_tables(positions.reshape(s, 1).astype(F32))

    for l in range(depth):
        mod = base_mod + ada_layer[l]
        shift1, scale1, gate1, shift2, scale2, gate2 = (mod[j:j + 1] for j in range(6))

        h = _norm(xs, norm1_g[l].reshape(1, d), scale1, shift1)
        p = _matmul(h, _reorder_w_in(w_in[l]).astype(BF16), BF16)
        gains = jnp.stack([a_q_norm[l], a_k_norm[l], b_q_norm[l], b_k_norm[l],
                           c_q_norm[l], c_k_norm[l], jnp.zeros_like(a_q_norm[l]),
                           jnp.zeros_like(a_q_norm[l])])
        (qa, ka, va, qb, kb, vb, qc, kc, vc, ql, kvl, kr) = _prep(
            p, gains, d_q_a_norm[l].reshape(1, -1), d_kv_a_norm[l].reshape(1, -1),
            _pad_lanes(c_f_bias[l], LANES), cos_b, sin_b)
        qd, kd, vd = _prep_d(
            ql, kvl, kr, _reorder_w_q_b(d_w_q_b[l]).astype(BF16),
            _reorder_w_kv_b(d_w_kv_b[l]).astype(BF16),
            _pad_lanes(d_q_norm[l], AUG), _pad_lanes(d_k_norm[l], AUG), cos_d, sin_d)

        o_a = _attn_a(qa, ka, va, _a_table(a_rel_bias[l]))
        o_b = _attn_b(qb, kb, vb, b_sinks[l].astype(F32))
        o_c = _flash(qc, kc, vc, C_HEADS, 1)
        o_d = _flash(qd, kd, vd, D_HEADS, CHUNK)

        merged = _merge(h, jnp.stack([o_a, o_b, o_c, o_d]), w_gate[l].astype(BF16),
                        b_gate[l].reshape(N_BRANCH, 1, d), w_branch[l].astype(BF16))
        xs = _resid(merged, w_out[l].astype(BF16), xs, gate1)

        rw = jnp.pad(router_w[l], ((0, 0), (0, LANES - N_EXPERTS)))
        rb = _pad_lanes(router_b[l].astype(F32), LANES, NEG_INF)
        h2, comb = _norm(xs, norm2_g[l].reshape(1, d), scale2, shift2, router=(rw, rb))
        act = _moe_up(h2, w_gu[l].astype(BF16), b_gu[l].reshape(N_EXPERTS, 1, 2 * D_EXPERT), comb)
        bdn = jnp.pad(b_dn[l], ((0, LANES - N_EXPERTS), (0, 0))).astype(BF16)
        xs = _resid(act, w_dn[l].reshape(N_EXPERTS * D_EXPERT, d).astype(BF16), xs, gate2,
                    comb=comb, b_dn=bdn)
    return xs.reshape(batch, s, d)
```

```python
import functools
import math

import numpy as np
import jax
import jax.numpy as jnp
from jax import lax
from jax.experimental import pallas as pl
from jax.experimental.pallas import tpu as pltpu

F32 = jnp.float32
BF16 = jnp.bfloat16

CHUNK = 64
HEAD_DIM = 128
N_BRANCH = 4
BRANCH_W = 1024
ROPE_THETA = 10000.0
EPS = 1e-6
NEG_INF = -1e30
A_HEADS = 8
A_PREV_CHUNKS = 8
REL_CLIP = 256
B_Q_HEADS = 8
B_KV_HEADS = 2
B_PREV_CHUNKS = 2
C_HEADS = 8
D_HEADS = 8
D_Q_LORA = 896
D_KV_LORA = 256
D_NOPE = 128
D_ROPE = 64
D_V = 128
D_QK = D_NOPE + D_ROPE
N_EXPERTS = 32
TOP_K = 4
D_EXPERT = 128
SWIGLU_LIMIT = 7.0
SWIGLU_ALPHA = 1.702

LANES = 128
V7X_VMEM_BYTES = 64 * 1024 * 1024

LOG2E = math.log2(math.e)

OFF_QA, OFF_KA, OFF_VA = 0, 1024, 2048
OFF_QB, OFF_KB, OFF_VB = 3072, 4096, 4352
OFF_QC, OFF_KC, OFF_VC = 4608, 5632, 6656
OFF_QLAT = 7680
OFF_KVLAT = OFF_QLAT + D_Q_LORA
OFF_KROPE = OFF_KVLAT + D_KV_LORA
OFF_F = OFF_KROPE + LANES
N_P = 9216

AUG = 2 * HEAD_DIM
FLASH_BLK = 1024
FLASH_ROW_CHUNK = 32
PREP_ROWS = 256
BAND_Q = 256
A_WIN = BAND_Q + A_PREV_CHUNKS * CHUNK
B_WIN = BAND_Q + B_PREV_CHUNKS * CHUNK


def _cparams(sems, vmem_mb):
    return pltpu.CompilerParams(dimension_semantics=sems,
                                vmem_limit_bytes=vmem_mb * 1024 * 1024)


def _dot(a, b):
    return jnp.dot(a, b, preferred_element_type=F32)


def _dot_nt(a, b):
    return lax.dot_general(a, b, (((1,), (1,)), ((), ())), preferred_element_type=F32)


def _adaln_kernel(c_ref, w_ref, b_ref, o_ref):
    c = c_ref[...]
    s = c * jax.nn.sigmoid(c)
    o_ref[...] = jnp.sum(w_ref[...] * s, axis=0, keepdims=True) + b_ref[...]


def _adaln(c_col, ada_w, ada_b):
    d, n = ada_w.shape
    tn = min(512, n)
    return pl.pallas_call(
        _adaln_kernel,
        grid=(n // tn,),
        in_specs=[pl.BlockSpec((d, 1), lambda j: (0, 0)),
                  pl.BlockSpec((d, tn), lambda j: (0, j)),
                  pl.BlockSpec((1, tn), lambda j: (0, j))],
        out_specs=pl.BlockSpec((1, tn), lambda j: (0, j)),
        out_shape=jax.ShapeDtypeStruct((1, n), F32),
        compiler_params=_cparams(("parallel",), 40),
        name="adaln",
    )(c_col, ada_w, ada_b.reshape(1, n))


def _rope_table_kernel(pos_ref, fb_ref, fd_ref, cb_ref, sb_ref, cd_ref, sd_ref):
    pos = pos_ref[...]
    lane = lax.broadcasted_iota(jnp.int32, (1, LANES), 1)
    ab = pos * fb_ref[...]
    sb = jnp.sin(ab)
    cb_ref[...] = jnp.cos(ab)
    sb_ref[...] = jnp.where(lane < HEAD_DIM // 2, -sb, sb)
    ad = pos * fd_ref[...]
    sd = jnp.sin(ad)
    cd_ref[...] = jnp.where(lane < D_ROPE, jnp.cos(ad), 1.0)
    sd_ref[...] = jnp.where(lane < D_ROPE // 2, -sd, jnp.where(lane < D_ROPE, sd, 0.0))


def _rope_tables(pos_col):
    s = pos_col.shape[0]
    tm = min(512, s)
    lane = np.arange(LANES)
    half_b = HEAD_DIM // 2
    fb = ROPE_THETA ** (-(lane % half_b).astype(np.float64) / half_b)
    half_d = D_ROPE // 2
    fd = np.where(lane < D_ROPE, ROPE_THETA ** (-(lane % half_d).astype(np.float64) / half_d), 0.0)
    fb = jnp.asarray(fb, F32).reshape(1, LANES)
    fd = jnp.asarray(fd, F32).reshape(1, LANES)
    tab = jax.ShapeDtypeStruct((s, LANES), F32)
    row = pl.BlockSpec((tm, LANES), lambda i: (i, 0))
    const = pl.BlockSpec((1, LANES), lambda i: (0, 0))
    return pl.pallas_call(
        _rope_table_kernel,
        grid=(s // tm,),
        in_specs=[pl.BlockSpec((tm, 1), lambda i: (i, 0)), const, const],
        out_specs=[row, row, row, row],
        out_shape=[tab, tab, tab, tab],
        compiler_params=_cparams(("parallel",), 32),
        name="rope_tables",
    )(pos_col, fb, fd)


def _mod_norm(x, g, sc, sh):
    ms = jnp.mean(x * x, axis=-1, keepdims=True)
    y = x * lax.rsqrt(ms + EPS)
    return (y * g) * (1.0 + sc) + sh


def _norm_kernel(x_ref, g_ref, sc_ref, sh_ref, h_ref):
    h_ref[...] = _mod_norm(x_ref[...], g_ref[...], sc_ref[...], sh_ref[...]).astype(h_ref.dtype)


def _norm_router_kernel(x_ref, g_ref, sc_ref, sh_ref, rw_ref, rb_ref, h_ref, comb_ref):
    h = _mod_norm(x_ref[...], g_ref[...], sc_ref[...], sh_ref[...])
    h_ref[...] = h.astype(h_ref.dtype)
    logits = jnp.dot(h, rw_ref[...], precision=lax.Precision.HIGHEST,
                     preferred_element_type=F32) + rb_ref[...]
    lane = lax.broadcasted_iota(jnp.int32, logits.shape, 1).astype(F32)
    work = logits
    vals, sels = [], []
    for _ in range(TOP_K):
        m = jnp.max(work, axis=-1, keepdims=True)
        idx = jnp.min(jnp.where(work == m, lane, float(LANES)), axis=-1, keepdims=True)
        sel = lane == idx
        vals.append(m)
        sels.append(sel)
        work = jnp.where(sel, -3.0e38, work)
    es = [jnp.exp(v - vals[0]) for v in vals]
    inv = 1.0 / (es[0] + es[1] + es[2] + es[3])
    comb = jnp.zeros_like(logits)
    for e, sel in zip(es, sels):
        comb = comb + jnp.where(sel, e * inv, 0.0)
    comb_ref[...] = comb


def _norm(x, g, sc, sh, router=None):
    s, d = x.shape
    tm = min(256, s)
    row = pl.BlockSpec((tm, d), lambda i: (i, 0))
    vec = pl.BlockSpec((1, d), lambda i: (0, 0))
    if router is None:
        return pl.pallas_call(
            _norm_kernel, grid=(s // tm,),
            in_specs=[row, vec, vec, vec], out_specs=row,
            out_shape=jax.ShapeDtypeStruct((s, d), BF16),
            compiler_params=_cparams(("parallel",), 32), name="norm",
        )(x, g, sc, sh)
    rw, rb = router
    return pl.pallas_call(
        _norm_router_kernel, grid=(s // tm,),
        in_specs=[row, vec, vec, vec,
                  pl.BlockSpec((d, LANES), lambda i: (0, 0)),
                  pl.BlockSpec((1, LANES), lambda i: (0, 0))],
        out_specs=[row, pl.BlockSpec((tm, LANES), lambda i: (i, 0))],
        out_shape=[jax.ShapeDtypeStruct((s, d), BF16), jax.ShapeDtypeStruct((s, LANES), F32)],
        compiler_params=_cparams(("parallel",), 40), name="norm_router",
    )(x, g, sc, sh, rw, rb)


def _mm_kernel(a_ref, b_ref, o_ref):
    o_ref[...] = _dot(a_ref[...], b_ref[...]).astype(o_ref.dtype)


def _matmul(a, b, out_dtype, tm=1024, tn=1024):
    m, k = a.shape
    n = b.shape[1]
    tm, tn = min(tm, m), min(tn, n)
    return pl.pallas_call(
        _mm_kernel, grid=(m // tm, n // tn),
        in_specs=[pl.BlockSpec((tm, k), lambda i, j: (i, 0)),
                  pl.BlockSpec((k, tn), lambda i, j: (0, j))],
        out_specs=pl.BlockSpec((tm, tn), lambda i, j: (i, j)),
        out_shape=jax.ShapeDtypeStruct((m, n), out_dtype),
        compiler_params=_cparams(("parallel", "parallel"), 48), name="in_proj",
    )(a, b)


def _prep_kernel(p_ref, gains_ref, gql_ref, gkvl_ref, fb_ref, cb_ref, sb_ref,
                 qa_ref, ka_ref, va_ref, qb_ref, kb_ref, vb_ref,
                 qc_ref, kc_ref, vc_ref, ql_ref, kvl_ref, kr_ref, carry_ref):
    tm = p_ref.shape[0]
    qscale = HEAD_DIM ** -0.5 * LOG2E

    @pl.when(pl.program_id(0) == 0)
    def _():
        carry_ref[...] = jnp.zeros_like(carry_ref)

    def head_norm(off, g_row, scale):
        x = p_ref[:, off:off + HEAD_DIM].astype(F32)
        ms = jnp.mean(x * x, axis=-1, keepdims=True)
        return x * lax.rsqrt(ms + EPS) * (gains_ref[g_row:g_row + 1, :] * scale)

    cb, sb = cb_ref[...], sb_ref[...]

    def rope(y):
        return y * cb + pltpu.roll(y, HEAD_DIM // 2, 1) * sb

    for h in range(A_HEADS):
        c = h * HEAD_DIM
        qa_ref[:, c:c + HEAD_DIM] = head_norm(OFF_QA + c, 0, qscale).astype(BF16)
        ka_ref[:, c:c + HEAD_DIM] = head_norm(OFF_KA + c, 1, 1.0).astype(BF16)
    va_ref[...] = p_ref[:, OFF_VA:OFF_VA + A_HEADS * HEAD_DIM]

    for h in range(B_Q_HEADS):
        c = h * HEAD_DIM
        qb_ref[:, c:c + HEAD_DIM] = rope(head_norm(OFF_QB + c, 2, qscale)).astype(BF16)
    for h in range(B_KV_HEADS):
        c = h * HEAD_DIM
        kb_ref[:, c:c + HEAD_DIM] = rope(head_norm(OFF_KB + c, 3, 1.0)).astype(BF16)
    vb_ref[...] = p_ref[:, OFF_VB:OFF_VB + B_KV_HEADS * HEAD_DIM]

    z = p_ref[:, OFF_F:OFF_F + LANES].astype(F32) + fb_ref[...]
    logf = jnp.minimum(z, 0.0) - jnp.log1p(jnp.exp(-jnp.abs(z)))
    r_i = lax.broadcasted_iota(jnp.int32, (tm, tm), 0)
    c_i = lax.broadcasted_iota(jnp.int32, (tm, tm), 1)
    tri = jnp.where(r_i >= c_i, 1.0, 0.0).astype(F32)
    cum = jnp.dot(tri, logf, precision=lax.Precision.HIGHEST,
                  preferred_element_type=F32) + carry_ref[...]
    carry_ref[...] = cum[tm - 1:tm, :]
    cum2 = cum * LOG2E
    hi = cum2.astype(BF16).astype(F32)
    r1 = cum2 - hi
    mid = r1.astype(BF16).astype(F32)
    lo = r1 - mid
    lane = lax.broadcasted_iota(jnp.int32, (tm, LANES), 1)
    for h in range(C_HEADS):
        c = h * HEAD_DIM
        a = h * AUG
        hi_h, mid_h, lo_h = hi[:, h:h + 1], mid[:, h:h + 1], lo[:, h:h + 1]
        aug_q = jnp.where(lane == 0, hi_h, jnp.where(lane == 1, mid_h, jnp.where(
            lane == 2, lo_h, jnp.where(lane < 6, 1.0, 0.0))))
        aug_k = jnp.where(lane < 3, 1.0, jnp.where(lane == 3, -hi_h, jnp.where(
            lane == 4, -mid_h, jnp.where(lane == 5, -lo_h, 0.0))))
        qc_ref[:, a:a + HEAD_DIM] = head_norm(OFF_QC + c, 4, qscale).astype(BF16)
        qc_ref[:, a + HEAD_DIM:a + AUG] = aug_q.astype(BF16)
        kc_ref[h, :HEAD_DIM, :] = head_norm(OFF_KC + c, 5, 1.0).T.astype(BF16)
        kc_ref[h, HEAD_DIM:, :] = aug_k.T.astype(BF16)
    vc_ref[...] = p_ref[:, OFF_VC:OFF_VC + C_HEADS * HEAD_DIM]

    xq = p_ref[:, OFF_QLAT:OFF_QLAT + D_Q_LORA].astype(F32)
    ms = jnp.mean(xq * xq, axis=-1, keepdims=True)
    ql_ref[...] = (xq * lax.rsqrt(ms + EPS) * gql_ref[...]).astype(BF16)
    xkv = p_ref[:, OFF_KVLAT:OFF_KVLAT + D_KV_LORA].astype(F32)
    ms = jnp.mean(xkv * xkv, axis=-1, keepdims=True)
    kvl_ref[...] = (xkv * lax.rsqrt(ms + EPS) * gkvl_ref[...]).astype(BF16)
    kr_ref[...] = p_ref[:, OFF_KROPE:OFF_KROPE + LANES].astype(F32)


def _kt_shape_spec(n_heads, s, tm):
    blk = min(FLASH_BLK, s)
    per = blk // tm
    shape = jax.ShapeDtypeStruct((n_heads, s // blk, AUG, blk), BF16)
    spec = pl.BlockSpec((n_heads, None, AUG, tm), lambda i: (0, i // per, 0, i % per))
    return shape, spec


def _prep(p, gains, gql, gkvl, fb, cos_b, sin_b):
    s = p.shape[0]
    tm = min(PREP_ROWS, s)

    def row(w):
        return pl.BlockSpec((tm, w), lambda i: (i, 0))

    def const(r, w):
        return pl.BlockSpec((r, w), lambda i: (0, 0))

    widths = [1024, 1024, 1024, 1024, 256, 256, C_HEADS * AUG, None, 1024,
              D_Q_LORA, D_KV_LORA]
    out_shape = [jax.ShapeDtypeStruct((s, w), BF16) for w in widths if w]
    out_specs = [row(w) for w in widths if w]
    kt_shape, kt_spec = _kt_shape_spec(C_HEADS, s, tm)
    out_shape.insert(7, kt_shape)
    out_specs.insert(7, kt_spec)
    out_shape.append(jax.ShapeDtypeStruct((s, LANES), F32))
    out_specs.append(row(LANES))
    return pl.pallas_call(
        _prep_kernel, grid=(s // tm,),
        in_specs=[row(N_P), const(8, LANES), const(1, D_Q_LORA), const(1, D_KV_LORA),
                  const(1, LANES), row(LANES), row(LANES)],
        out_specs=out_specs, out_shape=out_shape,
        scratch_shapes=[pltpu.VMEM((1, LANES), F32)],
        compiler_params=_cparams(("arbitrary",), 48), name="prep",
    )(p, gains, gql, gkvl, fb, cos_b, sin_b)


def _rope_d(y, cd, sd):
    lane = lax.broadcasted_iota(jnp.int32, y.shape, 1)
    half = D_ROPE // 2
    partner = jnp.where(lane < half, pltpu.roll(y, LANES - half, 1), pltpu.roll(y, half, 1))
    return y * cd + partner * sd


def _prep_d_kernel(ql_ref, kvl_ref, kr_ref, wq_ref, wkv_ref, gq_ref, gk_ref, cd_ref, sd_ref,
                   qd_ref, kd_ref, vd_ref):
    qscale = D_QK ** -0.5 * LOG2E
    cd, sd = cd_ref[...], sd_ref[...]
    q = _dot(ql_ref[...], wq_ref[...])
    kv = _dot(kvl_ref[...], wkv_ref[...])
    kr = kr_ref[...]
    ss_r = jnp.sum(kr * kr, axis=-1, keepdims=True)
    gq = gq_ref[...]
    gk = gk_ref[...]
    for h in range(D_HEADS):
        a = h * AUG
        qn, qr = q[:, a:a + HEAD_DIM], q[:, a + HEAD_DIM:a + AUG]
        ms = (jnp.sum(qn * qn, axis=-1, keepdims=True)
              + jnp.sum(qr * qr, axis=-1, keepdims=True)) * (1.0 / D_QK)
        r = lax.rsqrt(ms + EPS) * qscale
        qd_ref[:, a:a + HEAD_DIM] = (qn * r * gq[:, :HEAD_DIM]).astype(BF16)
        qd_ref[:, a + HEAD_DIM:a + AUG] = _rope_d(qr * r * gq[:, HEAD_DIM:], cd, sd).astype(BF16)
        kn = kv[:, h * D_NOPE:(h + 1) * D_NOPE]
        ms = (jnp.sum(kn * kn, axis=-1, keepdims=True) + ss_r) * (1.0 / D_QK)
        r = lax.rsqrt(ms + EPS)
        kd_ref[h, :HEAD_DIM, :] = (kn * r * gk[:, :HEAD_DIM]).T.astype(BF16)
        kd_ref[h, HEAD_DIM:, :] = _rope_d(kr * r * gk[:, HEAD_DIM:], cd, sd).T.astype(BF16)
    vd_ref[...] = kv[:, D_HEADS * D_NOPE:].astype(BF16)


def _prep_d(ql, kvl, kr, wq, wkv, gq, gk, cos_d, sin_d):
    s = ql.shape[0]
    tm = min(PREP_ROWS, s)

    def row(w):
        return pl.BlockSpec((tm, w), lambda i: (i, 0))

    def const(r, w):
        return pl.BlockSpec((r, w), lambda i: (0, 0))

    kt_shape, kt_spec = _kt_shape_spec(D_HEADS, s, tm)
    return pl.pallas_call(
        _prep_d_kernel, grid=(s // tm,),
        in_specs=[row(D_Q_LORA), row(D_KV_LORA), row(LANES),
                  const(D_Q_LORA, D_HEADS * AUG), const(D_KV_LORA, 2 * D_HEADS * D_NOPE),
                  const(1, AUG), const(1, AUG), row(LANES), row(LANES)],
        out_specs=[row(D_HEADS * AUG), kt_spec, row(D_HEADS * D_V)],
        out_shape=[jax.ShapeDtypeStruct((s, D_HEADS * AUG), BF16), kt_shape,
                   jax.ShapeDtypeStruct((s, D_HEADS * D_V), BF16)],
        compiler_params=_cparams(("parallel",), 40), name="prep_d",
    )(ql, kvl, kr, wq, wkv, gq, gk, cos_d, sin_d)


def _flash_kernel(q_ref, kt_ref, v_ref, o_ref, m_ref, l_ref, acc_ref, *, blk, unit):
    i = pl.program_id(1)
    q = q_ref[...]
    m_ref[...] = jnp.full_like(m_ref, NEG_INF)
    l_ref[...] = jnp.zeros_like(l_ref)
    acc_ref[...] = jnp.zeros_like(acc_ref)
    nct = blk // LANES
    rc = FLASH_ROW_CHUNK

    def step(j, diagonal):
        start = pl.multiple_of(j * blk, blk)
        s = _dot(q, kt_ref[j])
        v = v_ref[pl.ds(start, blk), :]
        ps, alphas = [], []
        for r in range(blk // rc):
            rows = slice(r * rc, (r + 1) * rc)
            sc = s[rows, :]
            if diagonal:
                qi = lax.broadcasted_iota(jnp.int32, sc.shape, 0) + r * rc
                ki = lax.broadcasted_iota(jnp.int32, sc.shape, 1)
                if unit > 1:
                    shift = unit.bit_length() - 1
                    qi, ki = qi >> shift, ki >> shift
                sc = jnp.where(ki <= qi, sc, NEG_INF)
            m_prev = m_ref[rows, :]
            m_new = jnp.maximum(m_prev, jnp.max(sc, axis=-1, keepdims=True))
            alpha = jnp.exp2(m_prev - m_new)
            pc = [jnp.exp2(sc[:, c * LANES:(c + 1) * LANES] - m_new) for c in range(nct)]
            lsum = pc[0]
            for c in range(1, nct):
                lsum = lsum + pc[c]
            l_ref[rows, :] = alpha * l_ref[rows, :] + lsum
            m_ref[rows, :] = m_new
            ps.append(jnp.concatenate(pc, axis=1).astype(BF16))
            alphas.append(alpha)
        p = jnp.concatenate(ps, axis=0)
        alpha = jnp.concatenate(alphas, axis=0)
        acc_ref[...] = alpha * acc_ref[...] + _dot(p, v)

    def body(j, carry):
        step(j, False)
        return carry

    lax.fori_loop(0, i, body, 0)
    step(i, True)
    l = jnp.sum(l_ref[...], axis=-1, keepdims=True)
    o_ref[...] = (acc_ref[...] / l).astype(o_ref.dtype)


def _flash(q, kt, v, n_heads, unit):
    s = q.shape[0]
    blk = kt.shape[3]
    dv = v.shape[1] // n_heads
    assert dv == LANES
    return pl.pallas_call(
        functools.partial(_flash_kernel, blk=blk, unit=unit),
        grid=(n_heads, s // blk),
        in_specs=[pl.BlockSpec((blk, AUG), lambda h, i: (i, h)),
                  pl.BlockSpec((None, s // blk, AUG, blk), lambda h, i: (h, 0, 0, 0)),
                  pl.BlockSpec((s, dv), lambda h, i: (0, h))],
        out_specs=pl.BlockSpec((blk, dv), lambda h, i: (i, h)),
        out_shape=jax.ShapeDtypeStruct((s, n_heads * dv), BF16),
        scratch_shapes=[pltpu.VMEM((blk, LANES), F32), pltpu.VMEM((blk, LANES), F32),
                        pltpu.VMEM((blk, dv), F32)],
        compiler_params=_cparams(("parallel", "arbitrary"), 48), name=f"flash_u{unit}",
    )(q, kt, v)


def _attn_a_kernel(q_ref, k_ref, v_ref, t_ref, o_ref):
    i = pl.program_id(1)
    start = pl.multiple_of(jnp.maximum(i * BAND_Q - A_PREV_CHUNKS * CHUNK, 0), BAND_Q)
    k = k_ref[pl.ds(start, A_WIN), :]
    v = v_ref[pl.ds(start, A_WIN), :]
    s = _dot_nt(q_ref[...], k) + t_ref[...]
    m = jnp.max(s, axis=-1, keepdims=True)
    p = jnp.exp2(s - m)
    l = jnp.sum(p, axis=-1, keepdims=True)
    o_ref[...] = (_dot(p.astype(BF16), v) / l).astype(o_ref.dtype)


def _attn_a(q, k, v, table):
    s = q.shape[0]
    nb = table.shape[0] - 1
    return pl.pallas_call(
        _attn_a_kernel, grid=(A_HEADS, s // BAND_Q),
        in_specs=[pl.BlockSpec((BAND_Q, HEAD_DIM), lambda h, i: (i, h)),
                  pl.BlockSpec((s, HEAD_DIM), lambda h, i: (0, h)),
                  pl.BlockSpec((s, HEAD_DIM), lambda h, i: (0, h)),
                  pl.BlockSpec((None, None, BAND_Q, A_WIN),
                               lambda h, i: (jnp.minimum(i, nb), h, 0, 0))],
        out_specs=pl.BlockSpec((BAND_Q, HEAD_DIM), lambda h, i: (i, h)),
        out_shape=jax.ShapeDtypeStruct((s, A_HEADS * HEAD_DIM), BF16),
        compiler_params=_cparams(("parallel", "arbitrary"), 40), name="attn_a",
    )(q, k, v, table)


def _a_table(rel_bias):
    nb = A_PREV_CHUNKS * CHUNK // BAND_Q + 1
    n, m = BAND_Q, A_WIN
    u = np.arange(n + m)
    dist = np.arange(nb)[:, None] * BAND_Q + (n - 1) - u[None, :]
    idx = np.clip(dist, -REL_CLIP, REL_CLIP) + REL_CLIP
    z = rel_bias.astype(F32).T[:, jnp.asarray(idx)] * LOG2E
    z = z.transpose(1, 0, 2)
    skew = jnp.broadcast_to(z[:, :, None, :], (nb, A_HEADS, n, n + m))
    skew = skew.reshape(nb, A_HEADS, n * (n + m))[:, :, :n * (n + m - 1)]
    bias = skew.reshape(nb, A_HEADS, n, n + m - 1)[:, :, :, n - 1:n - 1 + m]
    t = (np.arange(nb)[:, None, None] * BAND_Q + np.arange(n)[None, :, None])
    w = np.arange(m)[None, None, :]
    dchunk = t // CHUNK - w // CHUNK
    valid = (dchunk >= 0) & (dchunk <= A_PREV_CHUNKS)
    return jnp.where(jnp.asarray(valid)[:, None], bias, NEG_INF)


def _attn_b_kernel(sink_ref, q_ref, k_ref, v_ref, o_ref):
    g = pl.program_id(0)
    i = pl.program_id(1)
    group = B_Q_HEADS // B_KV_HEADS
    start = pl.multiple_of(jnp.maximum(i * BAND_Q - B_PREV_CHUNKS * CHUNK, 0), LANES)
    k = k_ref[pl.ds(start, B_WIN), :]
    v = v_ref[pl.ds(start, B_WIN), :]
    shift = CHUNK.bit_length() - 1
    t = (i * BAND_Q + lax.broadcasted_iota(jnp.int32, (BAND_Q, B_WIN), 0)) >> shift
    kp = (start + lax.broadcasted_iota(jnp.int32, (BAND_Q, B_WIN), 1)) >> shift
    valid = jnp.abs(t - kp - 1) <= 1
    for hi in range(group):
        c = hi * HEAD_DIM
        s = jnp.where(valid, _dot_nt(q_ref[:, c:c + HEAD_DIM], k), NEG_INF)
        sink = sink_ref[g * group + hi] * LOG2E
        m = jnp.maximum(jnp.max(s, axis=-1, keepdims=True), sink)
        p = jnp.exp2(s - m)
        l = jnp.sum(p, axis=-1, keepdims=True) + jnp.exp2(sink - m)
        o_ref[:, c:c + HEAD_DIM] = (_dot(p.astype(BF16), v) / l).astype(o_ref.dtype)


def _attn_b(q, k, v, sinks):
    s = q.shape[0]
    gw = (B_Q_HEADS // B_KV_HEADS) * HEAD_DIM
    return pl.pallas_call(
        _attn_b_kernel, grid=(B_KV_HEADS, s // BAND_Q),
        in_specs=[pl.BlockSpec(memory_space=pltpu.SMEM),
                  pl.BlockSpec((BAND_Q, gw), lambda g, i: (i, g)),
                  pl.BlockSpec((s, HEAD_DIM), lambda g, i: (0, g)),
                  pl.BlockSpec((s, HEAD_DIM), lambda g, i: (0, g))],
        out_specs=pl.BlockSpec((BAND_Q, gw), lambda g, i: (i, g)),
        out_shape=jax.ShapeDtypeStruct((s, B_Q_HEADS * HEAD_DIM), BF16),
        compiler_params=_cparams(("parallel", "arbitrary"), 40), name="attn_b",
    )(sinks, q, k, v)


def _merge_kernel(h_ref, o_ref, wg_ref, bg_ref, wb_ref, out_ref, acc_ref):
    b = pl.program_id(2)
    gate = jax.nn.sigmoid(_dot(h_ref[...], wg_ref[...]) + bg_ref[...])
    val = gate * _dot(o_ref[...], wb_ref[...])

    @pl.when(b == 0)
    def _():
        acc_ref[...] = val

    @pl.when(b > 0)
    def _():
        acc_ref[...] += val

    @pl.when(b == N_BRANCH - 1)
    def _():
        out_ref[...] = acc_ref[...].astype(out_ref.dtype)


def _merge(h, o_all, wg, bg, wb):
    s, d = h.shape
    tm, tn = min(1024, s), min(512, d)
    return pl.pallas_call(
        _merge_kernel, grid=(s // tm, d // tn, N_BRANCH),
        in_specs=[pl.BlockSpec((tm, d), lambda i, j, b: (i, 0)),
                  pl.BlockSpec((None, tm, BRANCH_W), lambda i, j, b: (b, i, 0)),
                  pl.BlockSpec((None, d, tn), lambda i, j, b: (b, 0, j)),
                  pl.BlockSpec((None, 1, tn), lambda i, j, b: (b, 0, j)),
                  pl.BlockSpec((None, BRANCH_W, tn), lambda i, j, b: (b, 0, j))],
        out_specs=pl.BlockSpec((tm, tn), lambda i, j, b: (i, j)),
        out_shape=jax.ShapeDtypeStruct((s, d), BF16),
        scratch_shapes=[pltpu.VMEM((tm, tn), F32)],
        compiler_params=_cparams(("parallel", "parallel", "arbitrary"), 56), name="merge",
    )(h, o_all, wg, bg, wb)


def _resid_kernel(a_ref, w_ref, x_ref, g_ref, o_ref):
    o_ref[...] = x_ref[...] + g_ref[...] * _dot(a_ref[...], w_ref[...])


def _resid_moe_kernel(a_ref, w_ref, c_ref, bd_ref, x_ref, g_ref, o_ref):
    y = _dot(a_ref[...], w_ref[...]) + _dot(c_ref[...].astype(BF16), bd_ref[...])
    o_ref[...] = x_ref[...] + g_ref[...] * y


def _resid(a, w, x, gate, comb=None, b_dn=None):
    s, k = a.shape
    d = w.shape[1]
    tm, tn = min(1024, s), min(512, d)
    a_spec = pl.BlockSpec((tm, k), lambda i, j: (i, 0))
    w_spec = pl.BlockSpec((k, tn), lambda i, j: (0, j))
    x_spec = pl.BlockSpec((tm, tn), lambda i, j: (i, j))
    g_spec = pl.BlockSpec((1, tn), lambda i, j: (0, j))
    if comb is None:
        kern, ins = _resid_kernel, (a, w, x, gate)
        in_specs = [a_spec, w_spec, x_spec, g_spec]
    else:
        kern, ins = _resid_moe_kernel, (a, w, comb, b_dn, x, gate)
        in_specs = [a_spec, w_spec, pl.BlockSpec((tm, LANES), lambda i, j: (i, 0)),
                    pl.BlockSpec((LANES, tn), lambda i, j: (0, j)), x_spec, g_spec]
    return pl.pallas_call(
        kern, grid=(s // tm, d // tn), in_specs=in_specs, out_specs=x_spec,
        out_shape=jax.ShapeDtypeStruct((s, d), F32),
        compiler_params=_cparams(("parallel", "parallel"), 48), name="resid",
    )(*ins)


def _moe_up_kernel(h_ref, w_ref, b_ref, c_ref, o_ref, *, eb):
    j = pl.program_id(1)
    h = h_ref[...]
    comb = c_ref[...]
    lane = lax.broadcasted_iota(jnp.int32, comb.shape, 1)
    for e in range(eb):
        gu = _dot(h, w_ref[e]) + b_ref[e]
        glu = jnp.minimum(gu[:, :D_EXPERT], SWIGLU_LIMIT)
        lin = jnp.clip(gu[:, D_EXPERT:], -SWIGLU_LIMIT, SWIGLU_LIMIT)
        act = glu * jax.nn.sigmoid(SWIGLU_ALPHA * glu) * (lin + 1.0)
        ce = jnp.sum(jnp.where(lane == j * eb + e, comb, 0.0), axis=-1, keepdims=True)
        o_ref[:, e * D_EXPERT:(e + 1) * D_EXPERT] = (act * ce).astype(o_ref.dtype)


def _moe_up(h, w_gu, b_gu, comb):
    s, d = h.shape
    tm, eb = min(1024, s), 4
    return pl.pallas_call(
        functools.partial(_moe_up_kernel, eb=eb), grid=(s // tm, N_EXPERTS // eb),
        in_specs=[pl.BlockSpec((tm, d), lambda i, j: (i, 0)),
                  pl.BlockSpec((eb, d, 2 * D_EXPERT), lambda i, j: (j, 0, 0)),
                  pl.BlockSpec((eb, 1, 2 * D_EXPERT), lambda i, j: (j, 0, 0)),
                  pl.BlockSpec((tm, LANES), lambda i, j: (i, 0))],
        out_specs=pl.BlockSpec((tm, eb * D_EXPERT), lambda i, j: (i, j)),
        out_shape=jax.ShapeDtypeStruct((s, N_EXPERTS * D_EXPERT), BF16),
        compiler_params=_cparams(("parallel", "parallel"), 48), name="moe_up",
    )(h, w_gu, b_gu, comb)


def _reorder_w_in(w):
    d = w.shape[0]
    f0 = OFF_QLAT
    q0 = f0 + C_HEADS
    zeros = lambda n: jnp.zeros((d, n), w.dtype)
    return jnp.concatenate(
        [w[:, :f0], w[:, q0:], zeros(LANES - D_ROPE), w[:, f0:q0],
         zeros(N_P - OFF_F - C_HEADS)], axis=1)


def _reorder_w_q_b(w):
    r = w.shape[0]
    w = w.reshape(r, D_HEADS, D_QK)
    w = jnp.pad(w, ((0, 0), (0, 0), (0, AUG - D_QK)))
    return w.reshape(r, D_HEADS * AUG)


def _reorder_w_kv_b(w):
    r = w.shape[0]
    w = w.reshape(r, D_HEADS, D_NOPE + D_V)
    return jnp.concatenate([w[:, :, :D_NOPE].reshape(r, -1), w[:, :, D_NOPE:].reshape(r, -1)], axis=1)


def _pad_lanes(v, width, value=0.0):
    return jnp.pad(v, (0, width - v.shape[0]), constant_values=value).reshape(1, width)


def kernel(x, c, positions, ada_w, ada_b, ada_layer, norm1_g, norm2_g, w_in, a_q_norm, a_k_norm, a_rel_bias, b_q_norm, b_k_norm, b_sinks, c_q_norm, c_k_norm, c_f_bias, d_q_a_norm, d_w_q_b, d_kv_a_norm, d_w_kv_b, d_q_norm, d_k_norm, w_branch, w_gate, b_gate, w_out, router_w, router_b, w_gu, b_gu, w_dn, b_dn):
    batch, s, d = x.shape
    assert batch == 1, "kernels are written for a single sequence"
    depth = w_in.shape[0]
    xs = x.reshape(s, d)

    base_mod = _adaln(c.reshape(d, 1), ada_w, ada_b).reshape(6, d)
    cos_b, sin_b, cos_d, sin_d = _rope_tables(positions.reshape(s, 1).astype(F32))

    for l in range(depth):
        mod = base_mod + ada_layer[l]
        shift1, scale1, gate1, shift2, scale2, gate2 = (mod[j:j + 1] for j in range(6))

        h = _norm(xs, norm1_g[l].reshape(1, d), scale1, shift1)
        p = _matmul(h, _reorder_w_in(w_in[l]).astype(BF16), BF16)
        gains = jnp.stack([a_q_norm[l], a_k_norm[l], b_q_norm[l], b_k_norm[l],
                           c_q_norm[l], c_k_norm[l], jnp.zeros_like(a_q_norm[l]),
                           jnp.zeros_like(a_q_norm[l])])
        (qa, ka, va, qb, kb, vb, qc, kc, vc, ql, kvl, kr) = _prep(
            p, gains, d_q_a_norm[l].reshape(1, -1), d_kv_a_norm[l].reshape(1, -1),
            _pad_lanes(c_f_bias[l], LANES), cos_b, sin_b)
        qd, kd, vd = _prep_d(
            ql, kvl, kr, _reorder_w_q_b(d_w_q_b[l]).astype(BF16),
            _reorder_w_kv_b(d_w_kv_b[l]).astype(BF16),
            _pad_lanes(d_q_norm[l], AUG), _pad_lanes(d_k_norm[l], AUG), cos_d, sin_d)

        o_a = _attn_a(qa, ka, va, _a_table(a_rel_bias[l]))
        o_b = _attn_b(qb, kb, vb, b_sinks[l].astype(F32))
        o_c = _flash(qc, kc, vc, C_HEADS, 1)
        o_d = _flash(qd, kd, vd, D_HEADS, CHUNK)

        merged = _merge(h, jnp.stack([o_a, o_b, o_c, o_d]), w_gate[l].astype(BF16),
                        b_gate[l].reshape(N_BRANCH, 1, d), w_branch[l].astype(BF16))
        xs = _resid(merged, w_out[l].astype(BF16), xs, gate1)

        rw = jnp.pad(router_w[l], ((0, 0), (0, LANES - N_EXPERTS)))
        rb = _pad_lanes(router_b[l].astype(F32), LANES, NEG_INF)
        h2, comb = _norm(xs, norm2_g[l].reshape(1, d), scale2, shift2, router=(rw, rb))
        act = _moe_up(h2, w_gu[l].astype(BF16), b_gu[l].reshape(N_EXPERTS, 1, 2 * D_EXPERT), comb)
        bdn = jnp.pad(b_dn[l], ((0, LANES - N_EXPERTS), (0, 0))).astype(BF16)
        xs = _resid(act, w_dn[l].reshape(N_EXPERTS * D_EXPERT, d).astype(BF16), xs, gate2,
                    comb=comb, b_dn=bdn)
    return xs.reshape(batch, s, d)
```

```python
import functools
import math

import numpy as np
import jax
import jax.numpy as jnp
from jax import lax
from jax.experimental import pallas as pl
from jax.experimental.pallas import tpu as pltpu

F32 = jnp.float32
BF16 = jnp.bfloat16

CHUNK = 64
HEAD_DIM = 128
N_BRANCH = 4
BRANCH_W = 1024
ROPE_THETA = 10000.0
EPS = 1e-6
NEG_INF = -1e30
A_HEADS = 8
A_PREV_CHUNKS = 8
REL_CLIP = 256
B_Q_HEADS = 8
B_KV_HEADS = 2
B_PREV_CHUNKS = 2
C_HEADS = 8
D_HEADS = 8
D_Q_LORA = 896
D_KV_LORA = 256
D_NOPE = 128
D_ROPE = 64
D_V = 128
D_QK = D_NOPE + D_ROPE
N_EXPERTS = 32
TOP_K = 4
D_EXPERT = 128
SWIGLU_LIMIT = 7.0
SWIGLU_ALPHA = 1.702

LANES = 128
V7X_VMEM_BYTES = 64 * 1024 * 1024

LOG2E = math.log2(math.e)

OFF_QA, OFF_KA, OFF_VA = 0, 1024, 2048
OFF_QB, OFF_KB, OFF_VB = 3072, 4096, 4352
OFF_QC, OFF_KC, OFF_VC = 4608, 5632, 6656
OFF_QLAT = 7680
OFF_KVLAT = OFF_QLAT + D_Q_LORA
OFF_KROPE = OFF_KVLAT + D_KV_LORA
OFF_F = OFF_KROPE + LANES
N_P = 9216

AUG = 2 * HEAD_DIM
FLASH_BLK = 1024
FLASH_ROW_CHUNK = 32
FLASH_HEADS = 2
PREP_ROWS = 256
BAND_Q = 256
A_WIN = BAND_Q + A_PREV_CHUNKS * CHUNK
B_WIN = BAND_Q + B_PREV_CHUNKS * CHUNK


def _cparams(sems, vmem_mb):
    return pltpu.CompilerParams(dimension_semantics=sems,
                                vmem_limit_bytes=vmem_mb * 1024 * 1024)


def _dot(a, b):
    return jnp.dot(a, b, preferred_element_type=F32)


def _dot_nt(a, b):
    return lax.dot_general(a, b, (((1,), (1,)), ((), ())), preferred_element_type=F32)


def _adaln_kernel(c_ref, w_ref, b_ref, o_ref):
    c = c_ref[...]
    s = c * jax.nn.sigmoid(c)
    o_ref[...] = jnp.sum(w_ref[...] * s, axis=0, keepdims=True) + b_ref[...]


def _adaln(c_col, ada_w, ada_b):
    d, n = ada_w.shape
    tn = min(512, n)
    return pl.pallas_call(
        _adaln_kernel,
        grid=(n // tn,),
        in_specs=[pl.BlockSpec((d, 1), lambda j: (0, 0)),
                  pl.BlockSpec((d, tn), lambda j: (0, j)),
                  pl.BlockSpec((1, tn), lambda j: (0, j))],
        out_specs=pl.BlockSpec((1, tn), lambda j: (0, j)),
        out_shape=jax.ShapeDtypeStruct((1, n), F32),
        compiler_params=_cparams(("parallel",), 40),
        name="adaln",
    )(c_col, ada_w, ada_b.reshape(1, n))


def _rope_table_kernel(pos_ref, fb_ref, fd_ref, cb_ref, sb_ref, cd_ref, sd_ref):
    pos = pos_ref[...]
    lane = lax.broadcasted_iota(jnp.int32, (1, LANES), 1)
    ab = pos * fb_ref[...]
    sb = jnp.sin(ab)
    cb_ref[...] = jnp.cos(ab)
    sb_ref[...] = jnp.where(lane < HEAD_DIM // 2, -sb, sb)
    ad = pos * fd_ref[...]
    sd = jnp.sin(ad)
    cd_ref[...] = jnp.where(lane < D_ROPE, jnp.cos(ad), 1.0)
    sd_ref[...] = jnp.where(lane < D_ROPE // 2, -sd, jnp.where(lane < D_ROPE, sd, 0.0))


def _rope_tables(pos_col):
    s = pos_col.shape[0]
    tm = min(512, s)
    lane = np.arange(LANES)
    half_b = HEAD_DIM // 2
    fb = ROPE_THETA ** (-(lane % half_b).astype(np.float64) / half_b)
    half_d = D_ROPE // 2
    fd = np.where(lane < D_ROPE, ROPE_THETA ** (-(lane % half_d).astype(np.float64) / half_d), 0.0)
    fb = jnp.asarray(fb, F32).reshape(1, LANES)
    fd = jnp.asarray(fd, F32).reshape(1, LANES)
    tab = jax.ShapeDtypeStruct((s, LANES), F32)
    row = pl.BlockSpec((tm, LANES), lambda i: (i, 0))
    const = pl.BlockSpec((1, LANES), lambda i: (0, 0))
    return pl.pallas_call(
        _rope_table_kernel,
        grid=(s // tm,),
        in_specs=[pl.BlockSpec((tm, 1), lambda i: (i, 0)), const, const],
        out_specs=[row, row, row, row],
        out_shape=[tab, tab, tab, tab],
        compiler_params=_cparams(("parallel",), 32),
        name="rope_tables",
    )(pos_col, fb, fd)


def _mod_norm(x, g, sc, sh):
    ms = jnp.mean(x * x, axis=-1, keepdims=True)
    y = x * lax.rsqrt(ms + EPS)
    return (y * g) * (1.0 + sc) + sh


def _norm_kernel(x_ref, g_ref, sc_ref, sh_ref, h_ref):
    h_ref[...] = _mod_norm(x_ref[...], g_ref[...], sc_ref[...], sh_ref[...]).astype(h_ref.dtype)


def _norm_router_kernel(x_ref, g_ref, sc_ref, sh_ref, rw_ref, rb_ref, h_ref, comb_ref):
    h = _mod_norm(x_ref[...], g_ref[...], sc_ref[...], sh_ref[...])
    h_ref[...] = h.astype(h_ref.dtype)
    logits = jnp.dot(h, rw_ref[...], precision=lax.Precision.HIGHEST,
                     preferred_element_type=F32) + rb_ref[...]
    lane = lax.broadcasted_iota(jnp.int32, logits.shape, 1).astype(F32)
    work = logits
    vals, sels = [], []
    for _ in range(TOP_K):
        m = jnp.max(work, axis=-1, keepdims=True)
        idx = jnp.min(jnp.where(work == m, lane, float(LANES)), axis=-1, keepdims=True)
        sel = lane == idx
        vals.append(m)
        sels.append(sel)
        work = jnp.where(sel, -3.0e38, work)
    es = [jnp.exp(v - vals[0]) for v in vals]
    inv = 1.0 / (es[0] + es[1] + es[2] + es[3])
    comb = jnp.zeros_like(logits)
    for e, sel in zip(es, sels):
        comb = comb + jnp.where(sel, e * inv, 0.0)
    comb_ref[...] = comb


def _norm(x, g, sc, sh, router=None):
    s, d = x.shape
    tm = min(256, s)
    row = pl.BlockSpec((tm, d), lambda i: (i, 0))
    vec = pl.BlockSpec((1, d), lambda i: (0, 0))
    if router is None:
        return pl.pallas_call(
            _norm_kernel, grid=(s // tm,),
            in_specs=[row, vec, vec, vec], out_specs=row,
            out_shape=jax.ShapeDtypeStruct((s, d), BF16),
            compiler_params=_cparams(("parallel",), 32), name="norm",
        )(x, g, sc, sh)
    rw, rb = router
    return pl.pallas_call(
        _norm_router_kernel, grid=(s // tm,),
        in_specs=[row, vec, vec, vec,
                  pl.BlockSpec((d, LANES), lambda i: (0, 0)),
                  pl.BlockSpec((1, LANES), lambda i: (0, 0))],
        out_specs=[row, pl.BlockSpec((tm, LANES), lambda i: (i, 0))],
        out_shape=[jax.ShapeDtypeStruct((s, d), BF16), jax.ShapeDtypeStruct((s, LANES), F32)],
        compiler_params=_cparams(("parallel",), 40), name="norm_router",
    )(x, g, sc, sh, rw, rb)


def _mm_kernel(a_ref, b_ref, o_ref):
    o_ref[...] = _dot(a_ref[...], b_ref[...]).astype(o_ref.dtype)


def _matmul(a, b, out_dtype, tm=1024, tn=1024):
    m, k = a.shape
    n = b.shape[1]
    tm, tn = min(tm, m), min(tn, n)
    return pl.pallas_call(
        _mm_kernel, grid=(m // tm, n // tn),
        in_specs=[pl.BlockSpec((tm, k), lambda i, j: (i, 0)),
                  pl.BlockSpec((k, tn), lambda i, j: (0, j))],
        out_specs=pl.BlockSpec((tm, tn), lambda i, j: (i, j)),
        out_shape=jax.ShapeDtypeStruct((m, n), out_dtype),
        compiler_params=_cparams(("parallel", "parallel"), 48), name="in_proj",
    )(a, b)


def _prep_kernel(p_ref, gains_ref, gql_ref, gkvl_ref, fb_ref, cb_ref, sb_ref,
                 qa_ref, ka_ref, va_ref, qb_ref, kb_ref, vb_ref,
                 qc_ref, kc_ref, vc_ref, ql_ref, kvl_ref, kr_ref, carry_ref):
    tm = p_ref.shape[0]
    qscale = HEAD_DIM ** -0.5 * LOG2E

    @pl.when(pl.program_id(0) == 0)
    def _():
        carry_ref[...] = jnp.zeros_like(carry_ref)

    def head_norm(off, g_row, scale):
        x = p_ref[:, off:off + HEAD_DIM].astype(F32)
        ms = jnp.mean(x * x, axis=-1, keepdims=True)
        return x * lax.rsqrt(ms + EPS) * (gains_ref[g_row:g_row + 1, :] * scale)

    cb, sb = cb_ref[...], sb_ref[...]

    def rope(y):
        return y * cb + pltpu.roll(y, HEAD_DIM // 2, 1) * sb

    for h in range(A_HEADS):
        c = h * HEAD_DIM
        qa_ref[:, c:c + HEAD_DIM] = head_norm(OFF_QA + c, 0, qscale).astype(BF16)
        ka_ref[:, c:c + HEAD_DIM] = head_norm(OFF_KA + c, 1, 1.0).astype(BF16)
    va_ref[...] = p_ref[:, OFF_VA:OFF_VA + A_HEADS * HEAD_DIM]

    for h in range(B_Q_HEADS):
        c = h * HEAD_DIM
        qb_ref[:, c:c + HEAD_DIM] = rope(head_norm(OFF_QB + c, 2, qscale)).astype(BF16)
    for h in range(B_KV_HEADS):
        c = h * HEAD_DIM
        kb_ref[:, c:c + HEAD_DIM] = rope(head_norm(OFF_KB + c, 3, 1.0)).astype(BF16)
    vb_ref[...] = p_ref[:, OFF_VB:OFF_VB + B_KV_HEADS * HEAD_DIM]

    z = p_ref[:, OFF_F:OFF_F + LANES].astype(F32) + fb_ref[...]
    logf = jnp.minimum(z, 0.0) - jnp.log1p(jnp.exp(-jnp.abs(z)))
    r_i = lax.broadcasted_iota(jnp.int32, (tm, tm), 0)
    c_i = lax.broadcasted_iota(jnp.int32, (tm, tm), 1)
    tri = jnp.where(r_i >= c_i, 1.0, 0.0).astype(F32)
    cum = jnp.dot(tri, logf, precision=lax.Precision.HIGHEST,
                  preferred_element_type=F32) + carry_ref[...]
    carry_ref[...] = cum[tm - 1:tm, :]
    cum2 = cum * LOG2E
    hi = cum2.astype(BF16).astype(F32)
    r1 = cum2 - hi
    mid = r1.astype(BF16).astype(F32)
    lo = r1 - mid
    lane = lax.broadcasted_iota(jnp.int32, (tm, LANES), 1)
    for h in range(C_HEADS):
        c = h * HEAD_DIM
        a = h * AUG
        hi_h, mid_h, lo_h = hi[:, h:h + 1], mid[:, h:h + 1], lo[:, h:h + 1]
        aug_q = jnp.where(lane == 0, hi_h, jnp.where(lane == 1, mid_h, jnp.where(
            lane == 2, lo_h, jnp.where(lane < 6, 1.0, 0.0))))
        aug_k = jnp.where(lane < 3, 1.0, jnp.where(lane == 3, -hi_h, jnp.where(
            lane == 4, -mid_h, jnp.where(lane == 5, -lo_h, 0.0))))
        qc_ref[:, a:a + HEAD_DIM] = head_norm(OFF_QC + c, 4, qscale).astype(BF16)
        qc_ref[:, a + HEAD_DIM:a + AUG] = aug_q.astype(BF16)
        kc_ref[h, :HEAD_DIM, :] = head_norm(OFF_KC + c, 5, 1.0).T.astype(BF16)
        kc_ref[h, HEAD_DIM:, :] = aug_k.T.astype(BF16)
    vc_ref[...] = p_ref[:, OFF_VC:OFF_VC + C_HEADS * HEAD_DIM]

    xq = p_ref[:, OFF_QLAT:OFF_QLAT + D_Q_LORA].astype(F32)
    ms = jnp.mean(xq * xq, axis=-1, keepdims=True)
    ql_ref[...] = (xq * lax.rsqrt(ms + EPS) * gql_ref[...]).astype(BF16)
    xkv = p_ref[:, OFF_KVLAT:OFF_KVLAT + D_KV_LORA].astype(F32)
    ms = jnp.mean(xkv * xkv, axis=-1, keepdims=True)
    kvl_ref[...] = (xkv * lax.rsqrt(ms + EPS) * gkvl_ref[...]).astype(BF16)
    kr_ref[...] = p_ref[:, OFF_KROPE:OFF_KROPE + LANES].astype(F32)


def _kt_shape_spec(n_heads, s, tm):
    blk = min(FLASH_BLK, s)
    per = blk // tm
    shape = jax.ShapeDtypeStruct((n_heads, s // blk, AUG, blk), BF16)
    spec = pl.BlockSpec((n_heads, None, AUG, tm), lambda i: (0, i // per, 0, i % per))
    return shape, spec


def _prep(p, gains, gql, gkvl, fb, cos_b, sin_b):
    s = p.shape[0]
    tm = min(PREP_ROWS, s)

    def row(w):
        return pl.BlockSpec((tm, w), lambda i: (i, 0))

    def const(r, w):
        return pl.BlockSpec((r, w), lambda i: (0, 0))

    widths = [1024, 1024, 1024, 1024, 256, 256, C_HEADS * AUG, None, 1024,
              D_Q_LORA, D_KV_LORA]
    out_shape = [jax.ShapeDtypeStruct((s, w), BF16) for w in widths if w]
    out_specs = [row(w) for w in widths if w]
    kt_shape, kt_spec = _kt_shape_spec(C_HEADS, s, tm)
    out_shape.insert(7, kt_shape)
    out_specs.insert(7, kt_spec)
    out_shape.append(jax.ShapeDtypeStruct((s, LANES), F32))
    out_specs.append(row(LANES))
    return pl.pallas_call(
        _prep_kernel, grid=(s // tm,),
        in_specs=[row(N_P), const(8, LANES), const(1, D_Q_LORA), const(1, D_KV_LORA),
                  const(1, LANES), row(LANES), row(LANES)],
        out_specs=out_specs, out_shape=out_shape,
        scratch_shapes=[pltpu.VMEM((1, LANES), F32)],
        compiler_params=_cparams(("arbitrary",), 48), name="prep",
    )(p, gains, gql, gkvl, fb, cos_b, sin_b)


def _rope_d(y, cd, sd):
    lane = lax.broadcasted_iota(jnp.int32, y.shape, 1)
    half = D_ROPE // 2
    partner = jnp.where(lane < half, pltpu.roll(y, LANES - half, 1), pltpu.roll(y, half, 1))
    return y * cd + partner * sd


def _prep_d_kernel(ql_ref, kvl_ref, kr_ref, wq_ref, wkv_ref, gq_ref, gk_ref, cd_ref, sd_ref,
                   qd_ref, kd_ref, vd_ref):
    qscale = D_QK ** -0.5 * LOG2E
    cd, sd = cd_ref[...], sd_ref[...]
    q = _dot(ql_ref[...], wq_ref[...])
    kv = _dot(kvl_ref[...], wkv_ref[...])
    kr = kr_ref[...]
    ss_r = jnp.sum(kr * kr, axis=-1, keepdims=True)
    gq = gq_ref[...]
    gk = gk_ref[...]
    for h in range(D_HEADS):
        a = h * AUG
        qn, qr = q[:, a:a + HEAD_DIM], q[:, a + HEAD_DIM:a + AUG]
        ms = (jnp.sum(qn * qn, axis=-1, keepdims=True)
              + jnp.sum(qr * qr, axis=-1, keepdims=True)) * (1.0 / D_QK)
        r = lax.rsqrt(ms + EPS) * qscale
        qd_ref[:, a:a + HEAD_DIM] = (qn * r * gq[:, :HEAD_DIM]).astype(BF16)
        qd_ref[:, a + HEAD_DIM:a + AUG] = _rope_d(qr * r * gq[:, HEAD_DIM:], cd, sd).astype(BF16)
        kn = kv[:, h * D_NOPE:(h + 1) * D_NOPE]
        ms = (jnp.sum(kn * kn, axis=-1, keepdims=True) + ss_r) * (1.0 / D_QK)
        r = lax.rsqrt(ms + EPS)
        kd_ref[h, :HEAD_DIM, :] = (kn * r * gk[:, :HEAD_DIM]).T.astype(BF16)
        kd_ref[h, HEAD_DIM:, :] = _rope_d(kr * r * gk[:, HEAD_DIM:], cd, sd).T.astype(BF16)
    vd_ref[...] = kv[:, D_HEADS * D_NOPE:].astype(BF16)


def _prep_d(ql, kvl, kr, wq, wkv, gq, gk, cos_d, sin_d):
    s = ql.shape[0]
    tm = min(PREP_ROWS, s)

    def row(w):
        return pl.BlockSpec((tm, w), lambda i: (i, 0))

    def const(r, w):
        return pl.BlockSpec((r, w), lambda i: (0, 0))

    kt_shape, kt_spec = _kt_shape_spec(D_HEADS, s, tm)
    return pl.pallas_call(
        _prep_d_kernel, grid=(s // tm,),
        in_specs=[row(D_Q_LORA), row(D_KV_LORA), row(LANES),
                  const(D_Q_LORA, D_HEADS * AUG), const(D_KV_LORA, 2 * D_HEADS * D_NOPE),
                  const(1, AUG), const(1, AUG), row(LANES), row(LANES)],
        out_specs=[row(D_HEADS * AUG), kt_spec, row(D_HEADS * D_V)],
        out_shape=[jax.ShapeDtypeStruct((s, D_HEADS * AUG), BF16), kt_shape,
                   jax.ShapeDtypeStruct((s, D_HEADS * D_V), BF16)],
        compiler_params=_cparams(("parallel",), 40), name="prep_d",
    )(ql, kvl, kr, wq, wkv, gq, gk, cos_d, sin_d)


def _flash_kernel(q_ref, kt_ref, v_ref, o_ref, m_ref, l_ref, acc_ref, *, blk, unit):
    i = pl.program_id(1)
    m_ref[...] = jnp.full_like(m_ref, NEG_INF)
    l_ref[...] = jnp.zeros_like(l_ref)
    acc_ref[...] = jnp.zeros_like(acc_ref)
    nct = blk // LANES
    rc = FLASH_ROW_CHUNK

    def step(j, diagonal):
        start = pl.multiple_of(j * blk, blk)
        for g in range(FLASH_HEADS):
            s = _dot(q_ref[:, g * AUG:(g + 1) * AUG], kt_ref[g, j])
            v = v_ref[pl.ds(start, blk), g * LANES:(g + 1) * LANES]
            ps, alphas = [], []
            for r in range(blk // rc):
                rows = slice(r * rc, (r + 1) * rc)
                sc = s[rows, :]
                if diagonal:
                    qi = lax.broadcasted_iota(jnp.int32, sc.shape, 0) + r * rc
                    ki = lax.broadcasted_iota(jnp.int32, sc.shape, 1)
                    if unit > 1:
                        shift = unit.bit_length() - 1
                        qi, ki = qi >> shift, ki >> shift
                    sc = jnp.where(ki <= qi, sc, NEG_INF)
                m_prev = m_ref[g, rows, :]
                m_new = jnp.maximum(m_prev, jnp.max(sc, axis=-1, keepdims=True))
                alpha = jnp.exp2(m_prev - m_new)
                pc = [jnp.exp2(sc[:, c * LANES:(c + 1) * LANES] - m_new) for c in range(nct)]
                lsum = pc[0]
                for c in range(1, nct):
                    lsum = lsum + pc[c]
                l_ref[g, rows, :] = alpha * l_ref[g, rows, :] + lsum
                m_ref[g, rows, :] = m_new
                ps.append(jnp.concatenate(pc, axis=1).astype(BF16))
                alphas.append(alpha)
            p = jnp.concatenate(ps, axis=0)
            alpha = jnp.concatenate(alphas, axis=0)
            acc_ref[g] = alpha * acc_ref[g] + _dot(p, v)

    def body(j, carry):
        step(j, False)
        return carry

    lax.fori_loop(0, i, body, 0)
    step(i, True)
    for g in range(FLASH_HEADS):
        l = jnp.sum(l_ref[g], axis=-1, keepdims=True)
        o_ref[:, g * LANES:(g + 1) * LANES] = (acc_ref[g] / l).astype(o_ref.dtype)


def _flash(q, kt, v, n_heads, unit):
    s = q.shape[0]
    blk = kt.shape[3]
    dv = v.shape[1] // n_heads
    g = FLASH_HEADS
    assert dv == LANES and n_heads % g == 0
    once = pl.Buffered(1)
    return pl.pallas_call(
        functools.partial(_flash_kernel, blk=blk, unit=unit),
        grid=(n_heads // g, s // blk),
        in_specs=[pl.BlockSpec((blk, g * AUG), lambda h, i: (i, h)),
                  pl.BlockSpec((g, s // blk, AUG, blk), lambda h, i: (h, 0, 0, 0), pipeline_mode=once),
                  pl.BlockSpec((s, g * dv), lambda h, i: (0, h), pipeline_mode=once)],
        out_specs=pl.BlockSpec((blk, g * dv), lambda h, i: (i, h)),
        out_shape=jax.ShapeDtypeStruct((s, n_heads * dv), BF16),
        scratch_shapes=[pltpu.VMEM((g, blk, LANES), F32), pltpu.VMEM((g, blk, LANES), F32),
                        pltpu.VMEM((g, blk, dv), F32)],
        compiler_params=_cparams(("parallel", "arbitrary"), 56), name=f"flash_u{unit}",
    )(q, kt, v)


A_PREV_BLOCKS = A_PREV_CHUNKS * CHUNK // BAND_Q


def _attn_a_kernel(q_ref, k0_ref, k1_ref, k2_ref, v0_ref, v1_ref, v2_ref, t_ref, mk_ref, o_ref):
    mask = mk_ref[...]
    for h in range(A_HEADS):
        cols = slice(h * HEAD_DIM, (h + 1) * HEAD_DIM)
        k = jnp.concatenate([k0_ref[:, cols], k1_ref[:, cols], k2_ref[:, cols]], axis=0)
        v = jnp.concatenate([v0_ref[:, cols], v1_ref[:, cols], v2_ref[:, cols]], axis=0)
        s = _dot_nt(q_ref[:, cols], k) + t_ref[h] + mask
        m = jnp.max(s, axis=-1, keepdims=True)
        p = jnp.exp2(s - m)
        l = jnp.sum(p, axis=-1, keepdims=True)
        o_ref[:, cols] = (_dot(p.astype(BF16), v) / l).astype(o_ref.dtype)


def _attn_a(q, k, v, table, mask):
    s, w = q.shape
    row = pl.BlockSpec((BAND_Q, w), lambda i: (i, 0))
    prev = [pl.BlockSpec((BAND_Q, w), functools.partial(
        lambda i, back: (jnp.maximum(i - back, 0), 0), back=back))
        for back in range(A_PREV_BLOCKS, -1, -1)]
    return pl.pallas_call(
        _attn_a_kernel, grid=(s // BAND_Q,),
        in_specs=[row] + prev + prev + [
            pl.BlockSpec((A_HEADS, BAND_Q, A_WIN), lambda i: (0, 0, 0)),
            pl.BlockSpec((None, BAND_Q, A_WIN), lambda i: (jnp.minimum(i, A_PREV_BLOCKS), 0, 0))],
        out_specs=row,
        out_shape=jax.ShapeDtypeStruct((s, w), BF16),
        compiler_params=_cparams(("arbitrary",), 40), name="attn_a",
    )(q, k, k, k, v, v, v, table, mask)


def _a_mask():
    b = np.arange(A_PREV_BLOCKS + 1)[:, None, None]
    q = np.arange(BAND_Q)[None, :, None]
    w = np.arange(A_WIN)[None, None, :]
    dchunk = (q + A_PREV_BLOCKS * BAND_Q) // CHUNK - w // CHUNK
    valid = (dchunk >= 0) & (dchunk <= A_PREV_CHUNKS) & (w >= (A_PREV_BLOCKS - b) * BAND_Q)
    return jnp.asarray(np.where(valid, 0.0, NEG_INF), F32)


def _a_table(rel_bias):
    n, m = BAND_Q, A_WIN
    u = np.arange(n + m)
    dist = A_PREV_BLOCKS * BAND_Q + (n - 1) - u
    idx = np.clip(dist, -REL_CLIP, REL_CLIP) + REL_CLIP
    z = rel_bias.astype(F32).T[:, jnp.asarray(idx)] * LOG2E
    skew = jnp.broadcast_to(z[:, None, :], (A_HEADS, n, n + m))
    skew = skew.reshape(A_HEADS, n * (n + m))[:, :n * (n + m - 1)]
    return skew.reshape(A_HEADS, n, n + m - 1)[:, :, n - 1:n - 1 + m]


def _attn_b_kernel(sink_ref, q_ref, k_ref, v_ref, o_ref):
    g = pl.program_id(0)
    i = pl.program_id(1)
    group = B_Q_HEADS // B_KV_HEADS
    start = pl.multiple_of(jnp.maximum(i * BAND_Q - B_PREV_CHUNKS * CHUNK, 0), LANES)
    k = k_ref[pl.ds(start, B_WIN), :]
    v = v_ref[pl.ds(start, B_WIN), :]
    shift = CHUNK.bit_length() - 1
    t = (i * BAND_Q + lax.broadcasted_iota(jnp.int32, (BAND_Q, B_WIN), 0)) >> shift
    kp = (start + lax.broadcasted_iota(jnp.int32, (BAND_Q, B_WIN), 1)) >> shift
    valid = jnp.abs(t - kp - 1) <= 1
    for hi in range(group):
        c = hi * HEAD_DIM
        s = jnp.where(valid, _dot_nt(q_ref[:, c:c + HEAD_DIM], k), NEG_INF)
        sink = sink_ref[g * group + hi] * LOG2E
        m = jnp.maximum(jnp.max(s, axis=-1, keepdims=True), sink)
        p = jnp.exp2(s - m)
        l = jnp.sum(p, axis=-1, keepdims=True) + jnp.exp2(sink - m)
        o_ref[:, c:c + HEAD_DIM] = (_dot(p.astype(BF16), v) / l).astype(o_ref.dtype)


def _attn_b(q, k, v, sinks):
    s = q.shape[0]
    gw = (B_Q_HEADS // B_KV_HEADS) * HEAD_DIM
    return pl.pallas_call(
        _attn_b_kernel, grid=(B_KV_HEADS, s // BAND_Q),
        in_specs=[pl.BlockSpec(memory_space=pltpu.SMEM),
                  pl.BlockSpec((BAND_Q, gw), lambda g, i: (i, g)),
                  pl.BlockSpec((s, HEAD_DIM), lambda g, i: (0, g)),
                  pl.BlockSpec((s, HEAD_DIM), lambda g, i: (0, g))],
        out_specs=pl.BlockSpec((BAND_Q, gw), lambda g, i: (i, g)),
        out_shape=jax.ShapeDtypeStruct((s, B_Q_HEADS * HEAD_DIM), BF16),
        compiler_params=_cparams(("parallel", "arbitrary"), 40), name="attn_b",
    )(sinks, q, k, v)


def _merge_kernel(h_ref, oa_ref, ob_ref, oc_ref, od_ref, wg_ref, bg_ref, wb_ref, out_ref, acc_ref):
    b = pl.program_id(2)
    gate = jax.nn.sigmoid(_dot(h_ref[...], wg_ref[...]) + bg_ref[...])

    for n, o_ref in enumerate((oa_ref, ob_ref, oc_ref, od_ref)):
        @pl.when(b == n)
        def _(o_ref=o_ref, n=n):
            val = gate * _dot(o_ref[...], wb_ref[...])
            if n == 0:
                acc_ref[...] = val
            elif n < N_BRANCH - 1:
                acc_ref[...] += val
            else:
                out_ref[...] = (acc_ref[...] + val).astype(out_ref.dtype)


def _merge(h, outs, wg, bg, wb):
    s, d = h.shape
    tm, tn = min(1024, s), min(512, d)
    once = pl.Buffered(1)
    o_spec = pl.BlockSpec((tm, BRANCH_W), lambda i, j, b: (i, 0), pipeline_mode=once)
    return pl.pallas_call(
        _merge_kernel, grid=(s // tm, d // tn, N_BRANCH),
        in_specs=[pl.BlockSpec((tm, d), lambda i, j, b: (i, 0), pipeline_mode=once),
                  o_spec, o_spec, o_spec, o_spec,
                  pl.BlockSpec((None, d, tn), lambda i, j, b: (b, 0, j)),
                  pl.BlockSpec((None, 1, tn), lambda i, j, b: (b, 0, j)),
                  pl.BlockSpec((None, BRANCH_W, tn), lambda i, j, b: (b, 0, j))],
        out_specs=pl.BlockSpec((tm, tn), lambda i, j, b: (i, j)),
        out_shape=jax.ShapeDtypeStruct((s, d), BF16),
        scratch_shapes=[pltpu.VMEM((tm, tn), F32)],
        compiler_params=_cparams(("parallel", "parallel", "arbitrary"), 56), name="merge",
    )(h, *outs, wg, bg, wb)


def _resid_kernel(a_ref, w_ref, x_ref, g_ref, o_ref):
    o_ref[...] = x_ref[...] + g_ref[...] * _dot(a_ref[...], w_ref[...])


def _resid_moe_kernel(a_ref, w_ref, c_ref, bd_ref, x_ref, g_ref, o_ref):
    y = _dot(a_ref[...], w_ref[...]) + _dot(c_ref[...].astype(BF16), bd_ref[...])
    o_ref[...] = x_ref[...] + g_ref[...] * y


def _resid(a, w, x, gate, comb=None, b_dn=None):
    s, k = a.shape
    d = w.shape[1]
    tm, tn = min(1024, s), min(512, d)
    a_spec = pl.BlockSpec((tm, k), lambda i, j: (i, 0))
    w_spec = pl.BlockSpec((k, tn), lambda i, j: (0, j))
    x_spec = pl.BlockSpec((tm, tn), lambda i, j: (i, j))
    g_spec = pl.BlockSpec((1, tn), lambda i, j: (0, j))
    if comb is None:
        kern, ins = _resid_kernel, (a, w, x, gate)
        in_specs = [a_spec, w_spec, x_spec, g_spec]
    else:
        kern, ins = _resid_moe_kernel, (a, w, comb, b_dn, x, gate)
        in_specs = [a_spec, w_spec, pl.BlockSpec((tm, LANES), lambda i, j: (i, 0)),
                    pl.BlockSpec((LANES, tn), lambda i, j: (0, j)), x_spec, g_spec]
    return pl.pallas_call(
        kern, grid=(s // tm, d // tn), in_specs=in_specs, out_specs=x_spec,
        out_shape=jax.ShapeDtypeStruct((s, d), F32),
        compiler_params=_cparams(("parallel", "parallel"), 48), name="resid",
    )(*ins)


def _moe_up_kernel(h_ref, w_ref, b_ref, c_ref, o_ref, *, eb):
    j = pl.program_id(1)
    h = h_ref[...]
    comb = c_ref[...]
    lane = lax.broadcasted_iota(jnp.int32, comb.shape, 1)
    for e in range(eb):
        gu = _dot(h, w_ref[e]) + b_ref[e]
        glu = jnp.minimum(gu[:, :D_EXPERT], SWIGLU_LIMIT)
        lin = jnp.clip(gu[:, D_EXPERT:], -SWIGLU_LIMIT, SWIGLU_LIMIT)
        act = glu * jax.nn.sigmoid(SWIGLU_ALPHA * glu) * (lin + 1.0)
        ce = jnp.sum(jnp.where(lane == j * eb + e, comb, 0.0), axis=-1, keepdims=True)
        o_ref[:, e * D_EXPERT:(e + 1) * D_EXPERT] = (act * ce).astype(o_ref.dtype)


def _moe_up(h, w_gu, b_gu, comb):
    s, d = h.shape
    tm, eb = min(1024, s), 4
    return pl.pallas_call(
        functools.partial(_moe_up_kernel, eb=eb), grid=(s // tm, N_EXPERTS // eb),
        in_specs=[pl.BlockSpec((tm, d), lambda i, j: (i, 0)),
                  pl.BlockSpec((eb, d, 2 * D_EXPERT), lambda i, j: (j, 0, 0)),
                  pl.BlockSpec((eb, 1, 2 * D_EXPERT), lambda i, j: (j, 0, 0)),
                  pl.BlockSpec((tm, LANES), lambda i, j: (i, 0))],
        out_specs=pl.BlockSpec((tm, eb * D_EXPERT), lambda i, j: (i, j)),
        out_shape=jax.ShapeDtypeStruct((s, N_EXPERTS * D_EXPERT), BF16),
        compiler_params=_cparams(("parallel", "parallel"), 48), name="moe_up",
    )(h, w_gu, b_gu, comb)


def _reorder_w_in(w):
    d = w.shape[0]
    f0 = OFF_QLAT
    q0 = f0 + C_HEADS
    zeros = lambda n: jnp.zeros((d, n), w.dtype)
    return jnp.concatenate(
        [w[:, :f0], w[:, q0:], zeros(LANES - D_ROPE), w[:, f0:q0],
         zeros(N_P - OFF_F - C_HEADS)], axis=1)


def _reorder_w_q_b(w):
    r = w.shape[0]
    w = w.reshape(r, D_HEADS, D_QK)
    w = jnp.pad(w, ((0, 0), (0, 0), (0, AUG - D_QK)))
    return w.reshape(r, D_HEADS * AUG)


def _reorder_w_kv_b(w):
    r = w.shape[0]
    w = w.reshape(r, D_HEADS, D_NOPE + D_V)
    return jnp.concatenate([w[:, :, :D_NOPE].reshape(r, -1), w[:, :, D_NOPE:].reshape(r, -1)], axis=1)


def _pad_lanes(v, width, value=0.0):
    return jnp.pad(v, (0, width - v.shape[0]), constant_values=value).reshape(1, width)


def kernel(x, c, positions, ada_w, ada_b, ada_layer, norm1_g, norm2_g, w_in, a_q_norm, a_k_norm, a_rel_bias, b_q_norm, b_k_norm, b_sinks, c_q_norm, c_k_norm, c_f_bias, d_q_a_norm, d_w_q_b, d_kv_a_norm, d_w_kv_b, d_q_norm, d_k_norm, w_branch, w_gate, b_gate, w_out, router_w, router_b, w_gu, b_gu, w_dn, b_dn):
    batch, s, d = x.shape
    assert batch == 1, "kernels are written for a single sequence"
    depth = w_in.shape[0]
    xs = x.reshape(s, d)

    base_mod = _adaln(c.reshape(d, 1), ada_w, ada_b).reshape(6, d)
    cos_b, sin_b, cos_d, sin_d = _rope_tables(positions.reshape(s, 1).astype(F32))
    a_mask = _a_mask()

    for l in range(depth):
        mod = base_mod + ada_layer[l]
        shift1, scale1, gate1, shift2, scale2, gate2 = (mod[j:j + 1] for j in range(6))

        h = _norm(xs, norm1_g[l].reshape(1, d), scale1, shift1)
        p = _matmul(h, _reorder_w_in(w_in[l]).astype(BF16), BF16)
        gains = jnp.stack([a_q_norm[l], a_k_norm[l], b_q_norm[l], b_k_norm[l],
                           c_q_norm[l], c_k_norm[l], jnp.zeros_like(a_q_norm[l]),
                           jnp.zeros_like(a_q_norm[l])])
        (qa, ka, va, qb, kb, vb, qc, kc, vc, ql, kvl, kr) = _prep(
            p, gains, d_q_a_norm[l].reshape(1, -1), d_kv_a_norm[l].reshape(1, -1),
            _pad_lanes(c_f_bias[l], LANES), cos_b, sin_b)
        qd, kd, vd = _prep_d(
            ql, kvl, kr, _reorder_w_q_b(d_w_q_b[l]).astype(BF16),
            _reorder_w_kv_b(d_w_kv_b[l]).astype(BF16),
            _pad_lanes(d_q_norm[l], AUG), _pad_lanes(d_k_norm[l], AUG), cos_d, sin_d)

        o_a = _attn_a(qa, ka, va, _a_table(a_rel_bias[l]), a_mask)
        o_b = _attn_b(qb, kb, vb, b_sinks[l].astype(F32))
        o_c = _flash(qc, kc, vc, C_HEADS, 1)
        o_d = _flash(qd, kd, vd, D_HEADS, CHUNK)

        merged = _merge(h, (o_a, o_b, o_c, o_d), w_gate[l].astype(BF16),
                        b_gate[l].reshape(N_BRANCH, 1, d), w_branch[l].astype(BF16))
        xs = _resid(merged, w_out[l].astype(BF16), xs, gate1)

        rw = jnp.pad(router_w[l], ((0, 0), (0, LANES - N_EXPERTS)))
        rb = _pad_lanes(router_b[l].astype(F32), LANES, NEG_INF)
        h2, comb = _norm(xs, norm2_g[l].reshape(1, d), scale2, shift2, router=(rw, rb))
        act = _moe_up(h2, w_gu[l].astype(BF16), b_gu[l].reshape(N_EXPERTS, 1, 2 * D_EXPERT), comb)
        bdn = jnp.pad(b_dn[l], ((0, LANES - N_EXPERTS), (0, 0))).astype(BF16)
        xs = _resid(act, w_dn[l].reshape(N_EXPERTS * D_EXPERT, d).astype(BF16), xs, gate2,
                    comb=comb, b_dn=bdn)
    return xs.reshape(batch, s, d)
```

```python
import functools
import math

import numpy as np
import jax
import jax.numpy as jnp
from jax import lax
from jax.experimental import pallas as pl
from jax.experimental.pallas import tpu as pltpu

F32 = jnp.float32
BF16 = jnp.bfloat16

CHUNK = 64
HEAD_DIM = 128
N_BRANCH = 4
BRANCH_W = 1024
ROPE_THETA = 10000.0
EPS = 1e-6
NEG_INF = -1e30
A_HEADS = 8
A_PREV_CHUNKS = 8
REL_CLIP = 256
B_Q_HEADS = 8
B_KV_HEADS = 2
B_PREV_CHUNKS = 2
C_HEADS = 8
D_HEADS = 8
D_Q_LORA = 896
D_KV_LORA = 256
D_NOPE = 128
D_ROPE = 64
D_V = 128
D_QK = D_NOPE + D_ROPE
N_EXPERTS = 32
TOP_K = 4
D_EXPERT = 128
SWIGLU_LIMIT = 7.0
SWIGLU_ALPHA = 1.702

LANES = 128
V7X_VMEM_BYTES = 64 * 1024 * 1024

LOG2E = math.log2(math.e)

OFF_QA, OFF_KA, OFF_VA = 0, 1024, 2048
OFF_QB, OFF_KB, OFF_VB = 3072, 4096, 4352
OFF_QC, OFF_KC, OFF_VC = 4608, 5632, 6656
OFF_QLAT = 7680
OFF_KVLAT = OFF_QLAT + D_Q_LORA
OFF_KROPE = OFF_KVLAT + D_KV_LORA
OFF_F = OFF_KROPE + LANES
N_P = 9216

AUG = 2 * HEAD_DIM
FLASH_BLK = 1024
FLASH_ROW_CHUNK = 32
FLASH_HEADS = 2
PREP_ROWS = 256
BAND_Q = 256
A_WIN = BAND_Q + A_PREV_CHUNKS * CHUNK
B_WIN = BAND_Q + B_PREV_CHUNKS * CHUNK


def _cparams(sems, vmem_mb):
    return pltpu.CompilerParams(dimension_semantics=sems,
                                vmem_limit_bytes=vmem_mb * 1024 * 1024)


def _dot(a, b):
    return jnp.dot(a, b, preferred_element_type=F32)


def _dot_nt(a, b):
    return lax.dot_general(a, b, (((1,), (1,)), ((), ())), preferred_element_type=F32)


def _adaln_kernel(c_ref, w_ref, b_ref, o_ref):
    c = c_ref[...]
    s = c * jax.nn.sigmoid(c)
    o_ref[...] = jnp.sum(w_ref[...] * s, axis=0, keepdims=True) + b_ref[...]


def _adaln(c_col, ada_w, ada_b):
    d, n = ada_w.shape
    tn = min(512, n)
    return pl.pallas_call(
        _adaln_kernel,
        grid=(n // tn,),
        in_specs=[pl.BlockSpec((d, 1), lambda j: (0, 0)),
                  pl.BlockSpec((d, tn), lambda j: (0, j)),
                  pl.BlockSpec((1, tn), lambda j: (0, j))],
        out_specs=pl.BlockSpec((1, tn), lambda j: (0, j)),
        out_shape=jax.ShapeDtypeStruct((1, n), F32),
        compiler_params=_cparams(("parallel",), 40),
        name="adaln",
    )(c_col, ada_w, ada_b.reshape(1, n))


def _rope_table_kernel(pos_ref, fb_ref, fd_ref, cb_ref, sb_ref, cd_ref, sd_ref):
    pos = pos_ref[...]
    lane = lax.broadcasted_iota(jnp.int32, (1, LANES), 1)
    ab = pos * fb_ref[...]
    sb = jnp.sin(ab)
    cb_ref[...] = jnp.cos(ab)
    sb_ref[...] = jnp.where(lane < HEAD_DIM // 2, -sb, sb)
    ad = pos * fd_ref[...]
    sd = jnp.sin(ad)
    cd_ref[...] = jnp.where(lane < D_ROPE, jnp.cos(ad), 1.0)
    sd_ref[...] = jnp.where(lane < D_ROPE // 2, -sd, jnp.where(lane < D_ROPE, sd, 0.0))


def _rope_tables(pos_col):
    s = pos_col.shape[0]
    tm = min(512, s)
    lane = np.arange(LANES)
    half_b = HEAD_DIM // 2
    fb = ROPE_THETA ** (-(lane % half_b).astype(np.float64) / half_b)
    half_d = D_ROPE // 2
    fd = np.where(lane < D_ROPE, ROPE_THETA ** (-(lane % half_d).astype(np.float64) / half_d), 0.0)
    fb = jnp.asarray(fb, F32).reshape(1, LANES)
    fd = jnp.asarray(fd, F32).reshape(1, LANES)
    tab = jax.ShapeDtypeStruct((s, LANES), F32)
    row = pl.BlockSpec((tm, LANES), lambda i: (i, 0))
    const = pl.BlockSpec((1, LANES), lambda i: (0, 0))
    return pl.pallas_call(
        _rope_table_kernel,
        grid=(s // tm,),
        in_specs=[pl.BlockSpec((tm, 1), lambda i: (i, 0)), const, const],
        out_specs=[row, row, row, row],
        out_shape=[tab, tab, tab, tab],
        compiler_params=_cparams(("parallel",), 32),
        name="rope_tables",
    )(pos_col, fb, fd)


def _mod_norm(x, g, sc, sh):
    ms = jnp.mean(x * x, axis=-1, keepdims=True)
    y = x * lax.rsqrt(ms + EPS)
    return (y * g) * (1.0 + sc) + sh


def _norm_kernel(x_ref, g_ref, sc_ref, sh_ref, h_ref):
    h_ref[...] = _mod_norm(x_ref[...], g_ref[...], sc_ref[...], sh_ref[...]).astype(h_ref.dtype)


def _norm_router_kernel(x_ref, g_ref, sc_ref, sh_ref, rw_ref, rb_ref, h_ref, comb_ref):
    h = _mod_norm(x_ref[...], g_ref[...], sc_ref[...], sh_ref[...])
    h_ref[...] = h.astype(h_ref.dtype)
    logits = jnp.dot(h, rw_ref[...], precision=lax.Precision.HIGHEST,
                     preferred_element_type=F32) + rb_ref[...]
    lane = lax.broadcasted_iota(jnp.int32, logits.shape, 1).astype(F32)
    work = logits
    vals, sels = [], []
    for _ in range(TOP_K):
        m = jnp.max(work, axis=-1, keepdims=True)
        idx = jnp.min(jnp.where(work == m, lane, float(LANES)), axis=-1, keepdims=True)
        sel = lane == idx
        vals.append(m)
        sels.append(sel)
        work = jnp.where(sel, -3.0e38, work)
    es = [jnp.exp(v - vals[0]) for v in vals]
    inv = 1.0 / (es[0] + es[1] + es[2] + es[3])
    comb = jnp.zeros_like(logits)
    for e, sel in zip(es, sels):
        comb = comb + jnp.where(sel, e * inv, 0.0)
    comb_ref[...] = comb


def _norm(x, g, sc, sh, router=None):
    s, d = x.shape
    tm = min(256, s)
    row = pl.BlockSpec((tm, d), lambda i: (i, 0))
    vec = pl.BlockSpec((1, d), lambda i: (0, 0))
    if router is None:
        return pl.pallas_call(
            _norm_kernel, grid=(s // tm,),
            in_specs=[row, vec, vec, vec], out_specs=row,
            out_shape=jax.ShapeDtypeStruct((s, d), BF16),
            compiler_params=_cparams(("parallel",), 32), name="norm",
        )(x, g, sc, sh)
    rw, rb = router
    return pl.pallas_call(
        _norm_router_kernel, grid=(s // tm,),
        in_specs=[row, vec, vec, vec,
                  pl.BlockSpec((d, LANES), lambda i: (0, 0)),
                  pl.BlockSpec((1, LANES), lambda i: (0, 0))],
        out_specs=[row, pl.BlockSpec((tm, LANES), lambda i: (i, 0))],
        out_shape=[jax.ShapeDtypeStruct((s, d), BF16), jax.ShapeDtypeStruct((s, LANES), F32)],
        compiler_params=_cparams(("parallel",), 40), name="norm_router",
    )(x, g, sc, sh, rw, rb)


IN_TN = 512
IN_MAIN_TILES = OFF_QLAT // IN_TN


def _in_proj_kernel(a_ref, w_ref, wt_ref, o_ref):
    j = pl.program_id(1)

    @pl.when(j < IN_MAIN_TILES)
    def _():
        o_ref[...] = _dot(a_ref[...], w_ref[...].astype(BF16)).astype(o_ref.dtype)

    @pl.when(j >= IN_MAIN_TILES)
    def _():
        o_ref[...] = _dot(a_ref[...], wt_ref[...]).astype(o_ref.dtype)


def _in_proj(a, w_in, l, w_tail):
    m, k = a.shape
    tm = min(1024, m)
    n_tail = w_tail.shape[1] // IN_TN
    last = IN_MAIN_TILES - 1
    return pl.pallas_call(
        _in_proj_kernel, grid=(m // tm, IN_MAIN_TILES + n_tail),
        in_specs=[pl.BlockSpec((tm, k), lambda i, j: (i, 0)),
                  pl.BlockSpec((None, k, IN_TN), lambda i, j: (l, 0, jnp.minimum(j, last))),
                  pl.BlockSpec((k, IN_TN), lambda i, j: (0, jnp.maximum(j - IN_MAIN_TILES, 0)))],
        out_specs=pl.BlockSpec((tm, IN_TN), lambda i, j: (i, j)),
        out_shape=jax.ShapeDtypeStruct((m, N_P), BF16),
        compiler_params=_cparams(("parallel", "arbitrary"), 56), name="in_proj",
    )(a, w_in, w_tail)


def _prep_kernel(p_ref, gains_ref, gql_ref, gkvl_ref, fb_ref, cb_ref, sb_ref,
                 qa_ref, ka_ref, va_ref, qb_ref, kb_ref, vb_ref,
                 qc_ref, kc_ref, vc_ref, ql_ref, kvl_ref, kr_ref, carry_ref):
    tm = p_ref.shape[0]
    qscale = HEAD_DIM ** -0.5 * LOG2E

    @pl.when(pl.program_id(0) == 0)
    def _():
        carry_ref[...] = jnp.zeros_like(carry_ref)

    def head_norm(off, g_row, scale):
        x = p_ref[:, off:off + HEAD_DIM].astype(F32)
        ms = jnp.mean(x * x, axis=-1, keepdims=True)
        return x * lax.rsqrt(ms + EPS) * (gains_ref[g_row:g_row + 1, :] * scale)

    cb, sb = cb_ref[...], sb_ref[...]

    def rope(y):
        return y * cb + pltpu.roll(y, HEAD_DIM // 2, 1) * sb

    for h in range(A_HEADS):
        c = h * HEAD_DIM
        qa_ref[:, c:c + HEAD_DIM] = head_norm(OFF_QA + c, 0, qscale).astype(BF16)
        ka_ref[:, c:c + HEAD_DIM] = head_norm(OFF_KA + c, 1, 1.0).astype(BF16)
    va_ref[...] = p_ref[:, OFF_VA:OFF_VA + A_HEADS * HEAD_DIM]

    for h in range(B_Q_HEADS):
        c = h * HEAD_DIM
        qb_ref[:, c:c + HEAD_DIM] = rope(head_norm(OFF_QB + c, 2, qscale)).astype(BF16)
    for h in range(B_KV_HEADS):
        c = h * HEAD_DIM
        kb_ref[:, c:c + HEAD_DIM] = rope(head_norm(OFF_KB + c, 3, 1.0)).astype(BF16)
    vb_ref[...] = p_ref[:, OFF_VB:OFF_VB + B_KV_HEADS * HEAD_DIM]

    z = p_ref[:, OFF_F:OFF_F + LANES].astype(F32) + fb_ref[...]
    logf = jnp.minimum(z, 0.0) - jnp.log1p(jnp.exp(-jnp.abs(z)))
    r_i = lax.broadcasted_iota(jnp.int32, (tm, tm), 0)
    c_i = lax.broadcasted_iota(jnp.int32, (tm, tm), 1)
    tri = jnp.where(r_i >= c_i, 1.0, 0.0).astype(F32)
    cum = jnp.dot(tri, logf, precision=lax.Precision.HIGHEST,
                  preferred_element_type=F32) + carry_ref[...]
    carry_ref[...] = cum[tm - 1:tm, :]
    cum2 = cum * LOG2E
    hi = cum2.astype(BF16).astype(F32)
    r1 = cum2 - hi
    mid = r1.astype(BF16).astype(F32)
    lo = r1 - mid
    lane = lax.broadcasted_iota(jnp.int32, (tm, LANES), 1)
    for h in range(C_HEADS):
        c = h * HEAD_DIM
        a = h * AUG
        hi_h, mid_h, lo_h = hi[:, h:h + 1], mid[:, h:h + 1], lo[:, h:h + 1]
        aug_q = jnp.where(lane == 0, hi_h, jnp.where(lane == 1, mid_h, jnp.where(
            lane == 2, lo_h, jnp.where(lane < 6, 1.0, 0.0))))
        aug_k = jnp.where(lane < 3, 1.0, jnp.where(lane == 3, -hi_h, jnp.where(
            lane == 4, -mid_h, jnp.where(lane == 5, -lo_h, 0.0))))
        qc_ref[:, a:a + HEAD_DIM] = head_norm(OFF_QC + c, 4, qscale).astype(BF16)
        qc_ref[:, a + HEAD_DIM:a + AUG] = aug_q.astype(BF16)
        kc_ref[h, :HEAD_DIM, :] = head_norm(OFF_KC + c, 5, 1.0).T.astype(BF16)
        kc_ref[h, HEAD_DIM:, :] = aug_k.T.astype(BF16)
    vc_ref[...] = p_ref[:, OFF_VC:OFF_VC + C_HEADS * HEAD_DIM]

    xq = p_ref[:, OFF_QLAT:OFF_QLAT + D_Q_LORA].astype(F32)
    ms = jnp.mean(xq * xq, axis=-1, keepdims=True)
    ql_ref[...] = (xq * lax.rsqrt(ms + EPS) * gql_ref[...]).astype(BF16)
    xkv = p_ref[:, OFF_KVLAT:OFF_KVLAT + D_KV_LORA].astype(F32)
    ms = jnp.mean(xkv * xkv, axis=-1, keepdims=True)
    kvl_ref[...] = (xkv * lax.rsqrt(ms + EPS) * gkvl_ref[...]).astype(BF16)
    kr_ref[...] = p_ref[:, OFF_KROPE:OFF_KROPE + LANES].astype(F32)


def _kt_shape_spec(n_heads, s, tm):
    blk = min(FLASH_BLK, s)
    per = blk // tm
    shape = jax.ShapeDtypeStruct((n_heads, s // blk, AUG, blk), BF16)
    spec = pl.BlockSpec((n_heads, None, AUG, tm), lambda i: (0, i // per, 0, i % per))
    return shape, spec


def _prep(p, gains, gql, gkvl, fb, cos_b, sin_b):
    s = p.shape[0]
    tm = min(PREP_ROWS, s)

    def row(w):
        return pl.BlockSpec((tm, w), lambda i: (i, 0))

    def const(r, w):
        return pl.BlockSpec((r, w), lambda i: (0, 0))

    widths = [1024, 1024, 1024, 1024, 256, 256, C_HEADS * AUG, None, 1024,
              D_Q_LORA, D_KV_LORA]
    out_shape = [jax.ShapeDtypeStruct((s, w), BF16) for w in widths if w]
    out_specs = [row(w) for w in widths if w]
    kt_shape, kt_spec = _kt_shape_spec(C_HEADS, s, tm)
    out_shape.insert(7, kt_shape)
    out_specs.insert(7, kt_spec)
    out_shape.append(jax.ShapeDtypeStruct((s, LANES), F32))
    out_specs.append(row(LANES))
    return pl.pallas_call(
        _prep_kernel, grid=(s // tm,),
        in_specs=[row(N_P), const(8, LANES), const(1, D_Q_LORA), const(1, D_KV_LORA),
                  const(1, LANES), row(LANES), row(LANES)],
        out_specs=out_specs, out_shape=out_shape,
        scratch_shapes=[pltpu.VMEM((1, LANES), F32)],
        compiler_params=_cparams(("arbitrary",), 48), name="prep",
    )(p, gains, gql, gkvl, fb, cos_b, sin_b)


def _rope_d(y, cd, sd):
    lane = lax.broadcasted_iota(jnp.int32, y.shape, 1)
    half = D_ROPE // 2
    partner = jnp.where(lane < half, pltpu.roll(y, LANES - half, 1), pltpu.roll(y, half, 1))
    return y * cd + partner * sd


def _prep_d_kernel(ql_ref, kvl_ref, kr_ref, wq_ref, wkv_ref, gq_ref, gk_ref, cd_ref, sd_ref,
                   qd_ref, kd_ref, vd_ref):
    qscale = D_QK ** -0.5 * LOG2E
    cd, sd = cd_ref[...], sd_ref[...]
    q = _dot(ql_ref[...], wq_ref[...])
    kv = _dot(kvl_ref[...], wkv_ref[...])
    kr = kr_ref[...]
    ss_r = jnp.sum(kr * kr, axis=-1, keepdims=True)
    gq = gq_ref[...]
    gk = gk_ref[...]
    for h in range(D_HEADS):
        a = h * AUG
        qn, qr = q[:, a:a + HEAD_DIM], q[:, a + HEAD_DIM:a + AUG]
        ms = (jnp.sum(qn * qn, axis=-1, keepdims=True)
              + jnp.sum(qr * qr, axis=-1, keepdims=True)) * (1.0 / D_QK)
        r = lax.rsqrt(ms + EPS) * qscale
        qd_ref[:, a:a + HEAD_DIM] = (qn * r * gq[:, :HEAD_DIM]).astype(BF16)
        qd_ref[:, a + HEAD_DIM:a + AUG] = _rope_d(qr * r * gq[:, HEAD_DIM:], cd, sd).astype(BF16)
        kn = kv[:, h * D_NOPE:(h + 1) * D_NOPE]
        ms = (jnp.sum(kn * kn, axis=-1, keepdims=True) + ss_r) * (1.0 / D_QK)
        r = lax.rsqrt(ms + EPS)
        kd_ref[h, :HEAD_DIM, :] = (kn * r * gk[:, :HEAD_DIM]).T.astype(BF16)
        kd_ref[h, HEAD_DIM:, :] = _rope_d(kr * r * gk[:, HEAD_DIM:], cd, sd).T.astype(BF16)
    vd_ref[...] = kv[:, D_HEADS * D_NOPE:].astype(BF16)


def _prep_d(ql, kvl, kr, wq, wkv, gq, gk, cos_d, sin_d):
    s = ql.shape[0]
    tm = min(PREP_ROWS, s)

    def row(w):
        return pl.BlockSpec((tm, w), lambda i: (i, 0))

    def const(r, w):
        return pl.BlockSpec((r, w), lambda i: (0, 0))

    kt_shape, kt_spec = _kt_shape_spec(D_HEADS, s, tm)
    return pl.pallas_call(
        _prep_d_kernel, grid=(s // tm,),
        in_specs=[row(D_Q_LORA), row(D_KV_LORA), row(LANES),
                  const(D_Q_LORA, D_HEADS * AUG), const(D_KV_LORA, 2 * D_HEADS * D_NOPE),
                  const(1, AUG), const(1, AUG), row(LANES), row(LANES)],
        out_specs=[row(D_HEADS * AUG), kt_spec, row(D_HEADS * D_V)],
        out_shape=[jax.ShapeDtypeStruct((s, D_HEADS * AUG), BF16), kt_shape,
                   jax.ShapeDtypeStruct((s, D_HEADS * D_V), BF16)],
        compiler_params=_cparams(("parallel",), 40), name="prep_d",
    )(ql, kvl, kr, wq, wkv, gq, gk, cos_d, sin_d)


def _flash_kernel(q_ref, kt_ref, v_ref, o_ref, m_ref, l_ref, acc_ref, *, blk, unit):
    i = pl.program_id(1)
    m_ref[...] = jnp.full_like(m_ref, NEG_INF)
    l_ref[...] = jnp.zeros_like(l_ref)
    acc_ref[...] = jnp.zeros_like(acc_ref)
    nct = blk // LANES
    rc = FLASH_ROW_CHUNK

    def step(j, diagonal):
        start = pl.multiple_of(j * blk, blk)
        for g in range(FLASH_HEADS):
            s = _dot(q_ref[:, g * AUG:(g + 1) * AUG], kt_ref[g, j])
            v = v_ref[pl.ds(start, blk), g * LANES:(g + 1) * LANES]
            ps, alphas = [], []
            for r in range(blk // rc):
                rows = slice(r * rc, (r + 1) * rc)
                sc = s[rows, :]
                if diagonal:
                    qi = lax.broadcasted_iota(jnp.int32, sc.shape, 0) + r * rc
                    ki = lax.broadcasted_iota(jnp.int32, sc.shape, 1)
                    if unit > 1:
                        shift = unit.bit_length() - 1
                        qi, ki = qi >> shift, ki >> shift
                    sc = jnp.where(ki <= qi, sc, NEG_INF)
                m_prev = m_ref[g, rows, :]
                m_new = jnp.maximum(m_prev, jnp.max(sc, axis=-1, keepdims=True))
                alpha = jnp.exp2(m_prev - m_new)
                pc = [jnp.exp2(sc[:, c * LANES:(c + 1) * LANES] - m_new) for c in range(nct)]
                lsum = pc[0]
                for c in range(1, nct):
                    lsum = lsum + pc[c]
                l_ref[g, rows, :] = alpha * l_ref[g, rows, :] + lsum
                m_ref[g, rows, :] = m_new
                ps.append(jnp.concatenate(pc, axis=1).astype(BF16))
                alphas.append(alpha)
            p = jnp.concatenate(ps, axis=0)
            alpha = jnp.concatenate(alphas, axis=0)
            acc_ref[g] = alpha * acc_ref[g] + _dot(p, v)

    def body(j, carry):
        step(j, False)
        return carry

    lax.fori_loop(0, i, body, 0)
    step(i, True)
    for g in range(FLASH_HEADS):
        l = jnp.sum(l_ref[g], axis=-1, keepdims=True)
        o_ref[:, g * LANES:(g + 1) * LANES] = (acc_ref[g] / l).astype(o_ref.dtype)


def _flash(q, kt, v, n_heads, unit):
    s = q.shape[0]
    blk = kt.shape[3]
    dv = v.shape[1] // n_heads
    g = FLASH_HEADS
    assert dv == LANES and n_heads % g == 0
    once = pl.Buffered(1)
    return pl.pallas_call(
        functools.partial(_flash_kernel, blk=blk, unit=unit),
        grid=(n_heads // g, s // blk),
        in_specs=[pl.BlockSpec((blk, g * AUG), lambda h, i: (i, h)),
                  pl.BlockSpec((g, s // blk, AUG, blk), lambda h, i: (h, 0, 0, 0), pipeline_mode=once),
                  pl.BlockSpec((s, g * dv), lambda h, i: (0, h), pipeline_mode=once)],
        out_specs=pl.BlockSpec((blk, g * dv), lambda h, i: (i, h)),
        out_shape=jax.ShapeDtypeStruct((s, n_heads * dv), BF16),
        scratch_shapes=[pltpu.VMEM((g, blk, LANES), F32), pltpu.VMEM((g, blk, LANES), F32),
                        pltpu.VMEM((g, blk, dv), F32)],
        compiler_params=_cparams(("parallel", "arbitrary"), 56), name=f"flash_u{unit}",
    )(q, kt, v)


A_PREV_BLOCKS = A_PREV_CHUNKS * CHUNK // BAND_Q


def _attn_a_kernel(q_ref, k0_ref, k1_ref, k2_ref, v0_ref, v1_ref, v2_ref, t_ref, mk_ref, o_ref):
    mask = mk_ref[...]
    for h in range(A_HEADS):
        cols = slice(h * HEAD_DIM, (h + 1) * HEAD_DIM)
        k = jnp.concatenate([k0_ref[:, cols], k1_ref[:, cols], k2_ref[:, cols]], axis=0)
        v = jnp.concatenate([v0_ref[:, cols], v1_ref[:, cols], v2_ref[:, cols]], axis=0)
        s = _dot_nt(q_ref[:, cols], k) + t_ref[h] + mask
        m = jnp.max(s, axis=-1, keepdims=True)
        p = jnp.exp2(s - m)
        l = jnp.sum(p, axis=-1, keepdims=True)
        o_ref[:, cols] = (_dot(p.astype(BF16), v) / l).astype(o_ref.dtype)


def _attn_a(q, k, v, table, mask):
    s, w = q.shape
    row = pl.BlockSpec((BAND_Q, w), lambda i: (i, 0))
    prev = [pl.BlockSpec((BAND_Q, w), functools.partial(
        lambda i, back: (jnp.maximum(i - back, 0), 0), back=back))
        for back in range(A_PREV_BLOCKS, -1, -1)]
    return pl.pallas_call(
        _attn_a_kernel, grid=(s // BAND_Q,),
        in_specs=[row] + prev + prev + [
            pl.BlockSpec((A_HEADS, BAND_Q, A_WIN), lambda i: (0, 0, 0)),
            pl.BlockSpec((None, BAND_Q, A_WIN), lambda i: (jnp.minimum(i, A_PREV_BLOCKS), 0, 0))],
        out_specs=row,
        out_shape=jax.ShapeDtypeStruct((s, w), BF16),
        compiler_params=_cparams(("arbitrary",), 40), name="attn_a",
    )(q, k, k, k, v, v, v, table, mask)


def _a_mask():
    b = np.arange(A_PREV_BLOCKS + 1)[:, None, None]
    q = np.arange(BAND_Q)[None, :, None]
    w = np.arange(A_WIN)[None, None, :]
    dchunk = (q + A_PREV_BLOCKS * BAND_Q) // CHUNK - w // CHUNK
    valid = (dchunk >= 0) & (dchunk <= A_PREV_CHUNKS) & (w >= (A_PREV_BLOCKS - b) * BAND_Q)
    return jnp.asarray(np.where(valid, 0.0, NEG_INF), F32)


def _a_table(rel_bias):
    n, m = BAND_Q, A_WIN
    u = np.arange(n + m)
    dist = A_PREV_BLOCKS * BAND_Q + (n - 1) - u
    idx = np.clip(dist, -REL_CLIP, REL_CLIP) + REL_CLIP
    z = rel_bias.astype(F32).T[:, jnp.asarray(idx)] * LOG2E
    skew = jnp.broadcast_to(z[:, None, :], (A_HEADS, n, n + m))
    skew = skew.reshape(A_HEADS, n * (n + m))[:, :n * (n + m - 1)]
    return skew.reshape(A_HEADS, n, n + m - 1)[:, :, n - 1:n - 1 + m]


def _attn_b_kernel(sink_ref, q_ref, k_ref, v_ref, o_ref):
    g = pl.program_id(0)
    i = pl.program_id(1)
    group = B_Q_HEADS // B_KV_HEADS
    start = pl.multiple_of(jnp.maximum(i * BAND_Q - B_PREV_CHUNKS * CHUNK, 0), LANES)
    k = k_ref[pl.ds(start, B_WIN), :]
    v = v_ref[pl.ds(start, B_WIN), :]
    shift = CHUNK.bit_length() - 1
    t = (i * BAND_Q + lax.broadcasted_iota(jnp.int32, (BAND_Q, B_WIN), 0)) >> shift
    kp = (start + lax.broadcasted_iota(jnp.int32, (BAND_Q, B_WIN), 1)) >> shift
    valid = jnp.abs(t - kp - 1) <= 1
    for hi in range(group):
        c = hi * HEAD_DIM
        s = jnp.where(valid, _dot_nt(q_ref[:, c:c + HEAD_DIM], k), NEG_INF)
        sink = sink_ref[g * group + hi] * LOG2E
        m = jnp.maximum(jnp.max(s, axis=-1, keepdims=True), sink)
        p = jnp.exp2(s - m)
        l = jnp.sum(p, axis=-1, keepdims=True) + jnp.exp2(sink - m)
        o_ref[:, c:c + HEAD_DIM] = (_dot(p.astype(BF16), v) / l).astype(o_ref.dtype)


def _attn_b(q, k, v, sinks):
    s = q.shape[0]
    gw = (B_Q_HEADS // B_KV_HEADS) * HEAD_DIM
    return pl.pallas_call(
        _attn_b_kernel, grid=(B_KV_HEADS, s // BAND_Q),
        in_specs=[pl.BlockSpec(memory_space=pltpu.SMEM),
                  pl.BlockSpec((BAND_Q, gw), lambda g, i: (i, g)),
                  pl.BlockSpec((s, HEAD_DIM), lambda g, i: (0, g)),
                  pl.BlockSpec((s, HEAD_DIM), lambda g, i: (0, g))],
        out_specs=pl.BlockSpec((BAND_Q, gw), lambda g, i: (i, g)),
        out_shape=jax.ShapeDtypeStruct((s, B_Q_HEADS * HEAD_DIM), BF16),
        compiler_params=_cparams(("parallel", "arbitrary"), 40), name="attn_b",
    )(sinks, q, k, v)


def _merge_kernel(h_ref, oa_ref, ob_ref, oc_ref, od_ref, wg_ref, bg_ref, wb_ref, out_ref, acc_ref):
    b = pl.program_id(2)

    for n, o_ref in enumerate((oa_ref, ob_ref, oc_ref, od_ref)):
        @pl.when(b == n)
        def _(o_ref=o_ref, n=n):
            gate = jax.nn.sigmoid(_dot(h_ref[...], wg_ref[...]) + bg_ref[...])
            val = gate * _dot(o_ref[...], wb_ref[...].astype(BF16))
            if n == 0:
                acc_ref[...] = val
            elif n < N_BRANCH - 1:
                acc_ref[...] += val
            else:
                out_ref[...] = (acc_ref[...] + val).astype(out_ref.dtype)


def _merge(h, outs, wg, bg, wb, l):
    s, d = h.shape
    tm, tn = min(1024, s), min(512, d)
    o_spec = pl.BlockSpec((tm, BRANCH_W), lambda i, j, b: (i, 0), pipeline_mode=pl.Buffered(1))
    return pl.pallas_call(
        _merge_kernel, grid=(s // tm, d // tn, N_BRANCH),
        in_specs=[pl.BlockSpec((tm, d), lambda i, j, b: (i, 0)),
                  o_spec, o_spec, o_spec, o_spec,
                  pl.BlockSpec((None, d, tn), lambda i, j, b: (b, 0, j)),
                  pl.BlockSpec((None, 1, tn), lambda i, j, b: (b, 0, j)),
                  pl.BlockSpec((None, None, BRANCH_W, tn), lambda i, j, b: (l, b, 0, j))],
        out_specs=pl.BlockSpec((tm, tn), lambda i, j, b: (i, j)),
        out_shape=jax.ShapeDtypeStruct((s, d), BF16),
        scratch_shapes=[pltpu.VMEM((tm, tn), F32)],
        compiler_params=_cparams(("parallel", "parallel", "arbitrary"), 56), name="merge",
    )(h, *outs, wg, bg, wb)


def _resid_kernel(a_ref, w_ref, x_ref, g_ref, o_ref):
    o_ref[...] = x_ref[...] + g_ref[...] * _dot(a_ref[...], w_ref[...].astype(BF16))


def _resid_moe_kernel(a_ref, w_ref, c_ref, bd_ref, x_ref, g_ref, o_ref):
    y = _dot(a_ref[...], w_ref[...].astype(BF16)) + _dot(c_ref[...].astype(BF16), bd_ref[...])
    o_ref[...] = x_ref[...] + g_ref[...] * y


def _resid(a, w, l, x, gate, comb=None, b_dn=None):
    s, k = a.shape
    d = w.shape[2]
    tm, tn = min(1024, s), min(512, d)
    a_spec = pl.BlockSpec((tm, k), lambda i, j: (i, 0))
    w_spec = pl.BlockSpec((None, k, tn), lambda i, j: (l, 0, j))
    x_spec = pl.BlockSpec((tm, tn), lambda i, j: (i, j))
    g_spec = pl.BlockSpec((1, tn), lambda i, j: (0, j))
    if comb is None:
        kern, ins = _resid_kernel, (a, w, x, gate)
        in_specs = [a_spec, w_spec, x_spec, g_spec]
    else:
        kern, ins = _resid_moe_kernel, (a, w, comb, b_dn, x, gate)
        in_specs = [a_spec, w_spec, pl.BlockSpec((tm, LANES), lambda i, j: (i, 0)),
                    pl.BlockSpec((LANES, tn), lambda i, j: (0, j)), x_spec, g_spec]
    return pl.pallas_call(
        kern, grid=(s // tm, d // tn), in_specs=in_specs, out_specs=x_spec,
        out_shape=jax.ShapeDtypeStruct((s, d), F32),
        compiler_params=_cparams(("parallel", "parallel"), 48), name="resid",
    )(*ins)


def _moe_up_kernel(h_ref, w_ref, b_ref, c_ref, o_ref, *, eb):
    j = pl.program_id(1)
    h = h_ref[...]
    comb = c_ref[...]
    lane = lax.broadcasted_iota(jnp.int32, comb.shape, 1)
    for e in range(eb):
        gu = _dot(h, w_ref[e].astype(BF16)) + b_ref[e]
        glu = jnp.minimum(gu[:, :D_EXPERT], SWIGLU_LIMIT)
        lin = jnp.clip(gu[:, D_EXPERT:], -SWIGLU_LIMIT, SWIGLU_LIMIT)
        act = glu * jax.nn.sigmoid(SWIGLU_ALPHA * glu) * (lin + 1.0)
        ce = jnp.sum(jnp.where(lane == j * eb + e, comb, 0.0), axis=-1, keepdims=True)
        o_ref[:, e * D_EXPERT:(e + 1) * D_EXPERT] = (act * ce).astype(o_ref.dtype)


def _moe_up(h, w_gu, l, b_gu, comb):
    s, d = h.shape
    tm, eb = min(1024, s), 2
    return pl.pallas_call(
        functools.partial(_moe_up_kernel, eb=eb), grid=(s // tm, N_EXPERTS // eb),
        in_specs=[pl.BlockSpec((tm, d), lambda i, j: (i, 0)),
                  pl.BlockSpec((None, eb, d, 2 * D_EXPERT), lambda i, j: (l, j, 0, 0)),
                  pl.BlockSpec((eb, 1, 2 * D_EXPERT), lambda i, j: (j, 0, 0)),
                  pl.BlockSpec((tm, LANES), lambda i, j: (i, 0))],
        out_specs=pl.BlockSpec((tm, eb * D_EXPERT), lambda i, j: (i, j)),
        out_shape=jax.ShapeDtypeStruct((s, N_EXPERTS * D_EXPERT), BF16),
        compiler_params=_cparams(("parallel", "parallel"), 48), name="moe_up",
    )(h, w_gu, b_gu, comb)


def _w_in_tail(w):
    d = w.shape[0]
    f0 = OFF_QLAT
    q0 = f0 + C_HEADS
    zeros = lambda n: jnp.zeros((d, n), BF16)
    return jnp.concatenate(
        [w[:, q0:].astype(BF16), zeros(LANES - D_ROPE), w[:, f0:q0].astype(BF16),
         zeros(N_P - OFF_F - C_HEADS)], axis=1)


def _reorder_w_q_b(w):
    r = w.shape[0]
    w = w.reshape(r, D_HEADS, D_QK)
    w = jnp.pad(w, ((0, 0), (0, 0), (0, AUG - D_QK)))
    return w.reshape(r, D_HEADS * AUG)


def _reorder_w_kv_b(w):
    r = w.shape[0]
    w = w.reshape(r, D_HEADS, D_NOPE + D_V)
    return jnp.concatenate([w[:, :, :D_NOPE].reshape(r, -1), w[:, :, D_NOPE:].reshape(r, -1)], axis=1)


def _pad_lanes(v, width, value=0.0):
    return jnp.pad(v, (0, width - v.shape[0]), constant_values=value).reshape(1, width)


def kernel(x, c, positions, ada_w, ada_b, ada_layer, norm1_g, norm2_g, w_in, a_q_norm, a_k_norm, a_rel_bias, b_q_norm, b_k_norm, b_sinks, c_q_norm, c_k_norm, c_f_bias, d_q_a_norm, d_w_q_b, d_kv_a_norm, d_w_kv_b, d_q_norm, d_k_norm, w_branch, w_gate, b_gate, w_out, router_w, router_b, w_gu, b_gu, w_dn, b_dn):
    batch, s, d = x.shape
    assert batch == 1, "kernels are written for a single sequence"
    depth = w_in.shape[0]
    xs = x.reshape(s, d)

    base_mod = _adaln(c.reshape(d, 1), ada_w, ada_b).reshape(6, d)
    cos_b, sin_b, cos_d, sin_d = _rope_tables(positions.reshape(s, 1).astype(F32))
    a_mask = _a_mask()

    for l in range(depth):
        mod = base_mod + ada_layer[l]
        shift1, scale1, gate1, shift2, scale2, gate2 = (mod[j:j + 1] for j in range(6))

        h = _norm(xs, norm1_g[l].reshape(1, d), scale1, shift1)
        p = _in_proj(h, w_in, l, _w_in_tail(w_in[l]))
        gains = jnp.stack([a_q_norm[l], a_k_norm[l], b_q_norm[l], b_k_norm[l],
                           c_q_norm[l], c_k_norm[l], jnp.zeros_like(a_q_norm[l]),
                           jnp.zeros_like(a_q_norm[l])])
        (qa, ka, va, qb, kb, vb, qc, kc, vc, ql, kvl, kr) = _prep(
            p, gains, d_q_a_norm[l].reshape(1, -1), d_kv_a_norm[l].reshape(1, -1),
            _pad_lanes(c_f_bias[l], LANES), cos_b, sin_b)
        qd, kd, vd = _prep_d(
            ql, kvl, kr, _reorder_w_q_b(d_w_q_b[l]).astype(BF16),
            _reorder_w_kv_b(d_w_kv_b[l]).astype(BF16),
            _pad_lanes(d_q_norm[l], AUG), _pad_lanes(d_k_norm[l], AUG), cos_d, sin_d)

        o_a = _attn_a(qa, ka, va, _a_table(a_rel_bias[l]), a_mask)
        o_b = _attn_b(qb, kb, vb, b_sinks[l].astype(F32))
        o_c = _flash(qc, kc, vc, C_HEADS, 1)
        o_d = _flash(qd, kd, vd, D_HEADS, CHUNK)

        merged = _merge(h, (o_a, o_b, o_c, o_d), w_gate[l].astype(BF16),
                        b_gate[l].reshape(N_BRANCH, 1, d), w_branch, l)
        xs = _resid(merged, w_out, l, xs, gate1)

        rw = jnp.pad(router_w[l], ((0, 0), (0, LANES - N_EXPERTS)))
        rb = _pad_lanes(router_b[l].astype(F32), LANES, NEG_INF)
        h2, comb = _norm(xs, norm2_g[l].reshape(1, d), scale2, shift2, router=(rw, rb))
        act = _moe_up(h2, w_gu, l, b_gu[l].reshape(N_EXPERTS, 1, 2 * D_EXPERT), comb)
        bdn = jnp.pad(b_dn[l], ((0, LANES - N_EXPERTS), (0, 0))).astype(BF16)
        xs = _resid(act, w_dn.reshape(depth, N_EXPERTS * D_EXPERT, d), l, xs, gate2,
                    comb=comb, b_dn=bdn)
    return xs.reshape(batch, s, d)
```

```python
import functools
import math

import numpy as np
import jax
import jax.numpy as jnp
from jax import lax
from jax.experimental import pallas as pl
from jax.experimental.pallas import tpu as pltpu

F32 = jnp.float32
BF16 = jnp.bfloat16

CHUNK = 64
HEAD_DIM = 128
N_BRANCH = 4
BRANCH_W = 1024
ROPE_THETA = 10000.0
EPS = 1e-6
NEG_INF = -1e30
A_HEADS = 8
A_PREV_CHUNKS = 8
REL_CLIP = 256
B_Q_HEADS = 8
B_KV_HEADS = 2
B_PREV_CHUNKS = 2
C_HEADS = 8
D_HEADS = 8
D_Q_LORA = 896
D_KV_LORA = 256
D_NOPE = 128
D_ROPE = 64
D_V = 128
D_QK = D_NOPE + D_ROPE
N_EXPERTS = 32
TOP_K = 4
D_EXPERT = 128
SWIGLU_LIMIT = 7.0
SWIGLU_ALPHA = 1.702

LANES = 128
V7X_VMEM_BYTES = 64 * 1024 * 1024

LOG2E = math.log2(math.e)

OFF_QA, OFF_KA, OFF_VA = 0, 1024, 2048
OFF_QB, OFF_KB, OFF_VB = 3072, 4096, 4352
OFF_QC, OFF_KC, OFF_VC = 4608, 5632, 6656
OFF_QLAT = 7680
OFF_KVLAT = OFF_QLAT + D_Q_LORA
OFF_KROPE = OFF_KVLAT + D_KV_LORA
OFF_F = OFF_KROPE + LANES
N_P = 9216

AUG = 2 * HEAD_DIM
FLASH_BLK = 1024
FLASH_ROW_CHUNK = 32
FLASH_HEADS = 2
PREP_ROWS = 256
BAND_Q = 256
A_WIN = BAND_Q + A_PREV_CHUNKS * CHUNK
B_WIN = BAND_Q + B_PREV_CHUNKS * CHUNK


def _cparams(sems, vmem_mb):
    return pltpu.CompilerParams(dimension_semantics=sems,
                                vmem_limit_bytes=vmem_mb * 1024 * 1024)


def _dot(a, b):
    return jnp.dot(a, b, preferred_element_type=F32)


def _dot_nt(a, b):
    return lax.dot_general(a, b, (((1,), (1,)), ((), ())), preferred_element_type=F32)


def _adaln_kernel(c_ref, w_ref, b_ref, o_ref):
    c = c_ref[...]
    s = c * jax.nn.sigmoid(c)
    o_ref[...] = jnp.sum(w_ref[...] * s, axis=0, keepdims=True) + b_ref[...]


def _adaln(c_col, ada_w, ada_b):
    d, n = ada_w.shape
    tn = min(512, n)
    return pl.pallas_call(
        _adaln_kernel,
        grid=(n // tn,),
        in_specs=[pl.BlockSpec((d, 1), lambda j: (0, 0)),
                  pl.BlockSpec((d, tn), lambda j: (0, j)),
                  pl.BlockSpec((1, tn), lambda j: (0, j))],
        out_specs=pl.BlockSpec((1, tn), lambda j: (0, j)),
        out_shape=jax.ShapeDtypeStruct((1, n), F32),
        compiler_params=_cparams(("parallel",), 40),
        name="adaln",
    )(c_col, ada_w, ada_b.reshape(1, n))


def _rope_table_kernel(pos_ref, fb_ref, fd_ref, cb_ref, sb_ref, cd_ref, sd_ref):
    pos = pos_ref[...]
    lane = lax.broadcasted_iota(jnp.int32, (1, LANES), 1)
    ab = pos * fb_ref[...]
    sb = jnp.sin(ab)
    cb_ref[...] = jnp.cos(ab)
    sb_ref[...] = jnp.where(lane < HEAD_DIM // 2, -sb, sb)
    ad = pos * fd_ref[...]
    sd = jnp.sin(ad)
    cd_ref[...] = jnp.where(lane < D_ROPE, jnp.cos(ad), 1.0)
    sd_ref[...] = jnp.where(lane < D_ROPE // 2, -sd, jnp.where(lane < D_ROPE, sd, 0.0))


def _rope_tables(pos_col):
    s = pos_col.shape[0]
    tm = min(512, s)
    lane = np.arange(LANES)
    half_b = HEAD_DIM // 2
    fb = ROPE_THETA ** (-(lane % half_b).astype(np.float64) / half_b)
    half_d = D_ROPE // 2
    fd = np.where(lane < D_ROPE, ROPE_THETA ** (-(lane % half_d).astype(np.float64) / half_d), 0.0)
    fb = jnp.asarray(fb, F32).reshape(1, LANES)
    fd = jnp.asarray(fd, F32).reshape(1, LANES)
    tab = jax.ShapeDtypeStruct((s, LANES), F32)
    row = pl.BlockSpec((tm, LANES), lambda i: (i, 0))
    const = pl.BlockSpec((1, LANES), lambda i: (0, 0))
    return pl.pallas_call(
        _rope_table_kernel,
        grid=(s // tm,),
        in_specs=[pl.BlockSpec((tm, 1), lambda i: (i, 0)), const, const],
        out_specs=[row, row, row, row],
        out_shape=[tab, tab, tab, tab],
        compiler_params=_cparams(("parallel",), 32),
        name="rope_tables",
    )(pos_col, fb, fd)


def _mod_norm(x, g, sc, sh):
    ms = jnp.mean(x * x, axis=-1, keepdims=True)
    y = x * lax.rsqrt(ms + EPS)
    return (y * g) * (1.0 + sc) + sh


def _norm_kernel(x_ref, g_ref, sc_ref, sh_ref, h_ref):
    h_ref[...] = _mod_norm(x_ref[...], g_ref[...], sc_ref[...], sh_ref[...]).astype(h_ref.dtype)


def _norm_router_kernel(x_ref, g_ref, sc_ref, sh_ref, rwh_ref, rwl_ref, rb_ref, h_ref, comb_ref):
    h = _mod_norm(x_ref[...], g_ref[...], sc_ref[...], sh_ref[...])
    h_hi = h.astype(BF16)
    h_ref[...] = h_hi
    h_lo = (h - h_hi.astype(F32)).astype(BF16)
    rw_hi = rwh_ref[...]
    logits = (_dot(h_hi, rw_hi) + _dot(h_lo, rw_hi) + _dot(h_hi, rwl_ref[...])
              + rb_ref[...])
    lane = lax.broadcasted_iota(jnp.int32, logits.shape, 1).astype(F32)
    work = logits
    vals, sels = [], []
    for _ in range(TOP_K):
        m = jnp.max(work, axis=-1, keepdims=True)
        idx = jnp.min(jnp.where(work == m, lane, float(LANES)), axis=-1, keepdims=True)
        sel = lane == idx
        vals.append(m)
        sels.append(sel)
        work = jnp.where(sel, -3.0e38, work)
    es = [jnp.exp(v - vals[0]) for v in vals]
    inv = 1.0 / (es[0] + es[1] + es[2] + es[3])
    comb = jnp.zeros_like(logits)
    for e, sel in zip(es, sels):
        comb = comb + jnp.where(sel, e * inv, 0.0)
    comb_ref[...] = comb


def _norm(x, g, sc, sh, router=None):
    s, d = x.shape
    tm = min(256, s)
    row = pl.BlockSpec((tm, d), lambda i: (i, 0))
    vec = pl.BlockSpec((1, d), lambda i: (0, 0))
    if router is None:
        return pl.pallas_call(
            _norm_kernel, grid=(s // tm,),
            in_specs=[row, vec, vec, vec], out_specs=row,
            out_shape=jax.ShapeDtypeStruct((s, d), BF16),
            compiler_params=_cparams(("parallel",), 32), name="norm",
        )(x, g, sc, sh)
    rw, rb = router
    rw_hi = rw.astype(BF16)
    rw_lo = (rw - rw_hi.astype(F32)).astype(BF16)
    rw_spec = pl.BlockSpec((d, LANES), lambda i: (0, 0))
    return pl.pallas_call(
        _norm_router_kernel, grid=(s // tm,),
        in_specs=[row, vec, vec, vec, rw_spec, rw_spec,
                  pl.BlockSpec((1, LANES), lambda i: (0, 0))],
        out_specs=[row, pl.BlockSpec((tm, LANES), lambda i: (i, 0))],
        out_shape=[jax.ShapeDtypeStruct((s, d), BF16), jax.ShapeDtypeStruct((s, LANES), F32)],
        compiler_params=_cparams(("parallel",), 40), name="norm_router",
    )(x, g, sc, sh, rw_hi, rw_lo, rb)


IN_TN = 512
IN_MAIN_TILES = OFF_QLAT // IN_TN


def _in_proj_kernel(a_ref, w_ref, wt_ref, o_ref):
    j = pl.program_id(1)

    @pl.when(j < IN_MAIN_TILES)
    def _():
        o_ref[...] = _dot(a_ref[...], w_ref[...]).astype(o_ref.dtype)

    @pl.when(j >= IN_MAIN_TILES)
    def _():
        o_ref[...] = _dot(a_ref[...], wt_ref[...]).astype(o_ref.dtype)


def _in_proj(a, w_main, w_tail):
    m, k = a.shape
    tm = min(1024, m)
    n_tail = w_tail.shape[1] // IN_TN
    last = IN_MAIN_TILES - 1
    return pl.pallas_call(
        _in_proj_kernel, grid=(m // tm, IN_MAIN_TILES + n_tail),
        in_specs=[pl.BlockSpec((tm, k), lambda i, j: (i, 0)),
                  pl.BlockSpec((k, IN_TN), lambda i, j: (0, jnp.minimum(j, last))),
                  pl.BlockSpec((k, IN_TN), lambda i, j: (0, jnp.maximum(j - IN_MAIN_TILES, 0)))],
        out_specs=pl.BlockSpec((tm, IN_TN), lambda i, j: (i, j)),
        out_shape=jax.ShapeDtypeStruct((m, N_P), BF16),
        compiler_params=_cparams(("parallel", "arbitrary"), 48), name="in_proj",
    )(a, w_main, w_tail)


def _prep_kernel(p_ref, gains_ref, gql_ref, gkvl_ref, fb_ref, cb_ref, sb_ref,
                 qa_ref, ka_ref, va_ref, qb_ref, kb_ref, vb_ref,
                 qc_ref, kc_ref, vc_ref, ql_ref, kvl_ref, kr_ref, carry_ref):
    tm = p_ref.shape[0]
    qscale = HEAD_DIM ** -0.5 * LOG2E

    @pl.when(pl.program_id(0) == 0)
    def _():
        carry_ref[...] = jnp.zeros_like(carry_ref)

    def head_norm(off, g_row, scale):
        x = p_ref[:, off:off + HEAD_DIM].astype(F32)
        ms = jnp.mean(x * x, axis=-1, keepdims=True)
        return x * lax.rsqrt(ms + EPS) * (gains_ref[g_row:g_row + 1, :] * scale)

    cb, sb = cb_ref[...], sb_ref[...]

    def rope(y):
        return y * cb + pltpu.roll(y, HEAD_DIM // 2, 1) * sb

    for h in range(A_HEADS):
        c = h * HEAD_DIM
        qa_ref[:, c:c + HEAD_DIM] = head_norm(OFF_QA + c, 0, qscale).astype(BF16)
        ka_ref[:, c:c + HEAD_DIM] = head_norm(OFF_KA + c, 1, 1.0).astype(BF16)
    va_ref[...] = p_ref[:, OFF_VA:OFF_VA + A_HEADS * HEAD_DIM]

    for h in range(B_Q_HEADS):
        c = h * HEAD_DIM
        qb_ref[:, c:c + HEAD_DIM] = rope(head_norm(OFF_QB + c, 2, qscale)).astype(BF16)
    for h in range(B_KV_HEADS):
        c = h * HEAD_DIM
        kb_ref[:, c:c + HEAD_DIM] = rope(head_norm(OFF_KB + c, 3, 1.0)).astype(BF16)
    vb_ref[...] = p_ref[:, OFF_VB:OFF_VB + B_KV_HEADS * HEAD_DIM]

    z = p_ref[:, OFF_F:OFF_F + LANES].astype(F32) + fb_ref[...]
    logf = jnp.minimum(z, 0.0) - jnp.log1p(jnp.exp(-jnp.abs(z)))
    r_i = lax.broadcasted_iota(jnp.int32, (tm, tm), 0)
    c_i = lax.broadcasted_iota(jnp.int32, (tm, tm), 1)
    tri = jnp.where(r_i >= c_i, 1.0, 0.0).astype(F32)
    cum = jnp.dot(tri, logf, precision=lax.Precision.HIGHEST,
                  preferred_element_type=F32) + carry_ref[...]
    carry_ref[...] = cum[tm - 1:tm, :]
    cum2 = cum * LOG2E
    hi = cum2.astype(BF16).astype(F32)
    r1 = cum2 - hi
    mid = r1.astype(BF16).astype(F32)
    lo = r1 - mid
    lane = lax.broadcasted_iota(jnp.int32, (tm, LANES), 1)
    for h in range(C_HEADS):
        c = h * HEAD_DIM
        a = h * AUG
        hi_h, mid_h, lo_h = hi[:, h:h + 1], mid[:, h:h + 1], lo[:, h:h + 1]
        aug_q = jnp.where(lane == 0, hi_h, jnp.where(lane == 1, mid_h, jnp.where(
            lane == 2, lo_h, jnp.where(lane < 6, 1.0, 0.0))))
        aug_k = jnp.where(lane < 3, 1.0, jnp.where(lane == 3, -hi_h, jnp.where(
            lane == 4, -mid_h, jnp.where(lane == 5, -lo_h, 0.0))))
        qc_ref[:, a:a + HEAD_DIM] = head_norm(OFF_QC + c, 4, qscale).astype(BF16)
        qc_ref[:, a + HEAD_DIM:a + AUG] = aug_q.astype(BF16)
        kc_ref[h, :HEAD_DIM, :] = head_norm(OFF_KC + c, 5, 1.0).T.astype(BF16)
        kc_ref[h, HEAD_DIM:, :] = aug_k.T.astype(BF16)
    vc_ref[...] = p_ref[:, OFF_VC:OFF_VC + C_HEADS * HEAD_DIM]

    xq = p_ref[:, OFF_QLAT:OFF_QLAT + D_Q_LORA].astype(F32)
    ms = jnp.mean(xq * xq, axis=-1, keepdims=True)
    ql_ref[...] = (xq * lax.rsqrt(ms + EPS) * gql_ref[...]).astype(BF16)
    xkv = p_ref[:, OFF_KVLAT:OFF_KVLAT + D_KV_LORA].astype(F32)
    ms = jnp.mean(xkv * xkv, axis=-1, keepdims=True)
    kvl_ref[...] = (xkv * lax.rsqrt(ms + EPS) * gkvl_ref[...]).astype(BF16)
    kr_ref[...] = p_ref[:, OFF_KROPE:OFF_KROPE + LANES].astype(F32)


def _kt_shape_spec(n_heads, s, tm):
    blk = min(FLASH_BLK, s)
    per = blk // tm
    shape = jax.ShapeDtypeStruct((n_heads, s // blk, AUG, blk), BF16)
    spec = pl.BlockSpec((n_heads, None, AUG, tm), lambda i: (0, i // per, 0, i % per))
    return shape, spec


def _prep(p, gains, gql, gkvl, fb, cos_b, sin_b):
    s = p.shape[0]
    tm = min(PREP_ROWS, s)

    def row(w):
        return pl.BlockSpec((tm, w), lambda i: (i, 0))

    def const(r, w):
        return pl.BlockSpec((r, w), lambda i: (0, 0))

    widths = [1024, 1024, 1024, 1024, 256, 256, C_HEADS * AUG, None, 1024,
              D_Q_LORA, D_KV_LORA]
    out_shape = [jax.ShapeDtypeStruct((s, w), BF16) for w in widths if w]
    out_specs = [row(w) for w in widths if w]
    kt_shape, kt_spec = _kt_shape_spec(C_HEADS, s, tm)
    out_shape.insert(7, kt_shape)
    out_specs.insert(7, kt_spec)
    out_shape.append(jax.ShapeDtypeStruct((s, LANES), F32))
    out_specs.append(row(LANES))
    return pl.pallas_call(
        _prep_kernel, grid=(s // tm,),
        in_specs=[row(N_P), const(8, LANES), const(1, D_Q_LORA), const(1, D_KV_LORA),
                  const(1, LANES), row(LANES), row(LANES)],
        out_specs=out_specs, out_shape=out_shape,
        scratch_shapes=[pltpu.VMEM((1, LANES), F32)],
        compiler_params=_cparams(("arbitrary",), 48), name="prep",
    )(p, gains, gql, gkvl, fb, cos_b, sin_b)


def _rope_d(y, cd, sd):
    lane = lax.broadcasted_iota(jnp.int32, y.shape, 1)
    half = D_ROPE // 2
    partner = jnp.where(lane < half, pltpu.roll(y, LANES - half, 1), pltpu.roll(y, half, 1))
    return y * cd + partner * sd


def _prep_d_kernel(ql_ref, kvl_ref, kr_ref, wq_ref, wkv_ref, gq_ref, gk_ref, cd_ref, sd_ref,
                   qd_ref, kd_ref, vd_ref):
    qscale = D_QK ** -0.5 * LOG2E
    cd, sd = cd_ref[...], sd_ref[...]
    q = _dot(ql_ref[...], wq_ref[...])
    kv = _dot(kvl_ref[...], wkv_ref[...])
    kr = kr_ref[...]
    ss_r = jnp.sum(kr * kr, axis=-1, keepdims=True)
    gq = gq_ref[...]
    gk = gk_ref[...]
    for h in range(D_HEADS):
        a = h * AUG
        qn, qr = q[:, a:a + HEAD_DIM], q[:, a + HEAD_DIM:a + AUG]
        ms = (jnp.sum(qn * qn, axis=-1, keepdims=True)
              + jnp.sum(qr * qr, axis=-1, keepdims=True)) * (1.0 / D_QK)
        r = lax.rsqrt(ms + EPS) * qscale
        qd_ref[:, a:a + HEAD_DIM] = (qn * r * gq[:, :HEAD_DIM]).astype(BF16)
        qd_ref[:, a + HEAD_DIM:a + AUG] = _rope_d(qr * r * gq[:, HEAD_DIM:], cd, sd).astype(BF16)
        kn = kv[:, h * D_NOPE:(h + 1) * D_NOPE]
        ms = (jnp.sum(kn * kn, axis=-1, keepdims=True) + ss_r) * (1.0 / D_QK)
        r = lax.rsqrt(ms + EPS)
        kd_ref[h, :HEAD_DIM, :] = (kn * r * gk[:, :HEAD_DIM]).T.astype(BF16)
        kd_ref[h, HEAD_DIM:, :] = _rope_d(kr * r * gk[:, HEAD_DIM:], cd, sd).T.astype(BF16)
    vd_ref[...] = kv[:, D_HEADS * D_NOPE:].astype(BF16)


def _prep_d(ql, kvl, kr, wq, wkv, gq, gk, cos_d, sin_d):
    s = ql.shape[0]
    tm = min(PREP_ROWS, s)

    def row(w):
        return pl.BlockSpec((tm, w), lambda i: (i, 0))

    def const(r, w):
        return pl.BlockSpec((r, w), lambda i: (0, 0))

    kt_shape, kt_spec = _kt_shape_spec(D_HEADS, s, tm)
    return pl.pallas_call(
        _prep_d_kernel, grid=(s // tm,),
        in_specs=[row(D_Q_LORA), row(D_KV_LORA), row(LANES),
                  const(D_Q_LORA, D_HEADS * AUG), const(D_KV_LORA, 2 * D_HEADS * D_NOPE),
                  const(1, AUG), const(1, AUG), row(LANES), row(LANES)],
        out_specs=[row(D_HEADS * AUG), kt_spec, row(D_HEADS * D_V)],
        out_shape=[jax.ShapeDtypeStruct((s, D_HEADS * AUG), BF16), kt_shape,
                   jax.ShapeDtypeStruct((s, D_HEADS * D_V), BF16)],
        compiler_params=_cparams(("parallel",), 40), name="prep_d",
    )(ql, kvl, kr, wq, wkv, gq, gk, cos_d, sin_d)


def _flash_kernel(q_ref, kt_ref, v_ref, o_ref, m_ref, l_ref, acc_ref, *, blk, unit):
    i = pl.program_id(1)
    m_ref[...] = jnp.full_like(m_ref, NEG_INF)
    l_ref[...] = jnp.zeros_like(l_ref)
    acc_ref[...] = jnp.zeros_like(acc_ref)
    nct = blk // LANES
    rc = FLASH_ROW_CHUNK

    def step(j, diagonal):
        start = pl.multiple_of(j * blk, blk)
        for g in range(FLASH_HEADS):
            s = _dot(q_ref[:, g * AUG:(g + 1) * AUG], kt_ref[g, j])
            v = v_ref[pl.ds(start, blk), g * LANES:(g + 1) * LANES]
            ps, alphas = [], []
            for r in range(blk // rc):
                rows = slice(r * rc, (r + 1) * rc)
                sc = s[rows, :]
                if diagonal:
                    qi = lax.broadcasted_iota(jnp.int32, sc.shape, 0) + r * rc
                    ki = lax.broadcasted_iota(jnp.int32, sc.shape, 1)
                    if unit > 1:
                        shift = unit.bit_length() - 1
                        qi, ki = qi >> shift, ki >> shift
                    sc = jnp.where(ki <= qi, sc, NEG_INF)
                m_prev = m_ref[g, rows, :]
                m_new = jnp.maximum(m_prev, jnp.max(sc, axis=-1, keepdims=True))
                alpha = jnp.exp2(m_prev - m_new)
                pc = [jnp.exp2(sc[:, c * LANES:(c + 1) * LANES] - m_new) for c in range(nct)]
                lsum = pc[0]
                for c in range(1, nct):
                    lsum = lsum + pc[c]
                l_ref[g, rows, :] = alpha * l_ref[g, rows, :] + lsum
                m_ref[g, rows, :] = m_new
                ps.append(jnp.concatenate(pc, axis=1).astype(BF16))
                alphas.append(alpha)
            p = jnp.concatenate(ps, axis=0)
            alpha = jnp.concatenate(alphas, axis=0)
            acc_ref[g] = alpha * acc_ref[g] + _dot(p, v)

    def body(j, carry):
        step(j, False)
        return carry

    lax.fori_loop(0, i, body, 0)
    step(i, True)
    for g in range(FLASH_HEADS):
        l = jnp.sum(l_ref[g], axis=-1, keepdims=True)
        o_ref[:, g * LANES:(g + 1) * LANES] = (acc_ref[g] / l).astype(o_ref.dtype)


def _flash(q, kt, v, n_heads, unit):
    s = q.shape[0]
    blk = kt.shape[3]
    dv = v.shape[1] // n_heads
    g = FLASH_HEADS
    assert dv == LANES and n_heads % g == 0
    once = pl.Buffered(1)
    return pl.pallas_call(
        functools.partial(_flash_kernel, blk=blk, unit=unit),
        grid=(n_heads // g, s // blk),
        in_specs=[pl.BlockSpec((blk, g * AUG), lambda h, i: (i, h)),
                  pl.BlockSpec((g, s // blk, AUG, blk), lambda h, i: (h, 0, 0, 0), pipeline_mode=once),
                  pl.BlockSpec((s, g * dv), lambda h, i: (0, h), pipeline_mode=once)],
        out_specs=pl.BlockSpec((blk, g * dv), lambda h, i: (i, h)),
        out_shape=jax.ShapeDtypeStruct((s, n_heads * dv), BF16),
        scratch_shapes=[pltpu.VMEM((g, blk, LANES), F32), pltpu.VMEM((g, blk, LANES), F32),
                        pltpu.VMEM((g, blk, dv), F32)],
        compiler_params=_cparams(("parallel", "arbitrary"), 56), name=f"flash_u{unit}",
    )(q, kt, v)


A_PREV_BLOCKS = A_PREV_CHUNKS * CHUNK // BAND_Q


def _attn_a_kernel(q_ref, k0_ref, k1_ref, k2_ref, v0_ref, v1_ref, v2_ref, t_ref, mk_ref, o_ref):
    mask = mk_ref[...]
    for h in range(A_HEADS):
        cols = slice(h * HEAD_DIM, (h + 1) * HEAD_DIM)
        k = jnp.concatenate([k0_ref[:, cols], k1_ref[:, cols], k2_ref[:, cols]], axis=0)
        v = jnp.concatenate([v0_ref[:, cols], v1_ref[:, cols], v2_ref[:, cols]], axis=0)
        s = _dot_nt(q_ref[:, cols], k) + t_ref[h] + mask
        m = jnp.max(s, axis=-1, keepdims=True)
        p = jnp.exp2(s - m)
        l = jnp.sum(p, axis=-1, keepdims=True)
        o_ref[:, cols] = (_dot(p.astype(BF16), v) / l).astype(o_ref.dtype)


def _attn_a(q, k, v, table, mask):
    s, w = q.shape
    row = pl.BlockSpec((BAND_Q, w), lambda i: (i, 0))
    prev = [pl.BlockSpec((BAND_Q, w), functools.partial(
        lambda i, back: (jnp.maximum(i - back, 0), 0), back=back))
        for back in range(A_PREV_BLOCKS, -1, -1)]
    return pl.pallas_call(
        _attn_a_kernel, grid=(s // BAND_Q,),
        in_specs=[row] + prev + prev + [
            pl.BlockSpec((A_HEADS, BAND_Q, A_WIN), lambda i: (0, 0, 0)),
            pl.BlockSpec((None, BAND_Q, A_WIN), lambda i: (jnp.minimum(i, A_PREV_BLOCKS), 0, 0))],
        out_specs=row,
        out_shape=jax.ShapeDtypeStruct((s, w), BF16),
        compiler_params=_cparams(("arbitrary",), 40), name="attn_a",
    )(q, k, k, k, v, v, v, table, mask)


def _a_mask():
    b = np.arange(A_PREV_BLOCKS + 1)[:, None, None]
    q = np.arange(BAND_Q)[None, :, None]
    w = np.arange(A_WIN)[None, None, :]
    dchunk = (q + A_PREV_BLOCKS * BAND_Q) // CHUNK - w // CHUNK
    valid = (dchunk >= 0) & (dchunk <= A_PREV_CHUNKS) & (w >= (A_PREV_BLOCKS - b) * BAND_Q)
    return jnp.asarray(np.where(valid, 0.0, NEG_INF), F32)


def _a_table(rel_bias):
    n, m = BAND_Q, A_WIN
    u = np.arange(n + m)
    dist = A_PREV_BLOCKS * BAND_Q + (n - 1) - u
    idx = np.clip(dist, -REL_CLIP, REL_CLIP) + REL_CLIP
    z = rel_bias.astype(F32).T[:, jnp.asarray(idx)] * LOG2E
    skew = jnp.broadcast_to(z[:, None, :], (A_HEADS, n, n + m))
    skew = skew.reshape(A_HEADS, n * (n + m))[:, :n * (n + m - 1)]
    return skew.reshape(A_HEADS, n, n + m - 1)[:, :, n - 1:n - 1 + m]


def _attn_b_kernel(sink_ref, q_ref, k_ref, v_ref, o_ref):
    g = pl.program_id(0)
    i = pl.program_id(1)
    group = B_Q_HEADS // B_KV_HEADS
    start = pl.multiple_of(jnp.maximum(i * BAND_Q - B_PREV_CHUNKS * CHUNK, 0), LANES)
    k = k_ref[pl.ds(start, B_WIN), :]
    v = v_ref[pl.ds(start, B_WIN), :]
    shift = CHUNK.bit_length() - 1
    t = (i * BAND_Q + lax.broadcasted_iota(jnp.int32, (BAND_Q, B_WIN), 0)) >> shift
    kp = (start + lax.broadcasted_iota(jnp.int32, (BAND_Q, B_WIN), 1)) >> shift
    valid = jnp.abs(t - kp - 1) <= 1
    for hi in range(group):
        c = hi * HEAD_DIM
        s = jnp.where(valid, _dot_nt(q_ref[:, c:c + HEAD_DIM], k), NEG_INF)
        sink = sink_ref[g * group + hi] * LOG2E
        m = jnp.maximum(jnp.max(s, axis=-1, keepdims=True), sink)
        p = jnp.exp2(s - m)
        l = jnp.sum(p, axis=-1, keepdims=True) + jnp.exp2(sink - m)
        o_ref[:, c:c + HEAD_DIM] = (_dot(p.astype(BF16), v) / l).astype(o_ref.dtype)


def _attn_b(q, k, v, sinks):
    s = q.shape[0]
    gw = (B_Q_HEADS // B_KV_HEADS) * HEAD_DIM
    return pl.pallas_call(
        _attn_b_kernel, grid=(B_KV_HEADS, s // BAND_Q),
        in_specs=[pl.BlockSpec(memory_space=pltpu.SMEM),
                  pl.BlockSpec((BAND_Q, gw), lambda g, i: (i, g)),
                  pl.BlockSpec((s, HEAD_DIM), lambda g, i: (0, g)),
                  pl.BlockSpec((s, HEAD_DIM), lambda g, i: (0, g))],
        out_specs=pl.BlockSpec((BAND_Q, gw), lambda g, i: (i, g)),
        out_shape=jax.ShapeDtypeStruct((s, B_Q_HEADS * HEAD_DIM), BF16),
        compiler_params=_cparams(("parallel", "arbitrary"), 40), name="attn_b",
    )(sinks, q, k, v)


def _merge_kernel(h_ref, oa_ref, ob_ref, oc_ref, od_ref, wg_ref, bg_ref, wb_ref, out_ref, acc_ref):
    b = pl.program_id(2)

    for n, o_ref in enumerate((oa_ref, ob_ref, oc_ref, od_ref)):
        @pl.when(b == n)
        def _(o_ref=o_ref, n=n):
            gate = jax.nn.sigmoid(_dot(h_ref[...], wg_ref[...]) + bg_ref[...])
            val = gate * _dot(o_ref[...], wb_ref[...].astype(BF16))
            if n == 0:
                acc_ref[...] = val
            elif n < N_BRANCH - 1:
                acc_ref[...] += val
            else:
                out_ref[...] = (acc_ref[...] + val).astype(out_ref.dtype)


def _merge(h, outs, wg, bg, wb, l):
    s, d = h.shape
    tm, tn = min(1024, s), min(512, d)
    o_spec = pl.BlockSpec((tm, BRANCH_W), lambda i, j, b: (i, 0), pipeline_mode=pl.Buffered(1))
    return pl.pallas_call(
        _merge_kernel, grid=(s // tm, d // tn, N_BRANCH),
        in_specs=[pl.BlockSpec((tm, d), lambda i, j, b: (i, 0)),
                  o_spec, o_spec, o_spec, o_spec,
                  pl.BlockSpec((None, None, d, tn), lambda i, j, b: (l, b, 0, j)),
                  pl.BlockSpec((None, 1, tn), lambda i, j, b: (b, 0, j)),
                  pl.BlockSpec((None, None, BRANCH_W, tn), lambda i, j, b: (l, b, 0, j))],
        out_specs=pl.BlockSpec((tm, tn), lambda i, j, b: (i, j)),
        out_shape=jax.ShapeDtypeStruct((s, d), BF16),
        scratch_shapes=[pltpu.VMEM((tm, tn), F32)],
        compiler_params=_cparams(("parallel", "parallel", "arbitrary"), 56), name="merge",
    )(h, *outs, wg, bg, wb)


def _resid_kernel(a_ref, w_ref, x_ref, g_ref, o_ref):
    o_ref[...] = x_ref[...] + g_ref[...] * _dot(a_ref[...], w_ref[...].astype(BF16))


def _resid_moe_kernel(a_ref, w_ref, c_ref, bd_ref, x_ref, g_ref, o_ref):
    y = _dot(a_ref[...], w_ref[...].astype(BF16)) + _dot(c_ref[...].astype(BF16), bd_ref[...])
    o_ref[...] = x_ref[...] + g_ref[...] * y


def _resid(a, w, l, x, gate, comb=None, b_dn=None):
    s, k = a.shape
    d = w.shape[2]
    tm, tn = min(1024, s), min(512, d)
    a_spec = pl.BlockSpec((tm, k), lambda i, j: (i, 0))
    w_spec = pl.BlockSpec((None, k, tn), lambda i, j: (l, 0, j))
    x_spec = pl.BlockSpec((tm, tn), lambda i, j: (i, j))
    g_spec = pl.BlockSpec((1, tn), lambda i, j: (0, j))
    if comb is None:
        kern, ins = _resid_kernel, (a, w, x, gate)
        in_specs = [a_spec, w_spec, x_spec, g_spec]
    else:
        kern, ins = _resid_moe_kernel, (a, w, comb, b_dn, x, gate)
        in_specs = [a_spec, w_spec, pl.BlockSpec((tm, LANES), lambda i, j: (i, 0)),
                    pl.BlockSpec((LANES, tn), lambda i, j: (0, j)), x_spec, g_spec]
    return pl.pallas_call(
        kern, grid=(s // tm, d // tn), in_specs=in_specs, out_specs=x_spec,
        out_shape=jax.ShapeDtypeStruct((s, d), F32),
        compiler_params=_cparams(("parallel", "parallel"), 48), name="resid",
    )(*ins)


def _moe_up_kernel(h_ref, w_ref, b_ref, c_ref, o_ref, *, eb):
    j = pl.program_id(1)
    h = h_ref[...]
    comb = c_ref[...]
    lane = lax.broadcasted_iota(jnp.int32, comb.shape, 1)
    for e in range(eb):
        gu = _dot(h, w_ref[e].astype(BF16)) + b_ref[e]
        glu = jnp.minimum(gu[:, :D_EXPERT], SWIGLU_LIMIT)
        lin = jnp.clip(gu[:, D_EXPERT:], -SWIGLU_LIMIT, SWIGLU_LIMIT)
        act = glu * jax.nn.sigmoid(SWIGLU_ALPHA * glu) * (lin + 1.0)
        ce = jnp.sum(jnp.where(lane == j * eb + e, comb, 0.0), axis=-1, keepdims=True)
        o_ref[:, e * D_EXPERT:(e + 1) * D_EXPERT] = (act * ce).astype(o_ref.dtype)


def _moe_up(h, w_gu, l, b_gu, comb):
    s, d = h.shape
    tm, eb = min(1024, s), 2
    return pl.pallas_call(
        functools.partial(_moe_up_kernel, eb=eb), grid=(s // tm, N_EXPERTS // eb),
        in_specs=[pl.BlockSpec((tm, d), lambda i, j: (i, 0)),
                  pl.BlockSpec((None, eb, d, 2 * D_EXPERT), lambda i, j: (l, j, 0, 0)),
                  pl.BlockSpec((eb, 1, 2 * D_EXPERT), lambda i, j: (j, 0, 0)),
                  pl.BlockSpec((tm, LANES), lambda i, j: (i, 0))],
        out_specs=pl.BlockSpec((tm, eb * D_EXPERT), lambda i, j: (i, j)),
        out_shape=jax.ShapeDtypeStruct((s, N_EXPERTS * D_EXPERT), BF16),
        compiler_params=_cparams(("parallel", "parallel"), 48), name="moe_up",
    )(h, w_gu, b_gu, comb)


def _w_in_tail(w):
    d = w.shape[0]
    zeros = lambda n: jnp.zeros((d, n), BF16)
    return jnp.concatenate(
        [w[:, C_HEADS:].astype(BF16), zeros(LANES - D_ROPE), w[:, :C_HEADS].astype(BF16),
         zeros(N_P - OFF_F - C_HEADS)], axis=1)


def _reorder_w_q_b(w):
    r = w.shape[0]
    w = w.reshape(r, D_HEADS, D_QK)
    w = jnp.pad(w, ((0, 0), (0, 0), (0, AUG - D_QK)))
    return w.reshape(r, D_HEADS * AUG)


def _reorder_w_kv_b(w):
    r = w.shape[0]
    w = w.reshape(r, D_HEADS, D_NOPE + D_V)
    return jnp.concatenate([w[:, :, :D_NOPE].reshape(r, -1), w[:, :, D_NOPE:].reshape(r, -1)], axis=1)


def _pad_lanes(v, width, value=0.0):
    return jnp.pad(v, (0, width - v.shape[0]), constant_values=value).reshape(1, width)


def kernel(x, c, positions, ada_w, ada_b, ada_layer, norm1_g, norm2_g, w_in, a_q_norm, a_k_norm, a_rel_bias, b_q_norm, b_k_norm, b_sinks, c_q_norm, c_k_norm, c_f_bias, d_q_a_norm, d_w_q_b, d_kv_a_norm, d_w_kv_b, d_q_norm, d_k_norm, w_branch, w_gate, b_gate, w_out, router_w, router_b, w_gu, b_gu, w_dn, b_dn):
    batch, s, d = x.shape
    assert batch == 1, "kernels are written for a single sequence"
    depth = w_in.shape[0]
    xs = x.reshape(s, d)

    base_mod = _adaln(c.reshape(d, 1), ada_w, ada_b).reshape(6, d)
    cos_b, sin_b, cos_d, sin_d = _rope_tables(positions.reshape(s, 1).astype(F32))
    a_mask = _a_mask()
    w_gate_bf16 = w_gate.astype(BF16)

    for l in range(depth):
        mod = base_mod + ada_layer[l]
        shift1, scale1, gate1, shift2, scale2, gate2 = (mod[j:j + 1] for j in range(6))

        h = _norm(xs, norm1_g[l].reshape(1, d), scale1, shift1)
        p = _in_proj(h, w_in[l, :, :OFF_QLAT].astype(BF16), _w_in_tail(w_in[l, :, OFF_QLAT:]))
        gains = jnp.stack([a_q_norm[l], a_k_norm[l], b_q_norm[l], b_k_norm[l],
                           c_q_norm[l], c_k_norm[l], jnp.zeros_like(a_q_norm[l]),
                           jnp.zeros_like(a_q_norm[l])])
        (qa, ka, va, qb, kb, vb, qc, kc, vc, ql, kvl, kr) = _prep(
            p, gains, d_q_a_norm[l].reshape(1, -1), d_kv_a_norm[l].reshape(1, -1),
            _pad_lanes(c_f_bias[l], LANES), cos_b, sin_b)
        qd, kd, vd = _prep_d(
            ql, kvl, kr, _reorder_w_q_b(d_w_q_b[l]).astype(BF16),
            _reorder_w_kv_b(d_w_kv_b[l]).astype(BF16),
            _pad_lanes(d_q_norm[l], AUG), _pad_lanes(d_k_norm[l], AUG), cos_d, sin_d)

        o_a = _attn_a(qa, ka, va, _a_table(a_rel_bias[l]), a_mask)
        o_b = _attn_b(qb, kb, vb, b_sinks[l].astype(F32))
        o_c = _flash(qc, kc, vc, C_HEADS, 1)
        o_d = _flash(qd, kd, vd, D_HEADS, CHUNK)

        merged = _merge(h, (o_a, o_b, o_c, o_d), w_gate_bf16,
                        b_gate[l].reshape(N_BRANCH, 1, d), w_branch, l)
        xs = _resid(merged, w_out, l, xs, gate1)

        rw = jnp.pad(router_w[l], ((0, 0), (0, LANES - N_EXPERTS)))
        rb = _pad_lanes(router_b[l].astype(F32), LANES, NEG_INF)
        h2, comb = _norm(xs, norm2_g[l].reshape(1, d), scale2, shift2, router=(rw, rb))
        act = _moe_up(h2, w_gu, l, b_gu[l].reshape(N_EXPERTS, 1, 2 * D_EXPERT), comb)
        bdn = jnp.pad(b_dn[l], ((0, LANES - N_EXPERTS), (0, 0))).astype(BF16)
        xs = _resid(act, w_dn.reshape(depth, N_EXPERTS * D_EXPERT, d), l, xs, gate2,
                    comb=comb, b_dn=bdn)
    return xs.reshape(batch, s, d)
```

```python
import functools
import math

import numpy as np
import jax
import jax.numpy as jnp
from jax import lax
from jax.experimental import pallas as pl
from jax.experimental.pallas import tpu as pltpu

F32 = jnp.float32
BF16 = jnp.bfloat16

CHUNK = 64
HEAD_DIM = 128
N_BRANCH = 4
BRANCH_W = 1024
ROPE_THETA = 10000.0
EPS = 1e-6
NEG_INF = -1e30
A_HEADS = 8
A_PREV_CHUNKS = 8
REL_CLIP = 256
B_Q_HEADS = 8
B_KV_HEADS = 2
B_PREV_CHUNKS = 2
C_HEADS = 8
D_HEADS = 8
D_Q_LORA = 896
D_KV_LORA = 256
D_NOPE = 128
D_ROPE = 64
D_V = 128
D_QK = D_NOPE + D_ROPE
N_EXPERTS = 32
TOP_K = 4
D_EXPERT = 128
SWIGLU_LIMIT = 7.0
SWIGLU_ALPHA = 1.702

LANES = 128
V7X_VMEM_BYTES = 64 * 1024 * 1024

LOG2E = math.log2(math.e)

OFF_QA, OFF_KA, OFF_VA = 0, 1024, 2048
OFF_QB, OFF_KB, OFF_VB = 3072, 4096, 4352
OFF_QC, OFF_KC, OFF_VC = 4608, 5632, 6656
OFF_QLAT = 7680
OFF_KVLAT = OFF_QLAT + D_Q_LORA
OFF_KROPE = OFF_KVLAT + D_KV_LORA
OFF_F = OFF_KROPE + LANES
N_P = 9216

AUG = 2 * HEAD_DIM
FLASH_BLK = 1024
FLASH_ROW_CHUNK = 32
FLASH_HEADS = 2
PREP_ROWS = 256
BAND_Q = 256
A_WIN = BAND_Q + A_PREV_CHUNKS * CHUNK
B_WIN = BAND_Q + B_PREV_CHUNKS * CHUNK


def _cparams(sems, vmem_mb):
    return pltpu.CompilerParams(dimension_semantics=sems,
                                vmem_limit_bytes=vmem_mb * 1024 * 1024)


def _dot(a, b):
    return jnp.dot(a, b, preferred_element_type=F32)


def _dot_nt(a, b):
    return lax.dot_general(a, b, (((1,), (1,)), ((), ())), preferred_element_type=F32)


def _adaln_kernel(c_ref, w_ref, b_ref, o_ref):
    c = c_ref[...]
    s = c * jax.nn.sigmoid(c)
    o_ref[...] = jnp.sum(w_ref[...] * s, axis=0, keepdims=True) + b_ref[...]


def _adaln(c_col, ada_w, ada_b):
    d, n = ada_w.shape
    tn = min(512, n)
    return pl.pallas_call(
        _adaln_kernel,
        grid=(n // tn,),
        in_specs=[pl.BlockSpec((d, 1), lambda j: (0, 0)),
                  pl.BlockSpec((d, tn), lambda j: (0, j)),
                  pl.BlockSpec((1, tn), lambda j: (0, j))],
        out_specs=pl.BlockSpec((1, tn), lambda j: (0, j)),
        out_shape=jax.ShapeDtypeStruct((1, n), F32),
        compiler_params=_cparams(("parallel",), 40),
        name="adaln",
    )(c_col, ada_w, ada_b.reshape(1, n))


def _rope_table_kernel(pos_ref, fb_ref, fd_ref, cb_ref, sb_ref, cd_ref, sd_ref):
    pos = pos_ref[...]
    lane = lax.broadcasted_iota(jnp.int32, (1, LANES), 1)
    ab = pos * fb_ref[...]
    sb = jnp.sin(ab)
    cb_ref[...] = jnp.cos(ab)
    sb_ref[...] = jnp.where(lane < HEAD_DIM // 2, -sb, sb)
    ad = pos * fd_ref[...]
    sd = jnp.sin(ad)
    cd_ref[...] = jnp.where(lane < D_ROPE, jnp.cos(ad), 1.0)
    sd_ref[...] = jnp.where(lane < D_ROPE // 2, -sd, jnp.where(lane < D_ROPE, sd, 0.0))


def _rope_tables(pos_col):
    s = pos_col.shape[0]
    tm = min(512, s)
    lane = np.arange(LANES)
    half_b = HEAD_DIM // 2
    fb = ROPE_THETA ** (-(lane % half_b).astype(np.float64) / half_b)
    half_d = D_ROPE // 2
    fd = np.where(lane < D_ROPE, ROPE_THETA ** (-(lane % half_d).astype(np.float64) / half_d), 0.0)
    fb = jnp.asarray(fb, F32).reshape(1, LANES)
    fd = jnp.asarray(fd, F32).reshape(1, LANES)
    tab = jax.ShapeDtypeStruct((s, LANES), F32)
    row = pl.BlockSpec((tm, LANES), lambda i: (i, 0))
    const = pl.BlockSpec((1, LANES), lambda i: (0, 0))
    return pl.pallas_call(
        _rope_table_kernel,
        grid=(s // tm,),
        in_specs=[pl.BlockSpec((tm, 1), lambda i: (i, 0)), const, const],
        out_specs=[row, row, row, row],
        out_shape=[tab, tab, tab, tab],
        compiler_params=_cparams(("parallel",), 32),
        name="rope_tables",
    )(pos_col, fb, fd)


def _mod_norm(x, g, sc, sh):
    ms = jnp.mean(x * x, axis=-1, keepdims=True)
    y = x * lax.rsqrt(ms + EPS)
    return (y * g) * (1.0 + sc) + sh


def _norm_kernel(x_ref, g_ref, sc_ref, sh_ref, h_ref):
    h_ref[...] = _mod_norm(x_ref[...], g_ref[...], sc_ref[...], sh_ref[...]).astype(h_ref.dtype)


def _norm_router_kernel(x_ref, g_ref, sc_ref, sh_ref, rwh_ref, rwl_ref, rb_ref, h_ref, comb_ref):
    h = _mod_norm(x_ref[...], g_ref[...], sc_ref[...], sh_ref[...])
    h_hi = h.astype(BF16)
    h_ref[...] = h_hi
    h_lo = (h - h_hi.astype(F32)).astype(BF16)
    rw_hi = rwh_ref[...]
    logits = (_dot(h_hi, rw_hi) + _dot(h_lo, rw_hi) + _dot(h_hi, rwl_ref[...])
              + rb_ref[...])
    lane = lax.broadcasted_iota(jnp.int32, logits.shape, 1).astype(F32)
    work = logits
    vals, sels = [], []
    for _ in range(TOP_K):
        m = jnp.max(work, axis=-1, keepdims=True)
        idx = jnp.min(jnp.where(work == m, lane, float(LANES)), axis=-1, keepdims=True)
        sel = lane == idx
        vals.append(m)
        sels.append(sel)
        work = jnp.where(sel, -3.0e38, work)
    es = [jnp.exp(v - vals[0]) for v in vals]
    inv = 1.0 / (es[0] + es[1] + es[2] + es[3])
    comb = jnp.zeros_like(logits)
    for e, sel in zip(es, sels):
        comb = comb + jnp.where(sel, e * inv, 0.0)
    comb_ref[...] = comb


def _norm(x, g, sc, sh, router=None):
    s, d = x.shape
    tm = min(256, s)
    row = pl.BlockSpec((tm, d), lambda i: (i, 0))
    vec = pl.BlockSpec((1, d), lambda i: (0, 0))
    if router is None:
        return pl.pallas_call(
            _norm_kernel, grid=(s // tm,),
            in_specs=[row, vec, vec, vec], out_specs=row,
            out_shape=jax.ShapeDtypeStruct((s, d), BF16),
            compiler_params=_cparams(("parallel",), 32), name="norm",
        )(x, g, sc, sh)
    rw, rb = router
    rw_hi = rw.astype(BF16)
    rw_lo = (rw - rw_hi.astype(F32)).astype(BF16)
    rw_spec = pl.BlockSpec((d, LANES), lambda i: (0, 0))
    return pl.pallas_call(
        _norm_router_kernel, grid=(s // tm,),
        in_specs=[row, vec, vec, vec, rw_spec, rw_spec,
                  pl.BlockSpec((1, LANES), lambda i: (0, 0))],
        out_specs=[row, pl.BlockSpec((tm, LANES), lambda i: (i, 0))],
        out_shape=[jax.ShapeDtypeStruct((s, d), BF16), jax.ShapeDtypeStruct((s, LANES), F32)],
        compiler_params=_cparams(("parallel",), 40), name="norm_router",
    )(x, g, sc, sh, rw_hi, rw_lo, rb)


IN_TN = 768
IN_MAIN_TILES = OFF_QLAT // IN_TN


def _in_proj_kernel(a_ref, w_ref, wt_ref, o_ref):
    j = pl.program_id(1)

    @pl.when(j < IN_MAIN_TILES)
    def _():
        o_ref[...] = _dot(a_ref[...], w_ref[...]).astype(o_ref.dtype)

    @pl.when(j >= IN_MAIN_TILES)
    def _():
        o_ref[...] = _dot(a_ref[...], wt_ref[...]).astype(o_ref.dtype)


def _in_proj(a, w_main, w_tail):
    m, k = a.shape
    tm = min(1024, m)
    n_tail = w_tail.shape[1] // IN_TN
    last = IN_MAIN_TILES - 1
    return pl.pallas_call(
        _in_proj_kernel, grid=(m // tm, IN_MAIN_TILES + n_tail),
        in_specs=[pl.BlockSpec((tm, k), lambda i, j: (i, 0)),
                  pl.BlockSpec((k, IN_TN), lambda i, j: (0, jnp.minimum(j, last))),
                  pl.BlockSpec((k, IN_TN), lambda i, j: (0, jnp.maximum(j - IN_MAIN_TILES, 0)))],
        out_specs=pl.BlockSpec((tm, IN_TN), lambda i, j: (i, j)),
        out_shape=jax.ShapeDtypeStruct((m, N_P), BF16),
        compiler_params=_cparams(("parallel", "arbitrary"), 56), name="in_proj",
    )(a, w_main, w_tail)


def _prep_kernel(p_ref, gains_ref, gql_ref, gkvl_ref, fb_ref, cb_ref, sb_ref,
                 qa_ref, ka_ref, va_ref, qb_ref, kb_ref, vb_ref,
                 qc_ref, kc_ref, vc_ref, ql_ref, kvl_ref, kr_ref, carry_ref):
    tm = p_ref.shape[0]
    qscale = HEAD_DIM ** -0.5 * LOG2E

    @pl.when(pl.program_id(0) == 0)
    def _():
        carry_ref[...] = jnp.zeros_like(carry_ref)

    def head_norm(off, g_row, scale):
        x = p_ref[:, off:off + HEAD_DIM].astype(F32)
        ms = jnp.mean(x * x, axis=-1, keepdims=True)
        return x * lax.rsqrt(ms + EPS) * (gains_ref[g_row:g_row + 1, :] * scale)

    cb, sb = cb_ref[...], sb_ref[...]

    def rope(y):
        return y * cb + pltpu.roll(y, HEAD_DIM // 2, 1) * sb

    for h in range(A_HEADS):
        c = h * HEAD_DIM
        qa_ref[:, c:c + HEAD_DIM] = head_norm(OFF_QA + c, 0, qscale).astype(BF16)
        ka_ref[:, c:c + HEAD_DIM] = head_norm(OFF_KA + c, 1, 1.0).astype(BF16)
    va_ref[...] = p_ref[:, OFF_VA:OFF_VA + A_HEADS * HEAD_DIM]

    for h in range(B_Q_HEADS):
        c = h * HEAD_DIM
        qb_ref[:, c:c + HEAD_DIM] = rope(head_norm(OFF_QB + c, 2, qscale)).astype(BF16)
    for h in range(B_KV_HEADS):
        c = h * HEAD_DIM
        kb_ref[:, c:c + HEAD_DIM] = rope(head_norm(OFF_KB + c, 3, 1.0)).astype(BF16)
    vb_ref[...] = p_ref[:, OFF_VB:OFF_VB + B_KV_HEADS * HEAD_DIM]

    z = p_ref[:, OFF_F:OFF_F + LANES].astype(F32) + fb_ref[...]
    logf = jnp.minimum(z, 0.0) - jnp.log1p(jnp.exp(-jnp.abs(z)))
    r_i = lax.broadcasted_iota(jnp.int32, (tm, tm), 0)
    c_i = lax.broadcasted_iota(jnp.int32, (tm, tm), 1)
    tri = jnp.where(r_i >= c_i, 1.0, 0.0).astype(F32)
    cum = jnp.dot(tri, logf, precision=lax.Precision.HIGHEST,
                  preferred_element_type=F32) + carry_ref[...]
    carry_ref[...] = cum[tm - 1:tm, :]
    cum2 = cum * LOG2E
    hi = cum2.astype(BF16).astype(F32)
    r1 = cum2 - hi
    mid = r1.astype(BF16).astype(F32)
    lo = r1 - mid
    lane = lax.broadcasted_iota(jnp.int32, (tm, LANES), 1)
    for h in range(C_HEADS):
        c = h * HEAD_DIM
        a = h * AUG
        hi_h, mid_h, lo_h = hi[:, h:h + 1], mid[:, h:h + 1], lo[:, h:h + 1]
        aug_q = jnp.where(lane == 0, hi_h, jnp.where(lane == 1, mid_h, jnp.where(
            lane == 2, lo_h, jnp.where(lane < 6, 1.0, 0.0))))
        aug_k = jnp.where(lane < 3, 1.0, jnp.where(lane == 3, -hi_h, jnp.where(
            lane == 4, -mid_h, jnp.where(lane == 5, -lo_h, 0.0))))
        qc_ref[:, a:a + HEAD_DIM] = head_norm(OFF_QC + c, 4, qscale).astype(BF16)
        qc_ref[:, a + HEAD_DIM:a + AUG] = aug_q.astype(BF16)
        kc_ref[h, :HEAD_DIM, :] = head_norm(OFF_KC + c, 5, 1.0).T.astype(BF16)
        kc_ref[h, HEAD_DIM:, :] = aug_k.T.astype(BF16)
    vc_ref[...] = p_ref[:, OFF_VC:OFF_VC + C_HEADS * HEAD_DIM]

    xq = p_ref[:, OFF_QLAT:OFF_QLAT + D_Q_LORA].astype(F32)
    ms = jnp.mean(xq * xq, axis=-1, keepdims=True)
    ql_ref[...] = (xq * lax.rsqrt(ms + EPS) * gql_ref[...]).astype(BF16)
    xkv = p_ref[:, OFF_KVLAT:OFF_KVLAT + D_KV_LORA].astype(F32)
    ms = jnp.mean(xkv * xkv, axis=-1, keepdims=True)
    kvl_ref[...] = (xkv * lax.rsqrt(ms + EPS) * gkvl_ref[...]).astype(BF16)
    kr_ref[...] = p_ref[:, OFF_KROPE:OFF_KROPE + LANES].astype(F32)


def _kt_shape_spec(n_heads, s, tm):
    blk = min(FLASH_BLK, s)
    per = blk // tm
    shape = jax.ShapeDtypeStruct((n_heads, s // blk, AUG, blk), BF16)
    spec = pl.BlockSpec((n_heads, None, AUG, tm), lambda i: (0, i // per, 0, i % per))
    return shape, spec


def _prep(p, gains, gql, gkvl, fb, cos_b, sin_b):
    s = p.shape[0]
    tm = min(PREP_ROWS, s)

    def row(w):
        return pl.BlockSpec((tm, w), lambda i: (i, 0))

    def const(r, w):
        return pl.BlockSpec((r, w), lambda i: (0, 0))

    widths = [1024, 1024, 1024, 1024, 256, 256, C_HEADS * AUG, None, 1024,
              D_Q_LORA, D_KV_LORA]
    out_shape = [jax.ShapeDtypeStruct((s, w), BF16) for w in widths if w]
    out_specs = [row(w) for w in widths if w]
    kt_shape, kt_spec = _kt_shape_spec(C_HEADS, s, tm)
    out_shape.insert(7, kt_shape)
    out_specs.insert(7, kt_spec)
    out_shape.append(jax.ShapeDtypeStruct((s, LANES), F32))
    out_specs.append(row(LANES))
    return pl.pallas_call(
        _prep_kernel, grid=(s // tm,),
        in_specs=[row(N_P), const(8, LANES), const(1, D_Q_LORA), const(1, D_KV_LORA),
                  const(1, LANES), row(LANES), row(LANES)],
        out_specs=out_specs, out_shape=out_shape,
        scratch_shapes=[pltpu.VMEM((1, LANES), F32)],
        compiler_params=_cparams(("arbitrary",), 48), name="prep",
    )(p, gains, gql, gkvl, fb, cos_b, sin_b)


def _rope_d(y, cd, sd):
    lane = lax.broadcasted_iota(jnp.int32, y.shape, 1)
    half = D_ROPE // 2
    partner = jnp.where(lane < half, pltpu.roll(y, LANES - half, 1), pltpu.roll(y, half, 1))
    return y * cd + partner * sd


def _prep_d_kernel(ql_ref, kvl_ref, kr_ref, wq_ref, wkv_ref, gq_ref, gk_ref, cd_ref, sd_ref,
                   qd_ref, kd_ref, vd_ref):
    qscale = D_QK ** -0.5 * LOG2E
    cd, sd = cd_ref[...], sd_ref[...]
    q = _dot(ql_ref[...], wq_ref[...])
    kv = _dot(kvl_ref[...], wkv_ref[...])
    kr = kr_ref[...]
    ss_r = jnp.sum(kr * kr, axis=-1, keepdims=True)
    gq = gq_ref[...]
    gk = gk_ref[...]
    for h in range(D_HEADS):
        a = h * AUG
        qn, qr = q[:, a:a + HEAD_DIM], q[:, a + HEAD_DIM:a + AUG]
        ms = (jnp.sum(qn * qn, axis=-1, keepdims=True)
              + jnp.sum(qr * qr, axis=-1, keepdims=True)) * (1.0 / D_QK)
        r = lax.rsqrt(ms + EPS) * qscale
        qd_ref[:, a:a + HEAD_DIM] = (qn * r * gq[:, :HEAD_DIM]).astype(BF16)
        qd_ref[:, a + HEAD_DIM:a + AUG] = _rope_d(qr * r * gq[:, HEAD_DIM:], cd, sd).astype(BF16)
        kn = kv[:, h * D_NOPE:(h + 1) * D_NOPE]
        ms = (jnp.sum(kn * kn, axis=-1, keepdims=True) + ss_r) * (1.0 / D_QK)
        r = lax.rsqrt(ms + EPS)
        kd_ref[h, :HEAD_DIM, :] = (kn * r * gk[:, :HEAD_DIM]).T.astype(BF16)
        kd_ref[h, HEAD_DIM:, :] = _rope_d(kr * r * gk[:, HEAD_DIM:], cd, sd).T.astype(BF16)
    vd_ref[...] = kv[:, D_HEADS * D_NOPE:].astype(BF16)


def _prep_d(ql, kvl, kr, wq, wkv, gq, gk, cos_d, sin_d):
    s = ql.shape[0]
    tm = min(PREP_ROWS, s)

    def row(w):
        return pl.BlockSpec((tm, w), lambda i: (i, 0))

    def const(r, w):
        return pl.BlockSpec((r, w), lambda i: (0, 0))

    kt_shape, kt_spec = _kt_shape_spec(D_HEADS, s, tm)
    return pl.pallas_call(
        _prep_d_kernel, grid=(s // tm,),
        in_specs=[row(D_Q_LORA), row(D_KV_LORA), row(LANES),
                  const(D_Q_LORA, D_HEADS * AUG), const(D_KV_LORA, 2 * D_HEADS * D_NOPE),
                  const(1, AUG), const(1, AUG), row(LANES), row(LANES)],
        out_specs=[row(D_HEADS * AUG), kt_spec, row(D_HEADS * D_V)],
        out_shape=[jax.ShapeDtypeStruct((s, D_HEADS * AUG), BF16), kt_shape,
                   jax.ShapeDtypeStruct((s, D_HEADS * D_V), BF16)],
        compiler_params=_cparams(("parallel",), 40), name="prep_d",
    )(ql, kvl, kr, wq, wkv, gq, gk, cos_d, sin_d)


def _flash_kernel(q_ref, kt_ref, v_ref, o_ref, m_ref, l_ref, acc_ref, *, blk, unit):
    i = pl.program_id(1)
    m_ref[...] = jnp.full_like(m_ref, NEG_INF)
    l_ref[...] = jnp.zeros_like(l_ref)
    acc_ref[...] = jnp.zeros_like(acc_ref)
    rc = FLASH_ROW_CHUNK

    def block(j, row0, nrows, nkeys, masked):
        start = pl.multiple_of(j * blk, blk)
        for g in range(FLASH_HEADS):
            s = _dot(q_ref[row0:row0 + nrows, g * AUG:(g + 1) * AUG], kt_ref[g, j, :, :nkeys])
            v = v_ref[pl.ds(start, nkeys), g * LANES:(g + 1) * LANES]
            ps, alphas = [], []
            for r in range(nrows // rc):
                rows = slice(row0 + r * rc, row0 + (r + 1) * rc)
                sc = s[r * rc:(r + 1) * rc, :]
                if masked:
                    qi = lax.broadcasted_iota(jnp.int32, sc.shape, 0) + (row0 + r * rc)
                    ki = lax.broadcasted_iota(jnp.int32, sc.shape, 1)
                    if unit > 1:
                        shift = unit.bit_length() - 1
                        qi, ki = qi >> shift, ki >> shift
                    sc = jnp.where(ki <= qi, sc, NEG_INF)
                m_prev = m_ref[g, rows, :]
                m_new = jnp.maximum(m_prev, jnp.max(sc, axis=-1, keepdims=True))
                alpha = jnp.exp2(m_prev - m_new)
                pc = [jnp.exp2(sc[:, c * LANES:(c + 1) * LANES] - m_new)
                      for c in range(nkeys // LANES)]
                lsum = pc[0]
                for c in range(1, len(pc)):
                    lsum = lsum + pc[c]
                l_ref[g, rows, :] = alpha * l_ref[g, rows, :] + lsum
                m_ref[g, rows, :] = m_new
                ps.append(jnp.concatenate(pc, axis=1).astype(BF16))
                alphas.append(alpha)
            p = jnp.concatenate(ps, axis=0)
            alpha = jnp.concatenate(alphas, axis=0)
            acc_ref[g, row0:row0 + nrows, :] = (alpha * acc_ref[g, row0:row0 + nrows, :]
                                                + _dot(p, v))

    def body(j, carry):
        block(j, 0, blk, blk, False)
        return carry

    lax.fori_loop(0, i, body, 0)
    half = blk // 2
    if half % max(rc, unit, LANES) == 0:
        block(i, 0, half, half, True)
        block(i, half, half, blk, True)
    else:
        block(i, 0, blk, blk, True)
    for g in range(FLASH_HEADS):
        l = jnp.sum(l_ref[g], axis=-1, keepdims=True)
        o_ref[:, g * LANES:(g + 1) * LANES] = (acc_ref[g] / l).astype(o_ref.dtype)


def _flash(q, kt, v, n_heads, unit):
    s = q.shape[0]
    blk = kt.shape[3]
    dv = v.shape[1] // n_heads
    g = FLASH_HEADS
    assert dv == LANES and n_heads % g == 0
    once = pl.Buffered(1)
    return pl.pallas_call(
        functools.partial(_flash_kernel, blk=blk, unit=unit),
        grid=(n_heads // g, s // blk),
        in_specs=[pl.BlockSpec((blk, g * AUG), lambda h, i: (i, h)),
                  pl.BlockSpec((g, s // blk, AUG, blk), lambda h, i: (h, 0, 0, 0), pipeline_mode=once),
                  pl.BlockSpec((s, g * dv), lambda h, i: (0, h), pipeline_mode=once)],
        out_specs=pl.BlockSpec((blk, g * dv), lambda h, i: (i, h)),
        out_shape=jax.ShapeDtypeStruct((s, n_heads * dv), BF16),
        scratch_shapes=[pltpu.VMEM((g, blk, LANES), F32), pltpu.VMEM((g, blk, LANES), F32),
                        pltpu.VMEM((g, blk, dv), F32)],
        compiler_params=_cparams(("parallel", "arbitrary"), 56), name=f"flash_u{unit}",
    )(q, kt, v)


A_PREV_BLOCKS = A_PREV_CHUNKS * CHUNK // BAND_Q


def _attn_a_kernel(q_ref, k0_ref, k1_ref, k2_ref, v0_ref, v1_ref, v2_ref, t_ref, mk_ref, o_ref):
    mask = mk_ref[...]
    for h in range(A_HEADS):
        cols = slice(h * HEAD_DIM, (h + 1) * HEAD_DIM)
        k = jnp.concatenate([k0_ref[:, cols], k1_ref[:, cols], k2_ref[:, cols]], axis=0)
        v = jnp.concatenate([v0_ref[:, cols], v1_ref[:, cols], v2_ref[:, cols]], axis=0)
        s = _dot_nt(q_ref[:, cols], k) + t_ref[h] + mask
        m = jnp.max(s, axis=-1, keepdims=True)
        p = jnp.exp2(s - m)
        l = jnp.sum(p, axis=-1, keepdims=True)
        o_ref[:, cols] = (_dot(p.astype(BF16), v) / l).astype(o_ref.dtype)


def _attn_a(q, k, v, table, mask):
    s, w = q.shape
    row = pl.BlockSpec((BAND_Q, w), lambda i: (i, 0))
    prev = [pl.BlockSpec((BAND_Q, w), functools.partial(
        lambda i, back: (jnp.maximum(i - back, 0), 0), back=back))
        for back in range(A_PREV_BLOCKS, -1, -1)]
    return pl.pallas_call(
        _attn_a_kernel, grid=(s // BAND_Q,),
        in_specs=[row] + prev + prev + [
            pl.BlockSpec((A_HEADS, BAND_Q, A_WIN), lambda i: (0, 0, 0)),
            pl.BlockSpec((None, BAND_Q, A_WIN), lambda i: (jnp.minimum(i, A_PREV_BLOCKS), 0, 0))],
        out_specs=row,
        out_shape=jax.ShapeDtypeStruct((s, w), BF16),
        compiler_params=_cparams(("arbitrary",), 40), name="attn_a",
    )(q, k, k, k, v, v, v, table, mask)


def _a_mask():
    b = np.arange(A_PREV_BLOCKS + 1)[:, None, None]
    q = np.arange(BAND_Q)[None, :, None]
    w = np.arange(A_WIN)[None, None, :]
    dchunk = (q + A_PREV_BLOCKS * BAND_Q) // CHUNK - w // CHUNK
    valid = (dchunk >= 0) & (dchunk <= A_PREV_CHUNKS) & (w >= (A_PREV_BLOCKS - b) * BAND_Q)
    return jnp.asarray(np.where(valid, 0.0, NEG_INF), F32)


def _a_table(rel_bias):
    n, m = BAND_Q, A_WIN
    u = np.arange(n + m)
    dist = A_PREV_BLOCKS * BAND_Q + (n - 1) - u
    idx = np.clip(dist, -REL_CLIP, REL_CLIP) + REL_CLIP
    z = rel_bias.astype(F32).T[:, jnp.asarray(idx)] * LOG2E
    skew = jnp.broadcast_to(z[:, None, :], (A_HEADS, n, n + m))
    skew = skew.reshape(A_HEADS, n * (n + m))[:, :n * (n + m - 1)]
    return skew.reshape(A_HEADS, n, n + m - 1)[:, :, n - 1:n - 1 + m]


B_KROWS = B_PREV_CHUNKS * CHUNK


def _attn_b_kernel(sink_ref, q_ref, k0_ref, k1_ref, k2_ref, v0_ref, v1_ref, v2_ref, o_ref):
    i = pl.program_id(0)
    group = B_Q_HEADS // B_KV_HEADS
    rows = group * BAND_Q
    shift = CHUNK.bit_length() - 1
    t = ((lax.broadcasted_iota(jnp.int32, (rows, B_WIN), 0) & (BAND_Q - 1)) + B_KROWS) >> shift
    w = lax.broadcasted_iota(jnp.int32, (rows, B_WIN), 1)
    first = jnp.where(i == 0, B_KROWS, 0)
    valid = jnp.where(w >= first, jnp.abs(t - (w >> shift) - 1), 2) <= 1
    head = lax.broadcasted_iota(jnp.int32, (rows, 1), 0) >> (BAND_Q.bit_length() - 1)
    for g in range(B_KV_HEADS):
        kc = slice(g * HEAD_DIM, (g + 1) * HEAD_DIM)
        k = jnp.concatenate([k0_ref[:, kc], k1_ref[:, kc], k2_ref[:, kc]], axis=0)
        v = jnp.concatenate([v0_ref[:, kc], v1_ref[:, kc], v2_ref[:, kc]], axis=0)
        q = jnp.concatenate([q_ref[:, (g * group + hi) * HEAD_DIM:(g * group + hi + 1) * HEAD_DIM]
                             for hi in range(group)], axis=0)
        s = jnp.where(valid, _dot_nt(q, k), NEG_INF)
        sink = jnp.zeros((rows, 1), F32)
        for hi in range(group):
            sink = jnp.where(head == hi, sink_ref[g * group + hi] * LOG2E, sink)
        m = jnp.maximum(jnp.max(s, axis=-1, keepdims=True), sink)
        p = jnp.exp2(s - m)
        l = jnp.sum(p, axis=-1, keepdims=True) + jnp.exp2(sink - m)
        o = (_dot(p.astype(BF16), v) / l).astype(o_ref.dtype)
        for hi in range(group):
            c = (g * group + hi) * HEAD_DIM
            o_ref[:, c:c + HEAD_DIM] = o[hi * BAND_Q:(hi + 1) * BAND_Q, :]


def _attn_b(q, k, v, sinks):
    s, w = q.shape
    kw = k.shape[1]
    per = BAND_Q // B_KROWS
    row = pl.BlockSpec((BAND_Q, w), lambda i: (i, 0))
    kv = [pl.BlockSpec((B_KROWS, kw), functools.partial(
        lambda i, off: (jnp.maximum(i * per + off, 0), 0), off=off)) for off in (-1, 0, 1)]
    return pl.pallas_call(
        _attn_b_kernel, grid=(s // BAND_Q,),
        in_specs=[pl.BlockSpec(memory_space=pltpu.SMEM), row] + kv + kv,
        out_specs=row,
        out_shape=jax.ShapeDtypeStruct((s, w), BF16),
        compiler_params=_cparams(("arbitrary",), 40), name="attn_b",
    )(sinks, q, k, k, k, v, v, v)


def _merge_kernel(h_ref, oa_ref, ob_ref, oc_ref, od_ref, wg_ref, bg_ref, wb_ref, out_ref, acc_ref):
    b = pl.program_id(2)

    for n, o_ref in enumerate((oa_ref, ob_ref, oc_ref, od_ref)):
        @pl.when(b == n)
        def _(o_ref=o_ref, n=n):
            gate = jax.nn.sigmoid(_dot(h_ref[...], wg_ref[...]) + bg_ref[...])
            val = gate * _dot(o_ref[...], wb_ref[...].astype(BF16))
            if n == 0:
                acc_ref[...] = val
            elif n < N_BRANCH - 1:
                acc_ref[...] += val
            else:
                out_ref[...] = (acc_ref[...] + val).astype(out_ref.dtype)


def _merge(h, outs, wg, bg, wb, l):
    s, d = h.shape
    tm, tn = min(1024, s), min(512, d)
    o_spec = pl.BlockSpec((tm, BRANCH_W), lambda i, j, b: (i, 0), pipeline_mode=pl.Buffered(1))
    return pl.pallas_call(
        _merge_kernel, grid=(s // tm, d // tn, N_BRANCH),
        in_specs=[pl.BlockSpec((tm, d), lambda i, j, b: (i, 0)),
                  o_spec, o_spec, o_spec, o_spec,
                  pl.BlockSpec((None, None, d, tn), lambda i, j, b: (l, b, 0, j)),
                  pl.BlockSpec((None, 1, tn), lambda i, j, b: (b, 0, j)),
                  pl.BlockSpec((None, None, BRANCH_W, tn), lambda i, j, b: (l, b, 0, j))],
        out_specs=pl.BlockSpec((tm, tn), lambda i, j, b: (i, j)),
        out_shape=jax.ShapeDtypeStruct((s, d), BF16),
        scratch_shapes=[pltpu.VMEM((tm, tn), F32)],
        compiler_params=_cparams(("parallel", "parallel", "arbitrary"), 56), name="merge",
    )(h, *outs, wg, bg, wb)


def _resid_kernel(a_ref, w_ref, x_ref, g_ref, o_ref):
    o_ref[...] = x_ref[...] + g_ref[...] * _dot(a_ref[...], w_ref[...].astype(BF16))


def _resid_moe_kernel(a_ref, w_ref, c_ref, bd_ref, x_ref, g_ref, o_ref):
    y = _dot(a_ref[...], w_ref[...].astype(BF16)) + _dot(c_ref[...].astype(BF16), bd_ref[...])
    o_ref[...] = x_ref[...] + g_ref[...] * y


def _resid(a, w, l, x, gate, comb=None, b_dn=None):
    s, k = a.shape
    d = w.shape[2]
    tm, tn = min(1024, s), min(512, d)
    a_spec = pl.BlockSpec((tm, k), lambda i, j: (i, 0))
    w_spec = pl.BlockSpec((None, k, tn), lambda i, j: (l, 0, j))
    x_spec = pl.BlockSpec((tm, tn), lambda i, j: (i, j))
    g_spec = pl.BlockSpec((1, tn), lambda i, j: (0, j))
    if comb is None:
        kern, ins = _resid_kernel, (a, w, x, gate)
        in_specs = [a_spec, w_spec, x_spec, g_spec]
    else:
        kern, ins = _resid_moe_kernel, (a, w, comb, b_dn, x, gate)
        in_specs = [a_spec, w_spec, pl.BlockSpec((tm, LANES), lambda i, j: (i, 0)),
                    pl.BlockSpec((LANES, tn), lambda i, j: (0, j)), x_spec, g_spec]
    return pl.pallas_call(
        kern, grid=(s // tm, d // tn), in_specs=in_specs, out_specs=x_spec,
        out_shape=jax.ShapeDtypeStruct((s, d), F32),
        compiler_params=_cparams(("parallel", "parallel"), 48), name="resid",
    )(*ins)


def _moe_up_kernel(h_ref, w_ref, b_ref, c_ref, o_ref, *, eb):
    j = pl.program_id(1)
    h = h_ref[...]
    comb = c_ref[...]
    lane = lax.broadcasted_iota(jnp.int32, comb.shape, 1)
    for e in range(eb):
        gu = _dot(h, w_ref[e].astype(BF16)) + b_ref[e]
        glu = jnp.minimum(gu[:, :D_EXPERT], SWIGLU_LIMIT)
        lin = jnp.clip(gu[:, D_EXPERT:], -SWIGLU_LIMIT, SWIGLU_LIMIT)
        act = glu * jax.nn.sigmoid(SWIGLU_ALPHA * glu) * (lin + 1.0)
        ce = jnp.sum(jnp.where(lane == j * eb + e, comb, 0.0), axis=-1, keepdims=True)
        o_ref[:, e * D_EXPERT:(e + 1) * D_EXPERT] = (act * ce).astype(o_ref.dtype)


def _moe_up(h, w_gu, l, b_gu, comb):
    s, d = h.shape
    tm, eb = min(1024, s), 2
    return pl.pallas_call(
        functools.partial(_moe_up_kernel, eb=eb), grid=(s // tm, N_EXPERTS // eb),
        in_specs=[pl.BlockSpec((tm, d), lambda i, j: (i, 0)),
                  pl.BlockSpec((None, eb, d, 2 * D_EXPERT), lambda i, j: (l, j, 0, 0)),
                  pl.BlockSpec((eb, 1, 2 * D_EXPERT), lambda i, j: (j, 0, 0)),
                  pl.BlockSpec((tm, LANES), lambda i, j: (i, 0))],
        out_specs=pl.BlockSpec((tm, eb * D_EXPERT), lambda i, j: (i, j)),
        out_shape=jax.ShapeDtypeStruct((s, N_EXPERTS * D_EXPERT), BF16),
        compiler_params=_cparams(("parallel", "parallel"), 48), name="moe_up",
    )(h, w_gu, b_gu, comb)


def _w_in_tail(w):
    d = w.shape[0]
    zeros = lambda n: jnp.zeros((d, n), BF16)
    return jnp.concatenate(
        [w[:, C_HEADS:].astype(BF16), zeros(LANES - D_ROPE), w[:, :C_HEADS].astype(BF16),
         zeros(N_P - OFF_F - C_HEADS)], axis=1)


def _reorder_w_q_b(w):
    r = w.shape[0]
    w = w.reshape(r, D_HEADS, D_QK)
    w = jnp.pad(w, ((0, 0), (0, 0), (0, AUG - D_QK)))
    return w.reshape(r, D_HEADS * AUG)


def _reorder_w_kv_b(w):
    r = w.shape[0]
    w = w.reshape(r, D_HEADS, D_NOPE + D_V)
    return jnp.concatenate([w[:, :, :D_NOPE].reshape(r, -1), w[:, :, D_NOPE:].reshape(r, -1)], axis=1)


def _pad_lanes(v, width, value=0.0):
    return jnp.pad(v, (0, width - v.shape[0]), constant_values=value).reshape(1, width)


def kernel(x, c, positions, ada_w, ada_b, ada_layer, norm1_g, norm2_g, w_in, a_q_norm, a_k_norm, a_rel_bias, b_q_norm, b_k_norm, b_sinks, c_q_norm, c_k_norm, c_f_bias, d_q_a_norm, d_w_q_b, d_kv_a_norm, d_w_kv_b, d_q_norm, d_k_norm, w_branch, w_gate, b_gate, w_out, router_w, router_b, w_gu, b_gu, w_dn, b_dn):
    batch, s, d = x.shape
    assert batch == 1, "kernels are written for a single sequence"
    depth = w_in.shape[0]
    xs = x.reshape(s, d)

    base_mod = _adaln(c.reshape(d, 1), ada_w, ada_b).reshape(6, d)
    cos_b, sin_b, cos_d, sin_d = _rope_tables(positions.reshape(s, 1).astype(F32))
    a_mask = _a_mask()
    w_gate_bf16 = w_gate.astype(BF16)

    for l in range(depth):
        mod = base_mod + ada_layer[l]
        shift1, scale1, gate1, shift2, scale2, gate2 = (mod[j:j + 1] for j in range(6))

        h = _norm(xs, norm1_g[l].reshape(1, d), scale1, shift1)
        p = _in_proj(h, w_in[l, :, :OFF_QLAT].astype(BF16), _w_in_tail(w_in[l, :, OFF_QLAT:]))
        gains = jnp.stack([a_q_norm[l], a_k_norm[l], b_q_norm[l], b_k_norm[l],
                           c_q_norm[l], c_k_norm[l], jnp.zeros_like(a_q_norm[l]),
                           jnp.zeros_like(a_q_norm[l])])
        (qa, ka, va, qb, kb, vb, qc, kc, vc, ql, kvl, kr) = _prep(
            p, gains, d_q_a_norm[l].reshape(1, -1), d_kv_a_norm[l].reshape(1, -1),
            _pad_lanes(c_f_bias[l], LANES), cos_b, sin_b)
        qd, kd, vd = _prep_d(
            ql, kvl, kr, _reorder_w_q_b(d_w_q_b[l]).astype(BF16),
            _reorder_w_kv_b(d_w_kv_b[l]).astype(BF16),
            _pad_lanes(d_q_norm[l], AUG), _pad_lanes(d_k_norm[l], AUG), cos_d, sin_d)

        o_a = _attn_a(qa, ka, va, _a_table(a_rel_bias[l]), a_mask)
        o_b = _attn_b(qb, kb, vb, b_sinks[l].astype(F32))
        o_c = _flash(qc, kc, vc, C_HEADS, 1)
        o_d = _flash(qd, kd, vd, D_HEADS, CHUNK)

        merged = _merge(h, (o_a, o_b, o_c, o_d), w_gate_bf16,
                        b_gate[l].reshape(N_BRANCH, 1, d), w_branch, l)
        xs = _resid(merged, w_out, l, xs, gate1)

        rw = jnp.pad(router_w[l], ((0, 0), (0, LANES - N_EXPERTS)))
        rb = _pad_lanes(router_b[l].astype(F32), LANES, NEG_INF)
        h2, comb = _norm(xs, norm2_g[l].reshape(1, d), scale2, shift2, router=(rw, rb))
        act = _moe_up(h2, w_gu, l, b_gu[l].reshape(N_EXPERTS, 1, 2 * D_EXPERT), comb)
        bdn = jnp.pad(b_dn[l], ((0, LANES - N_EXPERTS), (0, 0))).astype(BF16)
        xs = _resid(act, w_dn.reshape(depth, N_EXPERTS * D_EXPERT, d), l, xs, gate2,
                    comb=comb, b_dn=bdn)
    return xs.reshape(batch, s, d)
```

```python
import functools
import math

import numpy as np
import jax
import jax.numpy as jnp
from jax import lax
from jax.experimental import pallas as pl
from jax.experimental.pallas import tpu as pltpu

F32 = jnp.float32
BF16 = jnp.bfloat16

CHUNK = 64
HEAD_DIM = 128
N_BRANCH = 4
BRANCH_W = 1024
ROPE_THETA = 10000.0
EPS = 1e-6
NEG_INF = -1e30
A_HEADS = 8
A_PREV_CHUNKS = 8
REL_CLIP = 256
B_Q_HEADS = 8
B_KV_HEADS = 2
B_PREV_CHUNKS = 2
C_HEADS = 8
D_HEADS = 8
D_Q_LORA = 896
D_KV_LORA = 256
D_NOPE = 128
D_ROPE = 64
D_V = 128
D_QK = D_NOPE + D_ROPE
N_EXPERTS = 32
TOP_K = 4
D_EXPERT = 128
SWIGLU_LIMIT = 7.0
SWIGLU_ALPHA = 1.702

LANES = 128
V7X_VMEM_BYTES = 64 * 1024 * 1024

LOG2E = math.log2(math.e)

OFF_QA, OFF_KA, OFF_VA = 0, 1024, 2048
OFF_QB, OFF_KB, OFF_VB = 3072, 4096, 4352
OFF_QC, OFF_KC, OFF_VC = 4608, 5632, 6656
OFF_QLAT = 7680
OFF_KVLAT = OFF_QLAT + D_Q_LORA
OFF_KROPE = OFF_KVLAT + D_KV_LORA
OFF_F = OFF_KROPE + LANES
N_P = 9216

AUG = 2 * HEAD_DIM
FLASH_BLK = 1024
FLASH_ROW_CHUNK = 32
FLASH_HEADS = 2
PREP_ROWS = 256
BAND_Q = 256
A_WIN = BAND_Q + A_PREV_CHUNKS * CHUNK
B_WIN = BAND_Q + B_PREV_CHUNKS * CHUNK


def _cparams(sems, vmem_mb):
    return pltpu.CompilerParams(dimension_semantics=sems,
                                vmem_limit_bytes=vmem_mb * 1024 * 1024)


def _dot(a, b):
    return jnp.dot(a, b, preferred_element_type=F32)


def _dot_nt(a, b):
    return lax.dot_general(a, b, (((1,), (1,)), ((), ())), preferred_element_type=F32)


def _adaln_kernel(c_ref, w_ref, b_ref, o_ref):
    c = c_ref[...]
    s = c * jax.nn.sigmoid(c)
    o_ref[...] = jnp.sum(w_ref[...] * s, axis=0, keepdims=True) + b_ref[...]


def _adaln(c_col, ada_w, ada_b):
    d, n = ada_w.shape
    tn = min(512, n)
    return pl.pallas_call(
        _adaln_kernel,
        grid=(n // tn,),
        in_specs=[pl.BlockSpec((d, 1), lambda j: (0, 0)),
                  pl.BlockSpec((d, tn), lambda j: (0, j)),
                  pl.BlockSpec((1, tn), lambda j: (0, j))],
        out_specs=pl.BlockSpec((1, tn), lambda j: (0, j)),
        out_shape=jax.ShapeDtypeStruct((1, n), F32),
        compiler_params=_cparams(("parallel",), 40),
        name="adaln",
    )(c_col, ada_w, ada_b.reshape(1, n))


def _rope_table_kernel(pos_ref, fb_ref, fd_ref, cb_ref, sb_ref, cd_ref, sd_ref):
    pos = pos_ref[...]
    lane = lax.broadcasted_iota(jnp.int32, (1, LANES), 1)
    ab = pos * fb_ref[...]
    sb = jnp.sin(ab)
    cb_ref[...] = jnp.cos(ab)
    sb_ref[...] = jnp.where(lane < HEAD_DIM // 2, -sb, sb)
    ad = pos * fd_ref[...]
    sd = jnp.sin(ad)
    cd_ref[...] = jnp.where(lane < D_ROPE, jnp.cos(ad), 1.0)
    sd_ref[...] = jnp.where(lane < D_ROPE // 2, -sd, jnp.where(lane < D_ROPE, sd, 0.0))


def _rope_tables(pos_col):
    s = pos_col.shape[0]
    tm = min(512, s)
    lane = np.arange(LANES)
    half_b = HEAD_DIM // 2
    fb = ROPE_THETA ** (-(lane % half_b).astype(np.float64) / half_b)
    half_d = D_ROPE // 2
    fd = np.where(lane < D_ROPE, ROPE_THETA ** (-(lane % half_d).astype(np.float64) / half_d), 0.0)
    fb = jnp.asarray(fb, F32).reshape(1, LANES)
    fd = jnp.asarray(fd, F32).reshape(1, LANES)
    tab = jax.ShapeDtypeStruct((s, LANES), F32)
    row = pl.BlockSpec((tm, LANES), lambda i: (i, 0))
    const = pl.BlockSpec((1, LANES), lambda i: (0, 0))
    return pl.pallas_call(
        _rope_table_kernel,
        grid=(s // tm,),
        in_specs=[pl.BlockSpec((tm, 1), lambda i: (i, 0)), const, const],
        out_specs=[row, row, row, row],
        out_shape=[tab, tab, tab, tab],
        compiler_params=_cparams(("parallel",), 32),
        name="rope_tables",
    )(pos_col, fb, fd)


def _mod_norm(x, g, sc, sh):
    ms = jnp.mean(x * x, axis=-1, keepdims=True)
    y = x * lax.rsqrt(ms + EPS)
    return (y * g) * (1.0 + sc) + sh


def _norm_kernel(x_ref, g_ref, sc_ref, sh_ref, h_ref):
    h_ref[...] = _mod_norm(x_ref[...], g_ref[...], sc_ref[...], sh_ref[...]).astype(h_ref.dtype)


def _norm_router_kernel(x_ref, g_ref, sc_ref, sh_ref, rwh_ref, rwl_ref, rb_ref, h_ref, comb_ref):
    h = _mod_norm(x_ref[...], g_ref[...], sc_ref[...], sh_ref[...])
    h_hi = h.astype(BF16)
    h_ref[...] = h_hi
    h_lo = (h - h_hi.astype(F32)).astype(BF16)
    rw_hi = rwh_ref[...]
    logits = (_dot(h_hi, rw_hi) + _dot(h_lo, rw_hi) + _dot(h_hi, rwl_ref[...])
              + rb_ref[...])
    lane = lax.broadcasted_iota(jnp.int32, logits.shape, 1).astype(F32)
    work = logits
    vals, sels = [], []
    for _ in range(TOP_K):
        m = jnp.max(work, axis=-1, keepdims=True)
        idx = jnp.min(jnp.where(work == m, lane, float(LANES)), axis=-1, keepdims=True)
        sel = lane == idx
        vals.append(m)
        sels.append(sel)
        work = jnp.where(sel, -3.0e38, work)
    es = [jnp.exp(v - vals[0]) for v in vals]
    inv = 1.0 / (es[0] + es[1] + es[2] + es[3])
    comb = jnp.zeros_like(logits)
    for e, sel in zip(es, sels):
        comb = comb + jnp.where(sel, e * inv, 0.0)
    comb_ref[...] = comb


def _norm(x, g, sc, sh, router=None):
    s, d = x.shape
    tm = min(256, s)
    row = pl.BlockSpec((tm, d), lambda i: (i, 0))
    vec = pl.BlockSpec((1, d), lambda i: (0, 0))
    if router is None:
        return pl.pallas_call(
            _norm_kernel, grid=(s // tm,),
            in_specs=[row, vec, vec, vec], out_specs=row,
            out_shape=jax.ShapeDtypeStruct((s, d), BF16),
            compiler_params=_cparams(("parallel",), 32), name="norm",
        )(x, g, sc, sh)
    rw, rb = router
    rw_hi = rw.astype(BF16)
    rw_lo = (rw - rw_hi.astype(F32)).astype(BF16)
    rw_spec = pl.BlockSpec((d, LANES), lambda i: (0, 0))
    return pl.pallas_call(
        _norm_router_kernel, grid=(s // tm,),
        in_specs=[row, vec, vec, vec, rw_spec, rw_spec,
                  pl.BlockSpec((1, LANES), lambda i: (0, 0))],
        out_specs=[row, pl.BlockSpec((tm, LANES), lambda i: (i, 0))],
        out_shape=[jax.ShapeDtypeStruct((s, d), BF16), jax.ShapeDtypeStruct((s, LANES), F32)],
        compiler_params=_cparams(("parallel",), 40), name="norm_router",
    )(x, g, sc, sh, rw_hi, rw_lo, rb)


IN_TN = 768
IN_MAIN_TILES = OFF_QLAT // IN_TN


def _in_proj_kernel(a_ref, w_ref, wt_ref, o_ref):
    j = pl.program_id(1)

    @pl.when(j < IN_MAIN_TILES)
    def _():
        o_ref[...] = _dot(a_ref[...], w_ref[...]).astype(o_ref.dtype)

    @pl.when(j >= IN_MAIN_TILES)
    def _():
        o_ref[...] = _dot(a_ref[...], wt_ref[...]).astype(o_ref.dtype)


def _in_proj(a, w_main, w_tail):
    m, k = a.shape
    tm = min(1024, m)
    n_tail = w_tail.shape[1] // IN_TN
    last = IN_MAIN_TILES - 1
    return pl.pallas_call(
        _in_proj_kernel, grid=(m // tm, IN_MAIN_TILES + n_tail),
        in_specs=[pl.BlockSpec((tm, k), lambda i, j: (i, 0)),
                  pl.BlockSpec((k, IN_TN), lambda i, j: (0, jnp.minimum(j, last))),
                  pl.BlockSpec((k, IN_TN), lambda i, j: (0, jnp.maximum(j - IN_MAIN_TILES, 0)))],
        out_specs=pl.BlockSpec((tm, IN_TN), lambda i, j: (i, j)),
        out_shape=jax.ShapeDtypeStruct((m, N_P), BF16),
        compiler_params=_cparams(("parallel", "arbitrary"), 56), name="in_proj",
    )(a, w_main, w_tail)


def _prep_kernel(p_ref, gains_ref, gql_ref, gkvl_ref, fb_ref, cb_ref, sb_ref,
                 qa_ref, ka_ref, va_ref, qb_ref, kb_ref, vb_ref,
                 qc_ref, kc_ref, vc_ref, ql_ref, kvl_ref, kr_ref, carry_ref):
    tm = p_ref.shape[0]
    qscale = HEAD_DIM ** -0.5 * LOG2E

    @pl.when(pl.program_id(0) == 0)
    def _():
        carry_ref[...] = jnp.zeros_like(carry_ref)

    def head_norm(off, g_row, scale):
        x = p_ref[:, off:off + HEAD_DIM].astype(F32)
        ms = jnp.mean(x * x, axis=-1, keepdims=True)
        return x * lax.rsqrt(ms + EPS) * (gains_ref[g_row:g_row + 1, :] * scale)

    cb, sb = cb_ref[...], sb_ref[...]

    def rope(y):
        return y * cb + pltpu.roll(y, HEAD_DIM // 2, 1) * sb

    for h in range(A_HEADS):
        c = h * HEAD_DIM
        qa_ref[:, c:c + HEAD_DIM] = head_norm(OFF_QA + c, 0, qscale).astype(BF16)
        ka_ref[:, c:c + HEAD_DIM] = head_norm(OFF_KA + c, 1, 1.0).astype(BF16)
    va_ref[...] = p_ref[:, OFF_VA:OFF_VA + A_HEADS * HEAD_DIM]

    for h in range(B_Q_HEADS):
        c = h * HEAD_DIM
        qb_ref[:, c:c + HEAD_DIM] = rope(head_norm(OFF_QB + c, 2, qscale)).astype(BF16)
    for h in range(B_KV_HEADS):
        c = h * HEAD_DIM
        kb_ref[:, c:c + HEAD_DIM] = rope(head_norm(OFF_KB + c, 3, 1.0)).astype(BF16)
    vb_ref[...] = p_ref[:, OFF_VB:OFF_VB + B_KV_HEADS * HEAD_DIM]

    z = p_ref[:, OFF_F:OFF_F + LANES].astype(F32) + fb_ref[...]
    logf = jnp.minimum(z, 0.0) - jnp.log1p(jnp.exp(-jnp.abs(z)))
    r_i = lax.broadcasted_iota(jnp.int32, (tm, tm), 0)
    c_i = lax.broadcasted_iota(jnp.int32, (tm, tm), 1)
    tri = jnp.where(r_i >= c_i, 1.0, 0.0).astype(F32)
    cum = jnp.dot(tri, logf, precision=lax.Precision.HIGHEST,
                  preferred_element_type=F32) + carry_ref[...]
    carry_ref[...] = cum[tm - 1:tm, :]
    cum2 = cum * LOG2E
    hi = cum2.astype(BF16).astype(F32)
    r1 = cum2 - hi
    mid = r1.astype(BF16).astype(F32)
    lo = r1 - mid
    lane = lax.broadcasted_iota(jnp.int32, (tm, LANES), 1)
    for h in range(C_HEADS):
        c = h * HEAD_DIM
        a = h * AUG
        hi_h, mid_h, lo_h = hi[:, h:h + 1], mid[:, h:h + 1], lo[:, h:h + 1]
        aug_q = jnp.where(lane == 0, hi_h, jnp.where(lane == 1, mid_h, jnp.where(
            lane == 2, lo_h, jnp.where(lane < 6, 1.0, 0.0))))
        aug_k = jnp.where(lane < 3, 1.0, jnp.where(lane == 3, -hi_h, jnp.where(
            lane == 4, -mid_h, jnp.where(lane == 5, -lo_h, 0.0))))
        qc_ref[:, a:a + HEAD_DIM] = head_norm(OFF_QC + c, 4, qscale).astype(BF16)
        qc_ref[:, a + HEAD_DIM:a + AUG] = aug_q.astype(BF16)
        kc_ref[h, :HEAD_DIM, :] = head_norm(OFF_KC + c, 5, 1.0).T.astype(BF16)
        kc_ref[h, HEAD_DIM:, :] = aug_k.T.astype(BF16)
    vc_ref[...] = p_ref[:, OFF_VC:OFF_VC + C_HEADS * HEAD_DIM]

    xq = p_ref[:, OFF_QLAT:OFF_QLAT + D_Q_LORA].astype(F32)
    ms = jnp.mean(xq * xq, axis=-1, keepdims=True)
    ql_ref[...] = (xq * lax.rsqrt(ms + EPS) * gql_ref[...]).astype(BF16)
    xkv = p_ref[:, OFF_KVLAT:OFF_KVLAT + D_KV_LORA].astype(F32)
    ms = jnp.mean(xkv * xkv, axis=-1, keepdims=True)
    kvl_ref[...] = (xkv * lax.rsqrt(ms + EPS) * gkvl_ref[...]).astype(BF16)
    kr_ref[...] = p_ref[:, OFF_KROPE:OFF_KROPE + LANES].astype(F32)


def _kt_shape_spec(n_heads, s, tm):
    blk = min(FLASH_BLK, s)
    per = blk // tm
    shape = jax.ShapeDtypeStruct((n_heads, s // blk, AUG, blk), BF16)
    spec = pl.BlockSpec((n_heads, None, AUG, tm), lambda i: (0, i // per, 0, i % per))
    return shape, spec


def _prep(p, gains, gql, gkvl, fb, cos_b, sin_b):
    s = p.shape[0]
    tm = min(PREP_ROWS, s)

    def row(w):
        return pl.BlockSpec((tm, w), lambda i: (i, 0))

    def const(r, w):
        return pl.BlockSpec((r, w), lambda i: (0, 0))

    widths = [1024, 1024, 1024, 1024, 256, 256, C_HEADS * AUG, None, 1024,
              D_Q_LORA, D_KV_LORA]
    out_shape = [jax.ShapeDtypeStruct((s, w), BF16) for w in widths if w]
    out_specs = [row(w) for w in widths if w]
    kt_shape, kt_spec = _kt_shape_spec(C_HEADS, s, tm)
    out_shape.insert(7, kt_shape)
    out_specs.insert(7, kt_spec)
    out_shape.append(jax.ShapeDtypeStruct((s, LANES), F32))
    out_specs.append(row(LANES))
    return pl.pallas_call(
        _prep_kernel, grid=(s // tm,),
        in_specs=[row(N_P), const(8, LANES), const(1, D_Q_LORA), const(1, D_KV_LORA),
                  const(1, LANES), row(LANES), row(LANES)],
        out_specs=out_specs, out_shape=out_shape,
        scratch_shapes=[pltpu.VMEM((1, LANES), F32)],
        compiler_params=_cparams(("arbitrary",), 48), name="prep",
    )(p, gains, gql, gkvl, fb, cos_b, sin_b)


def _rope_d(y, cd, sd):
    lane = lax.broadcasted_iota(jnp.int32, y.shape, 1)
    half = D_ROPE // 2
    partner = jnp.where(lane < half, pltpu.roll(y, LANES - half, 1), pltpu.roll(y, half, 1))
    return y * cd + partner * sd


def _prep_d_kernel(ql_ref, kvl_ref, kr_ref, wq_ref, wkv_ref, gq_ref, gk_ref, cd_ref, sd_ref,
                   qd_ref, kd_ref, vd_ref):
    qscale = D_QK ** -0.5 * LOG2E
    cd, sd = cd_ref[...], sd_ref[...]
    q = _dot(ql_ref[...], wq_ref[...])
    kv = _dot(kvl_ref[...], wkv_ref[...])
    kr = kr_ref[...]
    ss_r = jnp.sum(kr * kr, axis=-1, keepdims=True)
    gq = gq_ref[...]
    gk = gk_ref[...]
    for h in range(D_HEADS):
        a = h * AUG
        qn, qr = q[:, a:a + HEAD_DIM], q[:, a + HEAD_DIM:a + AUG]
        ms = (jnp.sum(qn * qn, axis=-1, keepdims=True)
              + jnp.sum(qr * qr, axis=-1, keepdims=True)) * (1.0 / D_QK)
        r = lax.rsqrt(ms + EPS) * qscale
        qd_ref[:, a:a + HEAD_DIM] = (qn * r * gq[:, :HEAD_DIM]).astype(BF16)
        qd_ref[:, a + HEAD_DIM:a + AUG] = _rope_d(qr * r * gq[:, HEAD_DIM:], cd, sd).astype(BF16)
        kn = kv[:, h * D_NOPE:(h + 1) * D_NOPE]
        ms = (jnp.sum(kn * kn, axis=-1, keepdims=True) + ss_r) * (1.0 / D_QK)
        r = lax.rsqrt(ms + EPS)
        kd_ref[h, :HEAD_DIM, :] = (kn * r * gk[:, :HEAD_DIM]).T.astype(BF16)
        kd_ref[h, HEAD_DIM:, :] = _rope_d(kr * r * gk[:, HEAD_DIM:], cd, sd).T.astype(BF16)
    vd_ref[...] = kv[:, D_HEADS * D_NOPE:].astype(BF16)


def _prep_d(ql, kvl, kr, wq, wkv, gq, gk, cos_d, sin_d):
    s = ql.shape[0]
    tm = min(PREP_ROWS, s)

    def row(w):
        return pl.BlockSpec((tm, w), lambda i: (i, 0))

    def const(r, w):
        return pl.BlockSpec((r, w), lambda i: (0, 0))

    kt_shape, kt_spec = _kt_shape_spec(D_HEADS, s, tm)
    return pl.pallas_call(
        _prep_d_kernel, grid=(s // tm,),
        in_specs=[row(D_Q_LORA), row(D_KV_LORA), row(LANES),
                  const(D_Q_LORA, D_HEADS * AUG), const(D_KV_LORA, 2 * D_HEADS * D_NOPE),
                  const(1, AUG), const(1, AUG), row(LANES), row(LANES)],
        out_specs=[row(D_HEADS * AUG), kt_spec, row(D_HEADS * D_V)],
        out_shape=[jax.ShapeDtypeStruct((s, D_HEADS * AUG), BF16), kt_shape,
                   jax.ShapeDtypeStruct((s, D_HEADS * D_V), BF16)],
        compiler_params=_cparams(("parallel",), 40), name="prep_d",
    )(ql, kvl, kr, wq, wkv, gq, gk, cos_d, sin_d)


def _flash_kernel(q_ref, kt_ref, v_ref, o_ref, m_ref, l_ref, acc_ref, *, blk, unit):
    i = pl.program_id(1)
    m_ref[...] = jnp.full_like(m_ref, NEG_INF)
    l_ref[...] = jnp.zeros_like(l_ref)
    acc_ref[...] = jnp.zeros_like(acc_ref)
    rc = FLASH_ROW_CHUNK

    def block(j, row0, nrows, nkeys, masked):
        start = pl.multiple_of(j * blk, blk)
        for g in range(FLASH_HEADS):
            s = _dot(q_ref[row0:row0 + nrows, g * AUG:(g + 1) * AUG], kt_ref[g, j, :, :nkeys])
            v = v_ref[pl.ds(start, nkeys), g * LANES:(g + 1) * LANES]
            ps, alphas = [], []
            for r in range(nrows // rc):
                rows = slice(row0 + r * rc, row0 + (r + 1) * rc)
                sc = s[r * rc:(r + 1) * rc, :]
                if masked:
                    qi = lax.broadcasted_iota(jnp.int32, sc.shape, 0) + (row0 + r * rc)
                    ki = lax.broadcasted_iota(jnp.int32, sc.shape, 1)
                    if unit > 1:
                        shift = unit.bit_length() - 1
                        qi, ki = qi >> shift, ki >> shift
                    sc = jnp.where(ki <= qi, sc, NEG_INF)
                m_prev = m_ref[g, rows, :]
                m_new = jnp.maximum(m_prev, jnp.max(sc, axis=-1, keepdims=True))
                alpha = jnp.exp2(m_prev - m_new)
                pc = [jnp.exp2(sc[:, c * LANES:(c + 1) * LANES] - m_new)
                      for c in range(nkeys // LANES)]
                lsum = pc[0]
                for c in range(1, len(pc)):
                    lsum = lsum + pc[c]
                l_ref[g, rows, :] = alpha * l_ref[g, rows, :] + lsum
                m_ref[g, rows, :] = m_new
                ps.append(jnp.concatenate(pc, axis=1).astype(BF16))
                alphas.append(alpha)
            p = jnp.concatenate(ps, axis=0)
            alpha = jnp.concatenate(alphas, axis=0)
            acc_ref[g, row0:row0 + nrows, :] = (alpha * acc_ref[g, row0:row0 + nrows, :]
                                                + _dot(p, v))

    def body(j, carry):
        block(j, 0, blk, blk, False)
        return carry

    lax.fori_loop(0, i, body, 0)
    half = blk // 2
    if half % max(rc, unit, LANES) == 0:
        block(i, 0, half, half, True)
        block(i, half, half, blk, True)
    else:
        block(i, 0, blk, blk, True)
    for g in range(FLASH_HEADS):
        l = jnp.sum(l_ref[g], axis=-1, keepdims=True)
        o_ref[:, g * LANES:(g + 1) * LANES] = (acc_ref[g] / l).astype(o_ref.dtype)


def _flash(q, kt, v, n_heads, unit):
    s = q.shape[0]
    blk = kt.shape[3]
    dv = v.shape[1] // n_heads
    g = FLASH_HEADS
    assert dv == LANES and n_heads % g == 0
    once = pl.Buffered(1)
    return pl.pallas_call(
        functools.partial(_flash_kernel, blk=blk, unit=unit),
        grid=(n_heads // g, s // blk),
        in_specs=[pl.BlockSpec((blk, g * AUG), lambda h, i: (i, h)),
                  pl.BlockSpec((g, s // blk, AUG, blk), lambda h, i: (h, 0, 0, 0), pipeline_mode=once),
                  pl.BlockSpec((s, g * dv), lambda h, i: (0, h), pipeline_mode=once)],
        out_specs=pl.BlockSpec((blk, g * dv), lambda h, i: (i, h)),
        out_shape=jax.ShapeDtypeStruct((s, n_heads * dv), BF16),
        scratch_shapes=[pltpu.VMEM((g, blk, LANES), F32), pltpu.VMEM((g, blk, LANES), F32),
                        pltpu.VMEM((g, blk, dv), F32)],
        compiler_params=_cparams(("parallel", "arbitrary"), 56), name=f"flash_u{unit}",
    )(q, kt, v)


A_PREV_BLOCKS = A_PREV_CHUNKS * CHUNK // BAND_Q


def _attn_a_kernel(q_ref, k0_ref, k1_ref, k2_ref, v0_ref, v1_ref, v2_ref, t_ref, mk_ref, o_ref):
    mask = mk_ref[...]
    for h in range(A_HEADS):
        cols = slice(h * HEAD_DIM, (h + 1) * HEAD_DIM)
        k = jnp.concatenate([k0_ref[:, cols], k1_ref[:, cols], k2_ref[:, cols]], axis=0)
        v = jnp.concatenate([v0_ref[:, cols], v1_ref[:, cols], v2_ref[:, cols]], axis=0)
        s = _dot_nt(q_ref[:, cols], k) + t_ref[h] + mask
        m = jnp.max(s, axis=-1, keepdims=True)
        p = jnp.exp2(s - m)
        l = jnp.sum(p, axis=-1, keepdims=True)
        o_ref[:, cols] = (_dot(p.astype(BF16), v) / l).astype(o_ref.dtype)


def _attn_a(q, k, v, table, mask):
    s, w = q.shape
    row = pl.BlockSpec((BAND_Q, w), lambda i: (i, 0))
    prev = [pl.BlockSpec((BAND_Q, w), functools.partial(
        lambda i, back: (jnp.maximum(i - back, 0), 0), back=back))
        for back in range(A_PREV_BLOCKS, -1, -1)]
    return pl.pallas_call(
        _attn_a_kernel, grid=(s // BAND_Q,),
        in_specs=[row] + prev + prev + [
            pl.BlockSpec((A_HEADS, BAND_Q, A_WIN), lambda i: (0, 0, 0)),
            pl.BlockSpec((None, BAND_Q, A_WIN), lambda i: (jnp.minimum(i, A_PREV_BLOCKS), 0, 0))],
        out_specs=row,
        out_shape=jax.ShapeDtypeStruct((s, w), BF16),
        compiler_params=_cparams(("arbitrary",), 40), name="attn_a",
    )(q, k, k, k, v, v, v, table, mask)


def _a_mask():
    b = np.arange(A_PREV_BLOCKS + 1)[:, None, None]
    q = np.arange(BAND_Q)[None, :, None]
    w = np.arange(A_WIN)[None, None, :]
    dchunk = (q + A_PREV_BLOCKS * BAND_Q) // CHUNK - w // CHUNK
    valid = (dchunk >= 0) & (dchunk <= A_PREV_CHUNKS) & (w >= (A_PREV_BLOCKS - b) * BAND_Q)
    return jnp.asarray(np.where(valid, 0.0, NEG_INF), F32)


def _a_table(rel_bias):
    n, m = BAND_Q, A_WIN
    u = np.arange(n + m)
    dist = A_PREV_BLOCKS * BAND_Q + (n - 1) - u
    idx = np.clip(dist, -REL_CLIP, REL_CLIP) + REL_CLIP
    z = rel_bias.astype(F32).T[:, jnp.asarray(idx)] * LOG2E
    skew = jnp.broadcast_to(z[:, None, :], (A_HEADS, n, n + m))
    skew = skew.reshape(A_HEADS, n * (n + m))[:, :n * (n + m - 1)]
    return skew.reshape(A_HEADS, n, n + m - 1)[:, :, n - 1:n - 1 + m]


B_KROWS = B_PREV_CHUNKS * CHUNK


def _attn_b_kernel(sink_ref, q_ref, k0_ref, k1_ref, k2_ref, v0_ref, v1_ref, v2_ref, o_ref):
    i = pl.program_id(0)
    group = B_Q_HEADS // B_KV_HEADS
    rows = group * BAND_Q
    shift = CHUNK.bit_length() - 1
    t = ((lax.broadcasted_iota(jnp.int32, (rows, B_WIN), 0) & (BAND_Q - 1)) + B_KROWS) >> shift
    w = lax.broadcasted_iota(jnp.int32, (rows, B_WIN), 1)
    first = jnp.where(i == 0, B_KROWS, 0)
    valid = jnp.where(w >= first, jnp.abs(t - (w >> shift) - 1), 2) <= 1
    head = lax.broadcasted_iota(jnp.int32, (rows, 1), 0) >> (BAND_Q.bit_length() - 1)
    for g in range(B_KV_HEADS):
        kc = slice(g * HEAD_DIM, (g + 1) * HEAD_DIM)
        k = jnp.concatenate([k0_ref[:, kc], k1_ref[:, kc], k2_ref[:, kc]], axis=0)
        v = jnp.concatenate([v0_ref[:, kc], v1_ref[:, kc], v2_ref[:, kc]], axis=0)
        q = jnp.concatenate([q_ref[:, (g * group + hi) * HEAD_DIM:(g * group + hi + 1) * HEAD_DIM]
                             for hi in range(group)], axis=0)
        s = jnp.where(valid, _dot_nt(q, k), NEG_INF)
        sink = jnp.zeros((rows, 1), F32)
        for hi in range(group):
            sink = jnp.where(head == hi, sink_ref[g * group + hi] * LOG2E, sink)
        m = jnp.maximum(jnp.max(s, axis=-1, keepdims=True), sink)
        p = jnp.exp2(s - m)
        l = jnp.sum(p, axis=-1, keepdims=True) + jnp.exp2(sink - m)
        o = (_dot(p.astype(BF16), v) / l).astype(o_ref.dtype)
        for hi in range(group):
            c = (g * group + hi) * HEAD_DIM
            o_ref[:, c:c + HEAD_DIM] = o[hi * BAND_Q:(hi + 1) * BAND_Q, :]


def _attn_b(q, k, v, sinks):
    s, w = q.shape
    kw = k.shape[1]
    per = BAND_Q // B_KROWS
    row = pl.BlockSpec((BAND_Q, w), lambda i: (i, 0))
    kv = [pl.BlockSpec((B_KROWS, kw), functools.partial(
        lambda i, off: (jnp.maximum(i * per + off, 0), 0), off=off)) for off in (-1, 0, 1)]
    return pl.pallas_call(
        _attn_b_kernel, grid=(s // BAND_Q,),
        in_specs=[pl.BlockSpec(memory_space=pltpu.SMEM), row] + kv + kv,
        out_specs=row,
        out_shape=jax.ShapeDtypeStruct((s, w), BF16),
        compiler_params=_cparams(("arbitrary",), 40), name="attn_b",
    )(sinks, q, k, k, k, v, v, v)


def _merge_kernel(h_ref, oa_ref, ob_ref, oc_ref, od_ref, wg_ref, bg_ref, wb_ref, out_ref):
    h = h_ref[...]
    acc = None
    for n, o_ref in enumerate((oa_ref, ob_ref, oc_ref, od_ref)):
        gate = jax.nn.sigmoid(_dot(h, wg_ref[n]) + bg_ref[n])
        val = gate * _dot(o_ref[...], wb_ref[n].astype(BF16))
        acc = val if acc is None else acc + val
    out_ref[...] = acc.astype(out_ref.dtype)


def _merge(h, outs, wg, bg, wb, l):
    s, d = h.shape
    tm, tn = min(1024, s), min(256, d)
    once = pl.Buffered(1)
    o_spec = pl.BlockSpec((tm, BRANCH_W), lambda i, j: (i, 0), pipeline_mode=once)
    return pl.pallas_call(
        _merge_kernel, grid=(s // tm, d // tn),
        in_specs=[pl.BlockSpec((tm, d), lambda i, j: (i, 0), pipeline_mode=once),
                  o_spec, o_spec, o_spec, o_spec,
                  pl.BlockSpec((None, N_BRANCH, d, tn), lambda i, j: (l, 0, 0, j)),
                  pl.BlockSpec((N_BRANCH, 1, tn), lambda i, j: (0, 0, j)),
                  pl.BlockSpec((None, N_BRANCH, BRANCH_W, tn), lambda i, j: (l, 0, 0, j))],
        out_specs=pl.BlockSpec((tm, tn), lambda i, j: (i, j)),
        out_shape=jax.ShapeDtypeStruct((s, d), BF16),
        compiler_params=_cparams(("parallel", "arbitrary"), 56), name="merge",
    )(h, *outs, wg, bg, wb)


def _resid_kernel(a_ref, w_ref, x_ref, g_ref, o_ref):
    o_ref[...] = x_ref[...] + g_ref[...] * _dot(a_ref[...], w_ref[...].astype(BF16))


def _resid_moe_kernel(a_ref, w_ref, c_ref, bd_ref, x_ref, g_ref, o_ref):
    y = _dot(a_ref[...], w_ref[...].astype(BF16)) + _dot(c_ref[...].astype(BF16), bd_ref[...])
    o_ref[...] = x_ref[...] + g_ref[...] * y


def _resid(a, w, l, x, gate, comb=None, b_dn=None):
    s, k = a.shape
    d = w.shape[2]
    tm, tn = min(1024, s), min(512, d)
    a_spec = pl.BlockSpec((tm, k), lambda i, j: (i, 0))
    w_spec = pl.BlockSpec((None, k, tn), lambda i, j: (l, 0, j))
    x_spec = pl.BlockSpec((tm, tn), lambda i, j: (i, j))
    g_spec = pl.BlockSpec((1, tn), lambda i, j: (0, j))
    if comb is None:
        kern, ins = _resid_kernel, (a, w, x, gate)
        in_specs = [a_spec, w_spec, x_spec, g_spec]
    else:
        kern, ins = _resid_moe_kernel, (a, w, comb, b_dn, x, gate)
        in_specs = [a_spec, w_spec, pl.BlockSpec((tm, LANES), lambda i, j: (i, 0)),
                    pl.BlockSpec((LANES, tn), lambda i, j: (0, j)), x_spec, g_spec]
    return pl.pallas_call(
        kern, grid=(s // tm, d // tn), in_specs=in_specs, out_specs=x_spec,
        out_shape=jax.ShapeDtypeStruct((s, d), F32),
        compiler_params=_cparams(("parallel", "parallel"), 48), name="resid",
    )(*ins)


def _moe_up_kernel(h_ref, w_ref, b_ref, c_ref, o_ref, *, eb):
    j = pl.program_id(1)
    h = h_ref[...]
    comb = c_ref[...]
    lane = lax.broadcasted_iota(jnp.int32, comb.shape, 1)
    for e in range(eb):
        gu = _dot(h, w_ref[e].astype(BF16)) + b_ref[e]
        glu = jnp.minimum(gu[:, :D_EXPERT], SWIGLU_LIMIT)
        lin = jnp.clip(gu[:, D_EXPERT:], -SWIGLU_LIMIT, SWIGLU_LIMIT)
        act = glu * jax.nn.sigmoid(SWIGLU_ALPHA * glu) * (lin + 1.0)
        ce = jnp.sum(jnp.where(lane == j * eb + e, comb, 0.0), axis=-1, keepdims=True)
        o_ref[:, e * D_EXPERT:(e + 1) * D_EXPERT] = (act * ce).astype(o_ref.dtype)


def _moe_up(h, w_gu, l, b_gu, comb):
    s, d = h.shape
    tm, eb = min(1024, s), 2
    return pl.pallas_call(
        functools.partial(_moe_up_kernel, eb=eb), grid=(s // tm, N_EXPERTS // eb),
        in_specs=[pl.BlockSpec((tm, d), lambda i, j: (i, 0)),
                  pl.BlockSpec((None, eb, d, 2 * D_EXPERT), lambda i, j: (l, j, 0, 0)),
                  pl.BlockSpec((eb, 1, 2 * D_EXPERT), lambda i, j: (j, 0, 0)),
                  pl.BlockSpec((tm, LANES), lambda i, j: (i, 0))],
        out_specs=pl.BlockSpec((tm, eb * D_EXPERT), lambda i, j: (i, j)),
        out_shape=jax.ShapeDtypeStruct((s, N_EXPERTS * D_EXPERT), BF16),
        compiler_params=_cparams(("parallel", "parallel"), 48), name="moe_up",
    )(h, w_gu, b_gu, comb)


def _w_in_tail(w):
    d = w.shape[0]
    zeros = lambda n: jnp.zeros((d, n), BF16)
    return jnp.concatenate(
        [w[:, C_HEADS:].astype(BF16), zeros(LANES - D_ROPE), w[:, :C_HEADS].astype(BF16),
         zeros(N_P - OFF_F - C_HEADS)], axis=1)


def _reorder_w_q_b(w):
    r = w.shape[0]
    w = w.reshape(r, D_HEADS, D_QK)
    w = jnp.pad(w, ((0, 0), (0, 0), (0, AUG - D_QK)))
    return w.reshape(r, D_HEADS * AUG)


def _reorder_w_kv_b(w):
    r = w.shape[0]
    w = w.reshape(r, D_HEADS, D_NOPE + D_V)
    return jnp.concatenate([w[:, :, :D_NOPE].reshape(r, -1), w[:, :, D_NOPE:].reshape(r, -1)], axis=1)


def _pad_lanes(v, width, value=0.0):
    return jnp.pad(v, (0, width - v.shape[0]), constant_values=value).reshape(1, width)


def kernel(x, c, positions, ada_w, ada_b, ada_layer, norm1_g, norm2_g, w_in, a_q_norm, a_k_norm, a_rel_bias, b_q_norm, b_k_norm, b_sinks, c_q_norm, c_k_norm, c_f_bias, d_q_a_norm, d_w_q_b, d_kv_a_norm, d_w_kv_b, d_q_norm, d_k_norm, w_branch, w_gate, b_gate, w_out, router_w, router_b, w_gu, b_gu, w_dn, b_dn):
    batch, s, d = x.shape
    assert batch == 1, "kernels are written for a single sequence"
    depth = w_in.shape[0]
    xs = x.reshape(s, d)

    base_mod = _adaln(c.reshape(d, 1), ada_w, ada_b).reshape(6, d)
    cos_b, sin_b, cos_d, sin_d = _rope_tables(positions.reshape(s, 1).astype(F32))
    a_mask = _a_mask()
    w_gate_bf16 = w_gate.astype(BF16)

    for l in range(depth):
        mod = base_mod + ada_layer[l]
        shift1, scale1, gate1, shift2, scale2, gate2 = (mod[j:j + 1] for j in range(6))

        h = _norm(xs, norm1_g[l].reshape(1, d), scale1, shift1)
        p = _in_proj(h, w_in[l, :, :OFF_QLAT].astype(BF16), _w_in_tail(w_in[l, :, OFF_QLAT:]))
        gains = jnp.stack([a_q_norm[l], a_k_norm[l], b_q_norm[l], b_k_norm[l],
                           c_q_norm[l], c_k_norm[l], jnp.zeros_like(a_q_norm[l]),
                           jnp.zeros_like(a_q_norm[l])])
        (qa, ka, va, qb, kb, vb, qc, kc, vc, ql, kvl, kr) = _prep(
            p, gains, d_q_a_norm[l].reshape(1, -1), d_kv_a_norm[l].reshape(1, -1),
            _pad_lanes(c_f_bias[l], LANES), cos_b, sin_b)
        qd, kd, vd = _prep_d(
            ql, kvl, kr, _reorder_w_q_b(d_w_q_b[l]).astype(BF16),
            _reorder_w_kv_b(d_w_kv_b[l]).astype(BF16),
            _pad_lanes(d_q_norm[l], AUG), _pad_lanes(d_k_norm[l], AUG), cos_d, sin_d)

        o_a = _attn_a(qa, ka, va, _a_table(a_rel_bias[l]), a_mask)
        o_b = _attn_b(qb, kb, vb, b_sinks[l].astype(F32))
        o_c = _flash(qc, kc, vc, C_HEADS, 1)
        o_d = _flash(qd, kd, vd, D_HEADS, CHUNK)

        merged = _merge(h, (o_a, o_b, o_c, o_d), w_gate_bf16,
                        b_gate[l].reshape(N_BRANCH, 1, d), w_branch, l)
        xs = _resid(merged, w_out, l, xs, gate1)

        rw = jnp.pad(router_w[l], ((0, 0), (0, LANES - N_EXPERTS)))
        rb = _pad_lanes(router_b[l].astype(F32), LANES, NEG_INF)
        h2, comb = _norm(xs, norm2_g[l].reshape(1, d), scale2, shift2, router=(rw, rb))
        act = _moe_up(h2, w_gu, l, b_gu[l].reshape(N_EXPERTS, 1, 2 * D_EXPERT), comb)
        bdn = jnp.pad(b_dn[l], ((0, LANES - N_EXPERTS), (0, 0))).astype(BF16)
        xs = _resid(act, w_dn.reshape(depth, N_EXPERTS * D_EXPERT, d), l, xs, gate2,
                    comb=comb, b_dn=bdn)
    return xs.reshape(batch, s, d)
```

```python
import functools
import math

import numpy as np
import jax
import jax.numpy as jnp
from jax import lax
from jax.experimental import pallas as pl
from jax.experimental.pallas import tpu as pltpu

F32 = jnp.float32
BF16 = jnp.bfloat16

CHUNK = 64
HEAD_DIM = 128
N_BRANCH = 4
BRANCH_W = 1024
ROPE_THETA = 10000.0
EPS = 1e-6
NEG_INF = -1e30
A_HEADS = 8
A_PREV_CHUNKS = 8
REL_CLIP = 256
B_Q_HEADS = 8
B_KV_HEADS = 2
B_PREV_CHUNKS = 2
C_HEADS = 8
D_HEADS = 8
D_Q_LORA = 896
D_KV_LORA = 256
D_NOPE = 128
D_ROPE = 64
D_V = 128
D_QK = D_NOPE + D_ROPE
N_EXPERTS = 32
TOP_K = 4
D_EXPERT = 128
SWIGLU_LIMIT = 7.0
SWIGLU_ALPHA = 1.702

LANES = 128
V7X_VMEM_BYTES = 64 * 1024 * 1024

LOG2E = math.log2(math.e)

OFF_QA, OFF_KA, OFF_VA = 0, 1024, 2048
OFF_QB, OFF_KB, OFF_VB = 3072, 4096, 4352
OFF_QC, OFF_KC, OFF_VC = 4608, 5632, 6656
OFF_QLAT = 7680
OFF_KVLAT = OFF_QLAT + D_Q_LORA
OFF_KROPE = OFF_KVLAT + D_KV_LORA
OFF_F = OFF_KROPE + LANES
N_P = 9216

AUG = 2 * HEAD_DIM
FLASH_BLK = 1024
FLASH_ROW_CHUNK = 32
FLASH_HEADS = 2
PREP_ROWS = 256
BAND_Q = 256
A_WIN = BAND_Q + A_PREV_CHUNKS * CHUNK
B_WIN = BAND_Q + B_PREV_CHUNKS * CHUNK


def _cparams(sems, vmem_mb):
    return pltpu.CompilerParams(dimension_semantics=sems,
                                vmem_limit_bytes=vmem_mb * 1024 * 1024)


def _dot(a, b):
    return jnp.dot(a, b, preferred_element_type=F32)


def _dot_nt(a, b):
    return lax.dot_general(a, b, (((1,), (1,)), ((), ())), preferred_element_type=F32)


def _adaln_kernel(c_ref, w_ref, b_ref, o_ref):
    c = c_ref[...]
    s = c * jax.nn.sigmoid(c)
    o_ref[...] = jnp.sum(w_ref[...] * s, axis=0, keepdims=True) + b_ref[...]


def _adaln(c_col, ada_w, ada_b):
    d, n = ada_w.shape
    tn = min(512, n)
    return pl.pallas_call(
        _adaln_kernel,
        grid=(n // tn,),
        in_specs=[pl.BlockSpec((d, 1), lambda j: (0, 0)),
                  pl.BlockSpec((d, tn), lambda j: (0, j)),
                  pl.BlockSpec((1, tn), lambda j: (0, j))],
        out_specs=pl.BlockSpec((1, tn), lambda j: (0, j)),
        out_shape=jax.ShapeDtypeStruct((1, n), F32),
        compiler_params=_cparams(("parallel",), 40),
        name="adaln",
    )(c_col, ada_w, ada_b.reshape(1, n))


def _rope_table_kernel(pos_ref, fb_ref, fd_ref, cb_ref, sb_ref, cd_ref, sd_ref):
    pos = pos_ref[...]
    lane = lax.broadcasted_iota(jnp.int32, (1, LANES), 1)
    ab = pos * fb_ref[...]
    sb = jnp.sin(ab)
    cb_ref[...] = jnp.cos(ab)
    sb_ref[...] = jnp.where(lane < HEAD_DIM // 2, -sb, sb)
    ad = pos * fd_ref[...]
    sd = jnp.sin(ad)
    cd_ref[...] = jnp.where(lane < D_ROPE, jnp.cos(ad), 1.0)
    sd_ref[...] = jnp.where(lane < D_ROPE // 2, -sd, jnp.where(lane < D_ROPE, sd, 0.0))


def _rope_tables(pos_col):
    s = pos_col.shape[0]
    tm = min(512, s)
    lane = np.arange(LANES)
    half_b = HEAD_DIM // 2
    fb = ROPE_THETA ** (-(lane % half_b).astype(np.float64) / half_b)
    half_d = D_ROPE // 2
    fd = np.where(lane < D_ROPE, ROPE_THETA ** (-(lane % half_d).astype(np.float64) / half_d), 0.0)
    fb = jnp.asarray(fb, F32).reshape(1, LANES)
    fd = jnp.asarray(fd, F32).reshape(1, LANES)
    tab = jax.ShapeDtypeStruct((s, LANES), F32)
    row = pl.BlockSpec((tm, LANES), lambda i: (i, 0))
    const = pl.BlockSpec((1, LANES), lambda i: (0, 0))
    return pl.pallas_call(
        _rope_table_kernel,
        grid=(s // tm,),
        in_specs=[pl.BlockSpec((tm, 1), lambda i: (i, 0)), const, const],
        out_specs=[row, row, row, row],
        out_shape=[tab, tab, tab, tab],
        compiler_params=_cparams(("parallel",), 32),
        name="rope_tables",
    )(pos_col, fb, fd)


def _mod_norm(x, g, sc, sh):
    ms = jnp.mean(x * x, axis=-1, keepdims=True)
    y = x * lax.rsqrt(ms + EPS)
    return (y * g) * (1.0 + sc) + sh


def _norm_kernel(x_ref, g_ref, sc_ref, sh_ref, h_ref):
    h_ref[...] = _mod_norm(x_ref[...], g_ref[...], sc_ref[...], sh_ref[...]).astype(h_ref.dtype)


def _norm_router_kernel(x_ref, g_ref, sc_ref, sh_ref, rwh_ref, rwl_ref, rb_ref, h_ref, comb_ref):
    h = _mod_norm(x_ref[...], g_ref[...], sc_ref[...], sh_ref[...])
    h_hi = h.astype(BF16)
    h_ref[...] = h_hi
    h_lo = (h - h_hi.astype(F32)).astype(BF16)
    rw_hi = rwh_ref[...]
    logits = (_dot(h_hi, rw_hi) + _dot(h_lo, rw_hi) + _dot(h_hi, rwl_ref[...])
              + rb_ref[...])
    lane = lax.broadcasted_iota(jnp.int32, logits.shape, 1).astype(F32)
    work = logits
    vals, sels = [], []
    for _ in range(TOP_K):
        m = jnp.max(work, axis=-1, keepdims=True)
        idx = jnp.min(jnp.where(work == m, lane, float(LANES)), axis=-1, keepdims=True)
        sel = lane == idx
        vals.append(m)
        sels.append(sel)
        work = jnp.where(sel, -3.0e38, work)
    es = [jnp.exp(v - vals[0]) for v in vals]
    inv = 1.0 / (es[0] + es[1] + es[2] + es[3])
    comb = jnp.zeros_like(logits)
    for e, sel in zip(es, sels):
        comb = comb + jnp.where(sel, e * inv, 0.0)
    comb_ref[...] = comb


def _norm(x, g, sc, sh, router=None):
    s, d = x.shape
    tm = min(256, s)
    row = pl.BlockSpec((tm, d), lambda i: (i, 0))
    vec = pl.BlockSpec((1, d), lambda i: (0, 0))
    if router is None:
        return pl.pallas_call(
            _norm_kernel, grid=(s // tm,),
            in_specs=[row, vec, vec, vec], out_specs=row,
            out_shape=jax.ShapeDtypeStruct((s, d), BF16),
            compiler_params=_cparams(("parallel",), 32), name="norm",
        )(x, g, sc, sh)
    rw, rb = router
    rw_hi = rw.astype(BF16)
    rw_lo = (rw - rw_hi.astype(F32)).astype(BF16)
    rw_spec = pl.BlockSpec((d, LANES), lambda i: (0, 0))
    return pl.pallas_call(
        _norm_router_kernel, grid=(s // tm,),
        in_specs=[row, vec, vec, vec, rw_spec, rw_spec,
                  pl.BlockSpec((1, LANES), lambda i: (0, 0))],
        out_specs=[row, pl.BlockSpec((tm, LANES), lambda i: (i, 0))],
        out_shape=[jax.ShapeDtypeStruct((s, d), BF16), jax.ShapeDtypeStruct((s, LANES), F32)],
        compiler_params=_cparams(("parallel",), 40), name="norm_router",
    )(x, g, sc, sh, rw_hi, rw_lo, rb)


IN_TN = 768
IN_MAIN_TILES = OFF_QLAT // IN_TN


def _in_proj_kernel(a_ref, w_ref, wt_ref, o_ref):
    j = pl.program_id(1)

    @pl.when(j < IN_MAIN_TILES)
    def _():
        o_ref[...] = _dot_nt(a_ref[...], w_ref[...]).astype(o_ref.dtype)

    @pl.when(j >= IN_MAIN_TILES)
    def _():
        o_ref[...] = _dot(a_ref[...], wt_ref[...]).astype(o_ref.dtype)


def _in_proj(a, w_main, w_tail):
    m, k = a.shape
    tm = min(1024, m)
    n_tail = w_tail.shape[1] // IN_TN
    last = IN_MAIN_TILES - 1
    return pl.pallas_call(
        _in_proj_kernel, grid=(m // tm, IN_MAIN_TILES + n_tail),
        in_specs=[pl.BlockSpec((tm, k), lambda i, j: (i, 0)),
                  pl.BlockSpec((IN_TN, k), lambda i, j: (jnp.minimum(j, last), 0)),
                  pl.BlockSpec((k, IN_TN), lambda i, j: (0, jnp.maximum(j - IN_MAIN_TILES, 0)))],
        out_specs=pl.BlockSpec((tm, IN_TN), lambda i, j: (i, j)),
        out_shape=jax.ShapeDtypeStruct((m, N_P), BF16),
        compiler_params=_cparams(("parallel", "arbitrary"), 56), name="in_proj",
    )(a, w_main, w_tail)


def _prep_kernel(p_ref, gains_ref, gql_ref, gkvl_ref, fb_ref, cb_ref, sb_ref,
                 qa_ref, ka_ref, va_ref, qb_ref, kb_ref, vb_ref,
                 qc_ref, kc_ref, vc_ref, ql_ref, kvl_ref, kr_ref, carry_ref):
    tm = p_ref.shape[0]
    qscale = HEAD_DIM ** -0.5 * LOG2E

    @pl.when(pl.program_id(0) == 0)
    def _():
        carry_ref[...] = jnp.zeros_like(carry_ref)

    def head_norm(off, g_row, scale):
        x = p_ref[:, off:off + HEAD_DIM].astype(F32)
        ms = jnp.mean(x * x, axis=-1, keepdims=True)
        return x * lax.rsqrt(ms + EPS) * (gains_ref[g_row:g_row + 1, :] * scale)

    cb, sb = cb_ref[...], sb_ref[...]

    def rope(y):
        return y * cb + pltpu.roll(y, HEAD_DIM // 2, 1) * sb

    for h in range(A_HEADS):
        c = h * HEAD_DIM
        qa_ref[:, c:c + HEAD_DIM] = head_norm(OFF_QA + c, 0, qscale).astype(BF16)
        ka_ref[:, c:c + HEAD_DIM] = head_norm(OFF_KA + c, 1, 1.0).astype(BF16)
    va_ref[...] = p_ref[:, OFF_VA:OFF_VA + A_HEADS * HEAD_DIM]

    for h in range(B_Q_HEADS):
        c = h * HEAD_DIM
        qb_ref[:, c:c + HEAD_DIM] = rope(head_norm(OFF_QB + c, 2, qscale)).astype(BF16)
    for h in range(B_KV_HEADS):
        c = h * HEAD_DIM
        kb_ref[:, c:c + HEAD_DIM] = rope(head_norm(OFF_KB + c, 3, 1.0)).astype(BF16)
    vb_ref[...] = p_ref[:, OFF_VB:OFF_VB + B_KV_HEADS * HEAD_DIM]

    z = p_ref[:, OFF_F:OFF_F + LANES].astype(F32) + fb_ref[...]
    logf = jnp.minimum(z, 0.0) - jnp.log1p(jnp.exp(-jnp.abs(z)))
    r_i = lax.broadcasted_iota(jnp.int32, (tm, tm), 0)
    c_i = lax.broadcasted_iota(jnp.int32, (tm, tm), 1)
    tri = jnp.where(r_i >= c_i, 1.0, 0.0).astype(F32)
    cum = jnp.dot(tri, logf, precision=lax.Precision.HIGHEST,
                  preferred_element_type=F32) + carry_ref[...]
    carry_ref[...] = cum[tm - 1:tm, :]
    cum2 = cum * LOG2E
    hi = cum2.astype(BF16).astype(F32)
    r1 = cum2 - hi
    mid = r1.astype(BF16).astype(F32)
    lo = r1 - mid
    lane = lax.broadcasted_iota(jnp.int32, (tm, LANES), 1)
    for h in range(C_HEADS):
        c = h * HEAD_DIM
        a = h * AUG
        hi_h, mid_h, lo_h = hi[:, h:h + 1], mid[:, h:h + 1], lo[:, h:h + 1]
        aug_q = jnp.where(lane == 0, hi_h, jnp.where(lane == 1, mid_h, jnp.where(
            lane == 2, lo_h, jnp.where(lane < 6, 1.0, 0.0))))
        aug_k = jnp.where(lane < 3, 1.0, jnp.where(lane == 3, -hi_h, jnp.where(
            lane == 4, -mid_h, jnp.where(lane == 5, -lo_h, 0.0))))
        qc_ref[:, a:a + HEAD_DIM] = head_norm(OFF_QC + c, 4, qscale).astype(BF16)
        qc_ref[:, a + HEAD_DIM:a + AUG] = aug_q.astype(BF16)
        kc_ref[h, :HEAD_DIM, :] = head_norm(OFF_KC + c, 5, 1.0).T.astype(BF16)
        kc_ref[h, HEAD_DIM:, :] = aug_k.T.astype(BF16)
    vc_ref[...] = p_ref[:, OFF_VC:OFF_VC + C_HEADS * HEAD_DIM]

    xq = p_ref[:, OFF_QLAT:OFF_QLAT + D_Q_LORA].astype(F32)
    ms = jnp.mean(xq * xq, axis=-1, keepdims=True)
    ql_ref[...] = (xq * lax.rsqrt(ms + EPS) * gql_ref[...]).astype(BF16)
    xkv = p_ref[:, OFF_KVLAT:OFF_KVLAT + D_KV_LORA].astype(F32)
    ms = jnp.mean(xkv * xkv, axis=-1, keepdims=True)
    kvl_ref[...] = (xkv * lax.rsqrt(ms + EPS) * gkvl_ref[...]).astype(BF16)
    kr_ref[...] = p_ref[:, OFF_KROPE:OFF_KROPE + LANES].astype(F32)


def _kt_shape_spec(n_heads, s, tm):
    blk = min(FLASH_BLK, s)
    per = blk // tm
    shape = jax.ShapeDtypeStruct((n_heads, s // blk, AUG, blk), BF16)
    spec = pl.BlockSpec((n_heads, None, AUG, tm), lambda i: (0, i // per, 0, i % per))
    return shape, spec


def _prep(p, gains, gql, gkvl, fb, cos_b, sin_b):
    s = p.shape[0]
    tm = min(PREP_ROWS, s)

    def row(w):
        return pl.BlockSpec((tm, w), lambda i: (i, 0))

    def const(r, w):
        return pl.BlockSpec((r, w), lambda i: (0, 0))

    widths = [1024, 1024, 1024, 1024, 256, 256, C_HEADS * AUG, None, 1024,
              D_Q_LORA, D_KV_LORA]
    out_shape = [jax.ShapeDtypeStruct((s, w), BF16) for w in widths if w]
    out_specs = [row(w) for w in widths if w]
    kt_shape, kt_spec = _kt_shape_spec(C_HEADS, s, tm)
    out_shape.insert(7, kt_shape)
    out_specs.insert(7, kt_spec)
    out_shape.append(jax.ShapeDtypeStruct((s, LANES), F32))
    out_specs.append(row(LANES))
    return pl.pallas_call(
        _prep_kernel, grid=(s // tm,),
        in_specs=[row(N_P), const(8, LANES), const(1, D_Q_LORA), const(1, D_KV_LORA),
                  const(1, LANES), row(LANES), row(LANES)],
        out_specs=out_specs, out_shape=out_shape,
        scratch_shapes=[pltpu.VMEM((1, LANES), F32)],
        compiler_params=_cparams(("arbitrary",), 48), name="prep",
    )(p, gains, gql, gkvl, fb, cos_b, sin_b)


def _rope_d(y, cd, sd):
    lane = lax.broadcasted_iota(jnp.int32, y.shape, 1)
    half = D_ROPE // 2
    partner = jnp.where(lane < half, pltpu.roll(y, LANES - half, 1), pltpu.roll(y, half, 1))
    return y * cd + partner * sd


def _prep_d_kernel(ql_ref, kvl_ref, kr_ref, wq_ref, wkv_ref, gq_ref, gk_ref, cd_ref, sd_ref,
                   qd_ref, kd_ref, vd_ref):
    qscale = D_QK ** -0.5 * LOG2E
    cd, sd = cd_ref[...], sd_ref[...]
    q = _dot(ql_ref[...], wq_ref[...])
    kv = _dot(kvl_ref[...], wkv_ref[...])
    kr = kr_ref[...]
    ss_r = jnp.sum(kr * kr, axis=-1, keepdims=True)
    gq = gq_ref[...]
    gk = gk_ref[...]
    for h in range(D_HEADS):
        a = h * AUG
        qn, qr = q[:, a:a + HEAD_DIM], q[:, a + HEAD_DIM:a + AUG]
        ms = (jnp.sum(qn * qn, axis=-1, keepdims=True)
              + jnp.sum(qr * qr, axis=-1, keepdims=True)) * (1.0 / D_QK)
        r = lax.rsqrt(ms + EPS) * qscale
        qd_ref[:, a:a + HEAD_DIM] = (qn * r * gq[:, :HEAD_DIM]).astype(BF16)
        qd_ref[:, a + HEAD_DIM:a + AUG] = _rope_d(qr * r * gq[:, HEAD_DIM:], cd, sd).astype(BF16)
        kn = kv[:, h * D_NOPE:(h + 1) * D_NOPE]
        ms = (jnp.sum(kn * kn, axis=-1, keepdims=True) + ss_r) * (1.0 / D_QK)
        r = lax.rsqrt(ms + EPS)
        kd_ref[h, :HEAD_DIM, :] = (kn * r * gk[:, :HEAD_DIM]).T.astype(BF16)
        kd_ref[h, HEAD_DIM:, :] = _rope_d(kr * r * gk[:, HEAD_DIM:], cd, sd).T.astype(BF16)
    vd_ref[...] = kv[:, D_HEADS * D_NOPE:].astype(BF16)


def _prep_d(ql, kvl, kr, wq, wkv, gq, gk, cos_d, sin_d):
    s = ql.shape[0]
    tm = min(PREP_ROWS, s)

    def row(w):
        return pl.BlockSpec((tm, w), lambda i: (i, 0))

    def const(r, w):
        return pl.BlockSpec((r, w), lambda i: (0, 0))

    kt_shape, kt_spec = _kt_shape_spec(D_HEADS, s, tm)
    return pl.pallas_call(
        _prep_d_kernel, grid=(s // tm,),
        in_specs=[row(D_Q_LORA), row(D_KV_LORA), row(LANES),
                  const(D_Q_LORA, D_HEADS * AUG), const(D_KV_LORA, 2 * D_HEADS * D_NOPE),
                  const(1, AUG), const(1, AUG), row(LANES), row(LANES)],
        out_specs=[row(D_HEADS * AUG), kt_spec, row(D_HEADS * D_V)],
        out_shape=[jax.ShapeDtypeStruct((s, D_HEADS * AUG), BF16), kt_shape,
                   jax.ShapeDtypeStruct((s, D_HEADS * D_V), BF16)],
        compiler_params=_cparams(("parallel",), 40), name="prep_d",
    )(ql, kvl, kr, wq, wkv, gq, gk, cos_d, sin_d)


def _flash_kernel(q_ref, kt_ref, v_ref, o_ref, m_ref, l_ref, acc_ref, *, blk, unit):
    i = pl.program_id(1)
    m_ref[...] = jnp.full_like(m_ref, NEG_INF)
    l_ref[...] = jnp.zeros_like(l_ref)
    acc_ref[...] = jnp.zeros_like(acc_ref)
    rc = FLASH_ROW_CHUNK

    def block(j, row0, nrows, nkeys, masked):
        start = pl.multiple_of(j * blk, blk)
        for g in range(FLASH_HEADS):
            s = _dot(q_ref[row0:row0 + nrows, g * AUG:(g + 1) * AUG], kt_ref[g, j, :, :nkeys])
            v = v_ref[pl.ds(start, nkeys), g * LANES:(g + 1) * LANES]
            ps, alphas = [], []
            for r in range(nrows // rc):
                rows = slice(row0 + r * rc, row0 + (r + 1) * rc)
                sc = s[r * rc:(r + 1) * rc, :]
                if masked:
                    qi = lax.broadcasted_iota(jnp.int32, sc.shape, 0) + (row0 + r * rc)
                    ki = lax.broadcasted_iota(jnp.int32, sc.shape, 1)
                    if unit > 1:
                        shift = unit.bit_length() - 1
                        qi, ki = qi >> shift, ki >> shift
                    sc = jnp.where(ki <= qi, sc, NEG_INF)
                m_prev = m_ref[g, rows, :]
                m_new = jnp.maximum(m_prev, jnp.max(sc, axis=-1, keepdims=True))
                alpha = jnp.exp2(m_prev - m_new)
                pc = [jnp.exp2(sc[:, c * LANES:(c + 1) * LANES] - m_new)
                      for c in range(nkeys // LANES)]
                lsum = pc[0]
                for c in range(1, len(pc)):
                    lsum = lsum + pc[c]
                l_ref[g, rows, :] = alpha * l_ref[g, rows, :] + lsum
                m_ref[g, rows, :] = m_new
                ps.append(jnp.concatenate(pc, axis=1).astype(BF16))
                alphas.append(alpha)
            p = jnp.concatenate(ps, axis=0)
            alpha = jnp.concatenate(alphas, axis=0)
            acc_ref[g, row0:row0 + nrows, :] = (alpha * acc_ref[g, row0:row0 + nrows, :]
                                                + _dot(p, v))

    def body(j, carry):
        block(j, 0, blk, blk, False)
        return carry

    lax.fori_loop(0, i, body, 0)
    half = blk // 2
    if half % max(rc, unit, LANES) == 0:
        block(i, 0, half, half, True)
        block(i, half, half, blk, True)
    else:
        block(i, 0, blk, blk, True)
    for g in range(FLASH_HEADS):
        l = jnp.sum(l_ref[g], axis=-1, keepdims=True)
        o_ref[:, g * LANES:(g + 1) * LANES] = (acc_ref[g] / l).astype(o_ref.dtype)


def _flash(q, kt, v, n_heads, unit):
    s = q.shape[0]
    blk = kt.shape[3]
    dv = v.shape[1] // n_heads
    g = FLASH_HEADS
    assert dv == LANES and n_heads % g == 0
    once = pl.Buffered(1)
    return pl.pallas_call(
        functools.partial(_flash_kernel, blk=blk, unit=unit),
        grid=(n_heads // g, s // blk),
        in_specs=[pl.BlockSpec((blk, g * AUG), lambda h, i: (i, h)),
                  pl.BlockSpec((g, s // blk, AUG, blk), lambda h, i: (h, 0, 0, 0), pipeline_mode=once),
                  pl.BlockSpec((s, g * dv), lambda h, i: (0, h), pipeline_mode=once)],
        out_specs=pl.BlockSpec((blk, g * dv), lambda h, i: (i, h)),
        out_shape=jax.ShapeDtypeStruct((s, n_heads * dv), BF16),
        scratch_shapes=[pltpu.VMEM((g, blk, LANES), F32), pltpu.VMEM((g, blk, LANES), F32),
                        pltpu.VMEM((g, blk, dv), F32)],
        compiler_params=_cparams(("parallel", "arbitrary"), 56), name=f"flash_u{unit}",
    )(q, kt, v)


A_PREV_BLOCKS = A_PREV_CHUNKS * CHUNK // BAND_Q


def _attn_a_kernel(q_ref, k0_ref, k1_ref, k2_ref, v0_ref, v1_ref, v2_ref, t_ref, mk_ref, o_ref):
    rc = FLASH_ROW_CHUNK

    def scores(h):
        cols = slice(h * HEAD_DIM, (h + 1) * HEAD_DIM)
        k = jnp.concatenate([k0_ref[:, cols], k1_ref[:, cols], k2_ref[:, cols]], axis=0)
        return _dot_nt(q_ref[:, cols], k)

    s_next = scores(0)
    for h in range(A_HEADS):
        cols = slice(h * HEAD_DIM, (h + 1) * HEAD_DIM)
        v = jnp.concatenate([v0_ref[:, cols], v1_ref[:, cols], v2_ref[:, cols]], axis=0)
        s = s_next
        if h + 1 < A_HEADS:
            s_next = scores(h + 1)
        ps, ls = [], []
        for r in range(BAND_Q // rc):
            rows = slice(r * rc, (r + 1) * rc)
            sc = s[rows, :] + t_ref[h, rows, :] + mk_ref[rows, :]
            m = jnp.max(sc, axis=-1, keepdims=True)
            pc = [jnp.exp2(sc[:, c * LANES:(c + 1) * LANES] - m) for c in range(A_WIN // LANES)]
            lsum = pc[0]
            for c in range(1, len(pc)):
                lsum = lsum + pc[c]
            ls.append(lsum)
            ps.append(jnp.concatenate(pc, axis=1).astype(BF16))
        l = jnp.sum(jnp.concatenate(ls, axis=0), axis=-1, keepdims=True)
        o = _dot(jnp.concatenate(ps, axis=0), v) / l
        o_ref[:, cols] = o.astype(o_ref.dtype)


def _attn_a(q, k, v, table, mask):
    s, w = q.shape
    row = pl.BlockSpec((BAND_Q, w), lambda i: (i, 0))
    prev = [pl.BlockSpec((BAND_Q, w), functools.partial(
        lambda i, back: (jnp.maximum(i - back, 0), 0), back=back))
        for back in range(A_PREV_BLOCKS, -1, -1)]
    return pl.pallas_call(
        _attn_a_kernel, grid=(s // BAND_Q,),
        in_specs=[row] + prev + prev + [
            pl.BlockSpec((A_HEADS, BAND_Q, A_WIN), lambda i: (0, 0, 0)),
            pl.BlockSpec((None, BAND_Q, A_WIN), lambda i: (jnp.minimum(i, A_PREV_BLOCKS), 0, 0))],
        out_specs=row,
        out_shape=jax.ShapeDtypeStruct((s, w), BF16),
        compiler_params=_cparams(("arbitrary",), 40), name="attn_a",
    )(q, k, k, k, v, v, v, table, mask)


def _a_mask():
    b = np.arange(A_PREV_BLOCKS + 1)[:, None, None]
    q = np.arange(BAND_Q)[None, :, None]
    w = np.arange(A_WIN)[None, None, :]
    dchunk = (q + A_PREV_BLOCKS * BAND_Q) // CHUNK - w // CHUNK
    valid = (dchunk >= 0) & (dchunk <= A_PREV_CHUNKS) & (w >= (A_PREV_BLOCKS - b) * BAND_Q)
    return jnp.asarray(np.where(valid, 0.0, NEG_INF), F32)


def _a_table(rel_bias):
    n, m = BAND_Q, A_WIN
    u = np.arange(n + m)
    dist = A_PREV_BLOCKS * BAND_Q + (n - 1) - u
    idx = np.clip(dist, -REL_CLIP, REL_CLIP) + REL_CLIP
    z = rel_bias.astype(F32).T[:, jnp.asarray(idx)] * LOG2E
    skew = jnp.broadcast_to(z[:, None, :], (A_HEADS, n, n + m))
    skew = skew.reshape(A_HEADS, n * (n + m))[:, :n * (n + m - 1)]
    return skew.reshape(A_HEADS, n, n + m - 1)[:, :, n - 1:n - 1 + m]


B_KROWS = B_PREV_CHUNKS * CHUNK


def _attn_b_kernel(sink_ref, q_ref, k0_ref, k1_ref, k2_ref, v0_ref, v1_ref, v2_ref, o_ref):
    i = pl.program_id(0)
    group = B_Q_HEADS // B_KV_HEADS
    rows = group * BAND_Q
    shift = CHUNK.bit_length() - 1
    t = ((lax.broadcasted_iota(jnp.int32, (rows, B_WIN), 0) & (BAND_Q - 1)) + B_KROWS) >> shift
    w = lax.broadcasted_iota(jnp.int32, (rows, B_WIN), 1)
    first = jnp.where(i == 0, B_KROWS, 0)
    valid = jnp.where(w >= first, jnp.abs(t - (w >> shift) - 1), 2) <= 1
    head = lax.broadcasted_iota(jnp.int32, (rows, 1), 0) >> (BAND_Q.bit_length() - 1)
    for g in range(B_KV_HEADS):
        kc = slice(g * HEAD_DIM, (g + 1) * HEAD_DIM)
        k = jnp.concatenate([k0_ref[:, kc], k1_ref[:, kc], k2_ref[:, kc]], axis=0)
        v = jnp.concatenate([v0_ref[:, kc], v1_ref[:, kc], v2_ref[:, kc]], axis=0)
        q = jnp.concatenate([q_ref[:, (g * group + hi) * HEAD_DIM:(g * group + hi + 1) * HEAD_DIM]
                             for hi in range(group)], axis=0)
        s = jnp.where(valid, _dot_nt(q, k), NEG_INF)
        sink = jnp.zeros((rows, 1), F32)
        for hi in range(group):
            sink = jnp.where(head == hi, sink_ref[g * group + hi] * LOG2E, sink)
        m = jnp.maximum(jnp.max(s, axis=-1, keepdims=True), sink)
        p = jnp.exp2(s - m)
        l = jnp.sum(p, axis=-1, keepdims=True) + jnp.exp2(sink - m)
        o = (_dot(p.astype(BF16), v) / l).astype(o_ref.dtype)
        for hi in range(group):
            c = (g * group + hi) * HEAD_DIM
            o_ref[:, c:c + HEAD_DIM] = o[hi * BAND_Q:(hi + 1) * BAND_Q, :]


def _attn_b(q, k, v, sinks):
    s, w = q.shape
    kw = k.shape[1]
    per = BAND_Q // B_KROWS
    row = pl.BlockSpec((BAND_Q, w), lambda i: (i, 0))
    kv = [pl.BlockSpec((B_KROWS, kw), functools.partial(
        lambda i, off: (jnp.maximum(i * per + off, 0), 0), off=off)) for off in (-1, 0, 1)]
    return pl.pallas_call(
        _attn_b_kernel, grid=(s // BAND_Q,),
        in_specs=[pl.BlockSpec(memory_space=pltpu.SMEM), row] + kv + kv,
        out_specs=row,
        out_shape=jax.ShapeDtypeStruct((s, w), BF16),
        compiler_params=_cparams(("arbitrary",), 40), name="attn_b",
    )(sinks, q, k, k, k, v, v, v)


def _merge_kernel(h_ref, oa_ref, ob_ref, oc_ref, od_ref, wg_ref, bg_ref, wb_ref, out_ref):
    h = h_ref[...]
    acc = None
    for n, o_ref in enumerate((oa_ref, ob_ref, oc_ref, od_ref)):
        gate = jax.nn.sigmoid(_dot(h, wg_ref[n]) + bg_ref[n])
        val = gate * _dot(o_ref[...], wb_ref[n].astype(BF16))
        acc = val if acc is None else acc + val
    out_ref[...] = acc.astype(out_ref.dtype)


def _merge(h, outs, wg, bg, wb, l):
    s, d = h.shape
    tm, tn = min(1024, s), min(256, d)
    once = pl.Buffered(1)
    o_spec = pl.BlockSpec((tm, BRANCH_W), lambda i, j: (i, 0), pipeline_mode=once)
    return pl.pallas_call(
        _merge_kernel, grid=(s // tm, d // tn),
        in_specs=[pl.BlockSpec((tm, d), lambda i, j: (i, 0), pipeline_mode=once),
                  o_spec, o_spec, o_spec, o_spec,
                  pl.BlockSpec((None, N_BRANCH, d, tn), lambda i, j: (l, 0, 0, j)),
                  pl.BlockSpec((N_BRANCH, 1, tn), lambda i, j: (0, 0, j)),
                  pl.BlockSpec((None, N_BRANCH, BRANCH_W, tn), lambda i, j: (l, 0, 0, j))],
        out_specs=pl.BlockSpec((tm, tn), lambda i, j: (i, j)),
        out_shape=jax.ShapeDtypeStruct((s, d), BF16),
        compiler_params=_cparams(("parallel", "arbitrary"), 56), name="merge",
    )(h, *outs, wg, bg, wb)


def _resid_kernel(a_ref, w_ref, x_ref, g_ref, o_ref):
    o_ref[...] = x_ref[...] + g_ref[...] * _dot(a_ref[...], w_ref[...].astype(BF16))


def _resid_moe_kernel(a_ref, w_ref, c_ref, bd_ref, x_ref, g_ref, o_ref):
    y = _dot(a_ref[...], w_ref[...].astype(BF16)) + _dot(c_ref[...].astype(BF16), bd_ref[...])
    o_ref[...] = x_ref[...] + g_ref[...] * y


def _resid(a, w, l, x, gate, comb=None, b_dn=None):
    s, k = a.shape
    d = w.shape[2]
    tm, tn = min(1024, s), min(512, d)
    a_spec = pl.BlockSpec((tm, k), lambda i, j: (i, 0))
    w_spec = pl.BlockSpec((None, k, tn), lambda i, j: (l, 0, j))
    x_spec = pl.BlockSpec((tm, tn), lambda i, j: (i, j))
    g_spec = pl.BlockSpec((1, tn), lambda i, j: (0, j))
    if comb is None:
        kern, ins = _resid_kernel, (a, w, x, gate)
        in_specs = [a_spec, w_spec, x_spec, g_spec]
    else:
        kern, ins = _resid_moe_kernel, (a, w, comb, b_dn, x, gate)
        in_specs = [a_spec, w_spec, pl.BlockSpec((tm, LANES), lambda i, j: (i, 0)),
                    pl.BlockSpec((LANES, tn), lambda i, j: (0, j)), x_spec, g_spec]
    return pl.pallas_call(
        kern, grid=(s // tm, d // tn), in_specs=in_specs, out_specs=x_spec,
        out_shape=jax.ShapeDtypeStruct((s, d), F32),
        compiler_params=_cparams(("parallel", "parallel"), 48), name="resid",
    )(*ins)


def _moe_up_kernel(h_ref, w_ref, b_ref, c_ref, o_ref, *, eb):
    j = pl.program_id(1)
    h = h_ref[...]
    comb = c_ref[...]
    lane = lax.broadcasted_iota(jnp.int32, comb.shape, 1)
    for e in range(eb):
        gu = _dot(h, w_ref[e].astype(BF16)) + b_ref[e]
        glu = jnp.minimum(gu[:, :D_EXPERT], SWIGLU_LIMIT)
        lin = jnp.clip(gu[:, D_EXPERT:], -SWIGLU_LIMIT, SWIGLU_LIMIT)
        act = glu * jax.nn.sigmoid(SWIGLU_ALPHA * glu) * (lin + 1.0)
        ce = jnp.sum(jnp.where(lane == j * eb + e, comb, 0.0), axis=-1, keepdims=True)
        o_ref[:, e * D_EXPERT:(e + 1) * D_EXPERT] = (act * ce).astype(o_ref.dtype)


def _moe_up(h, w_gu, l, b_gu, comb):
    s, d = h.shape
    tm, eb = min(1024, s), 2
    return pl.pallas_call(
        functools.partial(_moe_up_kernel, eb=eb), grid=(s // tm, N_EXPERTS // eb),
        in_specs=[pl.BlockSpec((tm, d), lambda i, j: (i, 0)),
                  pl.BlockSpec((None, eb, d, 2 * D_EXPERT), lambda i, j: (l, j, 0, 0)),
                  pl.BlockSpec((eb, 1, 2 * D_EXPERT), lambda i, j: (j, 0, 0)),
                  pl.BlockSpec((tm, LANES), lambda i, j: (i, 0))],
        out_specs=pl.BlockSpec((tm, eb * D_EXPERT), lambda i, j: (i, j)),
        out_shape=jax.ShapeDtypeStruct((s, N_EXPERTS * D_EXPERT), BF16),
        compiler_params=_cparams(("parallel", "parallel"), 48), name="moe_up",
    )(h, w_gu, b_gu, comb)


def _w_in_tail(w):
    d = w.shape[0]
    zeros = lambda n: jnp.zeros((d, n), BF16)
    return jnp.concatenate(
        [w[:, C_HEADS:].astype(BF16), zeros(LANES - D_ROPE), w[:, :C_HEADS].astype(BF16),
         zeros(N_P - OFF_F - C_HEADS)], axis=1)


def _reorder_w_q_b(w):
    r = w.shape[0]
    w = w.reshape(r, D_HEADS, D_QK)
    w = jnp.pad(w, ((0, 0), (0, 0), (0, AUG - D_QK)))
    return w.reshape(r, D_HEADS * AUG)


def _reorder_w_kv_b(w):
    r = w.shape[0]
    w = w.reshape(r, D_HEADS, D_NOPE + D_V)
    return jnp.concatenate([w[:, :, :D_NOPE].reshape(r, -1), w[:, :, D_NOPE:].reshape(r, -1)], axis=1)


def _pad_lanes(v, width, value=0.0):
    return jnp.pad(v, (0, width - v.shape[0]), constant_values=value).reshape(1, width)


def kernel(x, c, positions, ada_w, ada_b, ada_layer, norm1_g, norm2_g, w_in, a_q_norm, a_k_norm, a_rel_bias, b_q_norm, b_k_norm, b_sinks, c_q_norm, c_k_norm, c_f_bias, d_q_a_norm, d_w_q_b, d_kv_a_norm, d_w_kv_b, d_q_norm, d_k_norm, w_branch, w_gate, b_gate, w_out, router_w, router_b, w_gu, b_gu, w_dn, b_dn):
    batch, s, d = x.shape
    assert batch == 1, "kernels are written for a single sequence"
    depth = w_in.shape[0]
    xs = x.reshape(s, d)

    base_mod = _adaln(c.reshape(d, 1), ada_w, ada_b).reshape(6, d)
    cos_b, sin_b, cos_d, sin_d = _rope_tables(positions.reshape(s, 1).astype(F32))
    a_mask = _a_mask()
    w_gate_bf16 = w_gate.astype(BF16)
    w_in_t = jnp.swapaxes(w_in, 1, 2)

    for l in range(depth):
        mod = base_mod + ada_layer[l]
        shift1, scale1, gate1, shift2, scale2, gate2 = (mod[j:j + 1] for j in range(6))

        h = _norm(xs, norm1_g[l].reshape(1, d), scale1, shift1)
        p = _in_proj(h, w_in_t[l, :OFF_QLAT, :].astype(BF16), _w_in_tail(w_in[l, :, OFF_QLAT:]))
        gains = jnp.stack([a_q_norm[l], a_k_norm[l], b_q_norm[l], b_k_norm[l],
                           c_q_norm[l], c_k_norm[l], jnp.zeros_like(a_q_norm[l]),
                           jnp.zeros_like(a_q_norm[l])])
        (qa, ka, va, qb, kb, vb, qc, kc, vc, ql, kvl, kr) = _prep(
            p, gains, d_q_a_norm[l].reshape(1, -1), d_kv_a_norm[l].reshape(1, -1),
            _pad_lanes(c_f_bias[l], LANES), cos_b, sin_b)
        qd, kd, vd = _prep_d(
            ql, kvl, kr, _reorder_w_q_b(d_w_q_b[l]).astype(BF16),
            _reorder_w_kv_b(d_w_kv_b[l]).astype(BF16),
            _pad_lanes(d_q_norm[l], AUG), _pad_lanes(d_k_norm[l], AUG), cos_d, sin_d)

        o_a = _attn_a(qa, ka, va, _a_table(a_rel_bias[l]), a_mask)
        o_b = _attn_b(qb, kb, vb, b_sinks[l].astype(F32))
        o_c = _flash(qc, kc, vc, C_HEADS, 1)
        o_d = _flash(qd, kd, vd, D_HEADS, CHUNK)

        merged = _merge(h, (o_a, o_b, o_c, o_d), w_gate_bf16,
                        b_gate[l].reshape(N_BRANCH, 1, d), w_branch, l)
        xs = _resid(merged, w_out, l, xs, gate1)

        rw = jnp.pad(router_w[l], ((0, 0), (0, LANES - N_EXPERTS)))
        rb = _pad_lanes(router_b[l].astype(F32), LANES, NEG_INF)
        h2, comb = _norm(xs, norm2_g[l].reshape(1, d), scale2, shift2, router=(rw, rb))
        act = _moe_up(h2, w_gu, l, b_gu[l].reshape(N_EXPERTS, 1, 2 * D_EXPERT), comb)
        bdn = jnp.pad(b_dn[l], ((0, LANES - N_EXPERTS), (0, 0))).astype(BF16)
        xs = _resid(act, w_dn.reshape(depth, N_EXPERTS * D_EXPERT, d), l, xs, gate2,
                    comb=comb, b_dn=bdn)
    return xs.reshape(batch, s, d)
```

```python
import functools
import math

import numpy as np
import jax
import jax.numpy as jnp
from jax import lax
from jax.experimental import pallas as pl
from jax.experimental.pallas import tpu as pltpu

F32 = jnp.float32
BF16 = jnp.bfloat16

CHUNK = 64
HEAD_DIM = 128
N_BRANCH = 4
BRANCH_W = 1024
ROPE_THETA = 10000.0
EPS = 1e-6
NEG_INF = -1e30
A_HEADS = 8
A_PREV_CHUNKS = 8
REL_CLIP = 256
B_Q_HEADS = 8
B_KV_HEADS = 2
B_PREV_CHUNKS = 2
C_HEADS = 8
D_HEADS = 8
D_Q_LORA = 896
D_KV_LORA = 256
D_NOPE = 128
D_ROPE = 64
D_V = 128
D_QK = D_NOPE + D_ROPE
N_EXPERTS = 32
TOP_K = 4
D_EXPERT = 128
SWIGLU_LIMIT = 7.0
SWIGLU_ALPHA = 1.702

LANES = 128
V7X_VMEM_BYTES = 64 * 1024 * 1024

LOG2E = math.log2(math.e)

OFF_QA, OFF_KA, OFF_VA = 0, 1024, 2048
OFF_QB, OFF_KB, OFF_VB = 3072, 4096, 4352
OFF_QC, OFF_KC, OFF_VC = 4608, 5632, 6656
OFF_QLAT = 7680
OFF_KVLAT = OFF_QLAT + D_Q_LORA
OFF_KROPE = OFF_KVLAT + D_KV_LORA
OFF_F = OFF_KROPE + LANES
N_P = 9216

AUG = 2 * HEAD_DIM
FLASH_BLK = 1024
FLASH_ROW_CHUNK = 32
FLASH_HEADS = 2
PREP_ROWS = 256
BAND_Q = 256
A_WIN = BAND_Q + A_PREV_CHUNKS * CHUNK
B_WIN = BAND_Q + B_PREV_CHUNKS * CHUNK


def _cparams(sems, vmem_mb):
    return pltpu.CompilerParams(dimension_semantics=sems,
                                vmem_limit_bytes=vmem_mb * 1024 * 1024)


def _dot(a, b):
    return jnp.dot(a, b, preferred_element_type=F32)


def _dot_nt(a, b):
    return lax.dot_general(a, b, (((1,), (1,)), ((), ())), preferred_element_type=F32)


def _adaln_kernel(c_ref, w_ref, b_ref, o_ref):
    c = c_ref[...]
    s = c * jax.nn.sigmoid(c)
    o_ref[...] = jnp.sum(w_ref[...] * s, axis=0, keepdims=True) + b_ref[...]


def _adaln(c_col, ada_w, ada_b):
    d, n = ada_w.shape
    tn = min(512, n)
    return pl.pallas_call(
        _adaln_kernel,
        grid=(n // tn,),
        in_specs=[pl.BlockSpec((d, 1), lambda j: (0, 0)),
                  pl.BlockSpec((d, tn), lambda j: (0, j)),
                  pl.BlockSpec((1, tn), lambda j: (0, j))],
        out_specs=pl.BlockSpec((1, tn), lambda j: (0, j)),
        out_shape=jax.ShapeDtypeStruct((1, n), F32),
        compiler_params=_cparams(("parallel",), 40),
        name="adaln",
    )(c_col, ada_w, ada_b.reshape(1, n))


def _rope_table_kernel(pos_ref, fb_ref, fd_ref, cb_ref, sb_ref, cd_ref, sd_ref):
    pos = pos_ref[...]
    lane = lax.broadcasted_iota(jnp.int32, (1, LANES), 1)
    ab = pos * fb_ref[...]
    sb = jnp.sin(ab)
    cb_ref[...] = jnp.cos(ab)
    sb_ref[...] = jnp.where(lane < HEAD_DIM // 2, -sb, sb)
    ad = pos * fd_ref[...]
    sd = jnp.sin(ad)
    cd_ref[...] = jnp.where(lane < D_ROPE, jnp.cos(ad), 1.0)
    sd_ref[...] = jnp.where(lane < D_ROPE // 2, -sd, jnp.where(lane < D_ROPE, sd, 0.0))


def _rope_tables(pos_col):
    s = pos_col.shape[0]
    tm = min(512, s)
    lane = np.arange(LANES)
    half_b = HEAD_DIM // 2
    fb = ROPE_THETA ** (-(lane % half_b).astype(np.float64) / half_b)
    half_d = D_ROPE // 2
    fd = np.where(lane < D_ROPE, ROPE_THETA ** (-(lane % half_d).astype(np.float64) / half_d), 0.0)
    fb = jnp.asarray(fb, F32).reshape(1, LANES)
    fd = jnp.asarray(fd, F32).reshape(1, LANES)
    tab = jax.ShapeDtypeStruct((s, LANES), F32)
    row = pl.BlockSpec((tm, LANES), lambda i: (i, 0))
    const = pl.BlockSpec((1, LANES), lambda i: (0, 0))
    return pl.pallas_call(
        _rope_table_kernel,
        grid=(s // tm,),
        in_specs=[pl.BlockSpec((tm, 1), lambda i: (i, 0)), const, const],
        out_specs=[row, row, row, row],
        out_shape=[tab, tab, tab, tab],
        compiler_params=_cparams(("parallel",), 32),
        name="rope_tables",
    )(pos_col, fb, fd)


def _mod_norm(x, g, sc, sh):
    ms = jnp.mean(x * x, axis=-1, keepdims=True)
    y = x * lax.rsqrt(ms + EPS)
    return (y * g) * (1.0 + sc) + sh


def _norm_kernel(x_ref, g_ref, sc_ref, sh_ref, h_ref):
    h_ref[...] = _mod_norm(x_ref[...], g_ref[...], sc_ref[...], sh_ref[...]).astype(h_ref.dtype)


def _norm_router_kernel(x_ref, g_ref, sc_ref, sh_ref, rwh_ref, rwl_ref, rb_ref, h_ref, comb_ref):
    h = _mod_norm(x_ref[...], g_ref[...], sc_ref[...], sh_ref[...])
    h_hi = h.astype(BF16)
    h_ref[...] = h_hi
    h_lo = (h - h_hi.astype(F32)).astype(BF16)
    rw_hi = rwh_ref[...]
    logits = (_dot(h_hi, rw_hi) + _dot(h_lo, rw_hi) + _dot(h_hi, rwl_ref[...])
              + rb_ref[...])
    lane = lax.broadcasted_iota(jnp.int32, logits.shape, 1).astype(F32)
    work = logits
    vals, sels = [], []
    for _ in range(TOP_K):
        m = jnp.max(work, axis=-1, keepdims=True)
        idx = jnp.min(jnp.where(work == m, lane, float(LANES)), axis=-1, keepdims=True)
        sel = lane == idx
        vals.append(m)
        sels.append(sel)
        work = jnp.where(sel, -3.0e38, work)
    es = [jnp.exp(v - vals[0]) for v in vals]
    inv = 1.0 / (es[0] + es[1] + es[2] + es[3])
    comb = jnp.zeros_like(logits)
    for e, sel in zip(es, sels):
        comb = comb + jnp.where(sel, e * inv, 0.0)
    comb_ref[...] = comb


def _norm(x, g, sc, sh, router=None):
    s, d = x.shape
    tm = min(256, s)
    row = pl.BlockSpec((tm, d), lambda i: (i, 0))
    vec = pl.BlockSpec((1, d), lambda i: (0, 0))
    if router is None:
        return pl.pallas_call(
            _norm_kernel, grid=(s // tm,),
            in_specs=[row, vec, vec, vec], out_specs=row,
            out_shape=jax.ShapeDtypeStruct((s, d), BF16),
            compiler_params=_cparams(("parallel",), 32), name="norm",
        )(x, g, sc, sh)
    rw, rb = router
    rw_hi = rw.astype(BF16)
    rw_lo = (rw - rw_hi.astype(F32)).astype(BF16)
    rw_spec = pl.BlockSpec((d, LANES), lambda i: (0, 0))
    return pl.pallas_call(
        _norm_router_kernel, grid=(s // tm,),
        in_specs=[row, vec, vec, vec, rw_spec, rw_spec,
                  pl.BlockSpec((1, LANES), lambda i: (0, 0))],
        out_specs=[row, pl.BlockSpec((tm, LANES), lambda i: (i, 0))],
        out_shape=[jax.ShapeDtypeStruct((s, d), BF16), jax.ShapeDtypeStruct((s, LANES), F32)],
        compiler_params=_cparams(("parallel",), 40), name="norm_router",
    )(x, g, sc, sh, rw_hi, rw_lo, rb)


IN_TN = 768
IN_MAIN_TILES = OFF_QLAT // IN_TN


def _in_proj_kernel(a_ref, w_ref, wt_ref, o_ref):
    j = pl.program_id(1)

    @pl.when(j < IN_MAIN_TILES)
    def _():
        o_ref[...] = _dot_nt(a_ref[...], w_ref[...]).astype(o_ref.dtype)

    @pl.when(j >= IN_MAIN_TILES)
    def _():
        o_ref[...] = _dot(a_ref[...], wt_ref[...]).astype(o_ref.dtype)


def _in_proj(a, w_t, l, w_tail):
    m, k = a.shape
    tm = min(1024, m)
    n_tail = w_tail.shape[1] // IN_TN
    last = IN_MAIN_TILES - 1
    return pl.pallas_call(
        _in_proj_kernel, grid=(m // tm, IN_MAIN_TILES + n_tail),
        in_specs=[pl.BlockSpec((tm, k), lambda i, j: (i, 0)),
                  pl.BlockSpec((None, IN_TN, k), lambda i, j: (l, jnp.minimum(j, last), 0)),
                  pl.BlockSpec((k, IN_TN), lambda i, j: (0, jnp.maximum(j - IN_MAIN_TILES, 0)))],
        out_specs=pl.BlockSpec((tm, IN_TN), lambda i, j: (i, j)),
        out_shape=jax.ShapeDtypeStruct((m, N_P), BF16),
        compiler_params=_cparams(("parallel", "arbitrary"), 56), name="in_proj",
    )(a, w_t, w_tail)


def _prep_kernel(p_ref, gains_ref, gql_ref, gkvl_ref, fb_ref, cb_ref, sb_ref,
                 qa_ref, ka_ref, va_ref, qb_ref, kb_ref, vb_ref,
                 qc_ref, kc_ref, vc_ref, ql_ref, kvl_ref, kr_ref, carry_ref):
    tm = p_ref.shape[0]
    qscale = HEAD_DIM ** -0.5 * LOG2E

    @pl.when(pl.program_id(0) == 0)
    def _():
        carry_ref[...] = jnp.zeros_like(carry_ref)

    def head_norm(off, g_row, scale):
        x = p_ref[:, off:off + HEAD_DIM].astype(F32)
        ms = jnp.mean(x * x, axis=-1, keepdims=True)
        return x * lax.rsqrt(ms + EPS) * (gains_ref[g_row:g_row + 1, :] * scale)

    cb, sb = cb_ref[...], sb_ref[...]

    def rope(y):
        return y * cb + pltpu.roll(y, HEAD_DIM // 2, 1) * sb

    for h in range(A_HEADS):
        c = h * HEAD_DIM
        qa_ref[:, c:c + HEAD_DIM] = head_norm(OFF_QA + c, 0, qscale).astype(BF16)
        ka_ref[:, c:c + HEAD_DIM] = head_norm(OFF_KA + c, 1, 1.0).astype(BF16)
    va_ref[...] = p_ref[:, OFF_VA:OFF_VA + A_HEADS * HEAD_DIM]

    for h in range(B_Q_HEADS):
        c = h * HEAD_DIM
        qb_ref[:, c:c + HEAD_DIM] = rope(head_norm(OFF_QB + c, 2, qscale)).astype(BF16)
    for h in range(B_KV_HEADS):
        c = h * HEAD_DIM
        kb_ref[:, c:c + HEAD_DIM] = rope(head_norm(OFF_KB + c, 3, 1.0)).astype(BF16)
    vb_ref[...] = p_ref[:, OFF_VB:OFF_VB + B_KV_HEADS * HEAD_DIM]

    z = p_ref[:, OFF_F:OFF_F + LANES].astype(F32) + fb_ref[...]
    logf = jnp.minimum(z, 0.0) - jnp.log1p(jnp.exp(-jnp.abs(z)))
    r_i = lax.broadcasted_iota(jnp.int32, (tm, tm), 0)
    c_i = lax.broadcasted_iota(jnp.int32, (tm, tm), 1)
    tri = jnp.where(r_i >= c_i, 1.0, 0.0).astype(F32)
    cum = jnp.dot(tri, logf, precision=lax.Precision.HIGHEST,
                  preferred_element_type=F32) + carry_ref[...]
    carry_ref[...] = cum[tm - 1:tm, :]
    cum2 = cum * LOG2E
    hi = cum2.astype(BF16).astype(F32)
    r1 = cum2 - hi
    mid = r1.astype(BF16).astype(F32)
    lo = r1 - mid
    lane = lax.broadcasted_iota(jnp.int32, (tm, LANES), 1)
    for h in range(C_HEADS):
        c = h * HEAD_DIM
        a = h * AUG
        hi_h, mid_h, lo_h = hi[:, h:h + 1], mid[:, h:h + 1], lo[:, h:h + 1]
        aug_q = jnp.where(lane == 0, hi_h, jnp.where(lane == 1, mid_h, jnp.where(
            lane == 2, lo_h, jnp.where(lane < 6, 1.0, 0.0))))
        aug_k = jnp.where(lane < 3, 1.0, jnp.where(lane == 3, -hi_h, jnp.where(
            lane == 4, -mid_h, jnp.where(lane == 5, -lo_h, 0.0))))
        qc_ref[:, a:a + HEAD_DIM] = head_norm(OFF_QC + c, 4, qscale).astype(BF16)
        qc_ref[:, a + HEAD_DIM:a + AUG] = aug_q.astype(BF16)
        kc_ref[h, :HEAD_DIM, :] = head_norm(OFF_KC + c, 5, 1.0).T.astype(BF16)
        kc_ref[h, HEAD_DIM:, :] = aug_k.T.astype(BF16)
    vc_ref[...] = p_ref[:, OFF_VC:OFF_VC + C_HEADS * HEAD_DIM]

    xq = p_ref[:, OFF_QLAT:OFF_QLAT + D_Q_LORA].astype(F32)
    ms = jnp.mean(xq * xq, axis=-1, keepdims=True)
    ql_ref[...] = (xq * lax.rsqrt(ms + EPS) * gql_ref[...]).astype(BF16)
    xkv = p_ref[:, OFF_KVLAT:OFF_KVLAT + D_KV_LORA].astype(F32)
    ms = jnp.mean(xkv * xkv, axis=-1, keepdims=True)
    kvl_ref[...] = (xkv * lax.rsqrt(ms + EPS) * gkvl_ref[...]).astype(BF16)
    kr_ref[...] = p_ref[:, OFF_KROPE:OFF_KROPE + LANES].astype(F32)


def _kt_shape_spec(n_heads, s, tm):
    blk = min(FLASH_BLK, s)
    per = blk // tm
    shape = jax.ShapeDtypeStruct((n_heads, s // blk, AUG, blk), BF16)
    spec = pl.BlockSpec((n_heads, None, AUG, tm), lambda i: (0, i // per, 0, i % per))
    return shape, spec


def _prep(p, gains, gql, gkvl, fb, cos_b, sin_b):
    s = p.shape[0]
    tm = min(PREP_ROWS, s)

    def row(w):
        return pl.BlockSpec((tm, w), lambda i: (i, 0))

    def const(r, w):
        return pl.BlockSpec((r, w), lambda i: (0, 0))

    widths = [1024, 1024, 1024, 1024, 256, 256, C_HEADS * AUG, None, 1024,
              D_Q_LORA, D_KV_LORA]
    out_shape = [jax.ShapeDtypeStruct((s, w), BF16) for w in widths if w]
    out_specs = [row(w) for w in widths if w]
    kt_shape, kt_spec = _kt_shape_spec(C_HEADS, s, tm)
    out_shape.insert(7, kt_shape)
    out_specs.insert(7, kt_spec)
    out_shape.append(jax.ShapeDtypeStruct((s, LANES), F32))
    out_specs.append(row(LANES))
    return pl.pallas_call(
        _prep_kernel, grid=(s // tm,),
        in_specs=[row(N_P), const(8, LANES), const(1, D_Q_LORA), const(1, D_KV_LORA),
                  const(1, LANES), row(LANES), row(LANES)],
        out_specs=out_specs, out_shape=out_shape,
        scratch_shapes=[pltpu.VMEM((1, LANES), F32)],
        compiler_params=_cparams(("arbitrary",), 48), name="prep",
    )(p, gains, gql, gkvl, fb, cos_b, sin_b)


def _rope_d(y, cd, sd):
    lane = lax.broadcasted_iota(jnp.int32, y.shape, 1)
    half = D_ROPE // 2
    partner = jnp.where(lane < half, pltpu.roll(y, LANES - half, 1), pltpu.roll(y, half, 1))
    return y * cd + partner * sd


def _prep_d_kernel(ql_ref, kvl_ref, kr_ref, wq_ref, wkv_ref, gq_ref, gk_ref, cd_ref, sd_ref,
                   qd_ref, kd_ref, vd_ref):
    qscale = D_QK ** -0.5 * LOG2E
    cd, sd = cd_ref[...], sd_ref[...]
    q = _dot(ql_ref[...], wq_ref[...])
    kv = _dot(kvl_ref[...], wkv_ref[...])
    kr = kr_ref[...]
    ss_r = jnp.sum(kr * kr, axis=-1, keepdims=True)
    gq = gq_ref[...]
    gk = gk_ref[...]
    for h in range(D_HEADS):
        a = h * AUG
        qn, qr = q[:, a:a + HEAD_DIM], q[:, a + HEAD_DIM:a + AUG]
        ms = (jnp.sum(qn * qn, axis=-1, keepdims=True)
              + jnp.sum(qr * qr, axis=-1, keepdims=True)) * (1.0 / D_QK)
        r = lax.rsqrt(ms + EPS) * qscale
        qd_ref[:, a:a + HEAD_DIM] = (qn * r * gq[:, :HEAD_DIM]).astype(BF16)
        qd_ref[:, a + HEAD_DIM:a + AUG] = _rope_d(qr * r * gq[:, HEAD_DIM:], cd, sd).astype(BF16)
        kn = kv[:, h * D_NOPE:(h + 1) * D_NOPE]
        ms = (jnp.sum(kn * kn, axis=-1, keepdims=True) + ss_r) * (1.0 / D_QK)
        r = lax.rsqrt(ms + EPS)
        kd_ref[h, :HEAD_DIM, :] = (kn * r * gk[:, :HEAD_DIM]).T.astype(BF16)
        kd_ref[h, HEAD_DIM:, :] = _rope_d(kr * r * gk[:, HEAD_DIM:], cd, sd).T.astype(BF16)
    vd_ref[...] = kv[:, D_HEADS * D_NOPE:].astype(BF16)


def _prep_d(ql, kvl, kr, wq, wkv, gq, gk, cos_d, sin_d):
    s = ql.shape[0]
    tm = min(PREP_ROWS, s)

    def row(w):
        return pl.BlockSpec((tm, w), lambda i: (i, 0))

    def const(r, w):
        return pl.BlockSpec((r, w), lambda i: (0, 0))

    kt_shape, kt_spec = _kt_shape_spec(D_HEADS, s, tm)
    return pl.pallas_call(
        _prep_d_kernel, grid=(s // tm,),
        in_specs=[row(D_Q_LORA), row(D_KV_LORA), row(LANES),
                  const(D_Q_LORA, D_HEADS * AUG), const(D_KV_LORA, 2 * D_HEADS * D_NOPE),
                  const(1, AUG), const(1, AUG), row(LANES), row(LANES)],
        out_specs=[row(D_HEADS * AUG), kt_spec, row(D_HEADS * D_V)],
        out_shape=[jax.ShapeDtypeStruct((s, D_HEADS * AUG), BF16), kt_shape,
                   jax.ShapeDtypeStruct((s, D_HEADS * D_V), BF16)],
        compiler_params=_cparams(("parallel",), 40), name="prep_d",
    )(ql, kvl, kr, wq, wkv, gq, gk, cos_d, sin_d)


def _flash_kernel(q_ref, kt_ref, v_ref, o_ref, m_ref, l_ref, acc_ref, *, blk, unit):
    i = pl.program_id(1)
    m_ref[...] = jnp.full_like(m_ref, NEG_INF)
    l_ref[...] = jnp.zeros_like(l_ref)
    acc_ref[...] = jnp.zeros_like(acc_ref)
    rc = FLASH_ROW_CHUNK

    def block(j, row0, nrows, nkeys, masked):
        start = pl.multiple_of(j * blk, blk)
        for g in range(FLASH_HEADS):
            s = _dot(q_ref[row0:row0 + nrows, g * AUG:(g + 1) * AUG], kt_ref[g, j, :, :nkeys])
            v = v_ref[pl.ds(start, nkeys), g * LANES:(g + 1) * LANES]
            ps, alphas = [], []
            for r in range(nrows // rc):
                rows = slice(row0 + r * rc, row0 + (r + 1) * rc)
                sc = s[r * rc:(r + 1) * rc, :]
                if masked:
                    qi = lax.broadcasted_iota(jnp.int32, sc.shape, 0) + (row0 + r * rc)
                    ki = lax.broadcasted_iota(jnp.int32, sc.shape, 1)
                    if unit > 1:
                        shift = unit.bit_length() - 1
                        qi, ki = qi >> shift, ki >> shift
                    sc = jnp.where(ki <= qi, sc, NEG_INF)
                m_prev = m_ref[g, rows, :]
                m_new = jnp.maximum(m_prev, jnp.max(sc, axis=-1, keepdims=True))
                alpha = jnp.exp2(m_prev - m_new)
                pc = [jnp.exp2(sc[:, c * LANES:(c + 1) * LANES] - m_new)
                      for c in range(nkeys // LANES)]
                lsum = pc[0]
                for c in range(1, len(pc)):
                    lsum = lsum + pc[c]
                l_ref[g, rows, :] = alpha * l_ref[g, rows, :] + lsum
                m_ref[g, rows, :] = m_new
                ps.append(jnp.concatenate(pc, axis=1).astype(BF16))
                alphas.append(alpha)
            p = jnp.concatenate(ps, axis=0)
            alpha = jnp.concatenate(alphas, axis=0)
            acc_ref[g, row0:row0 + nrows, :] = (alpha * acc_ref[g, row0:row0 + nrows, :]
                                                + _dot(p, v))

    def body(j, carry):
        block(j, 0, blk, blk, False)
        return carry

    lax.fori_loop(0, i, body, 0)
    half = blk // 2
    if half % max(rc, unit, LANES) == 0:
        block(i, 0, half, half, True)
        block(i, half, half, blk, True)
    else:
        block(i, 0, blk, blk, True)
    for g in range(FLASH_HEADS):
        l = jnp.sum(l_ref[g], axis=-1, keepdims=True)
        o_ref[:, g * LANES:(g + 1) * LANES] = (acc_ref[g] / l).astype(o_ref.dtype)


def _flash(q, kt, v, n_heads, unit):
    s = q.shape[0]
    blk = kt.shape[3]
    dv = v.shape[1] // n_heads
    g = FLASH_HEADS
    assert dv == LANES and n_heads % g == 0
    once = pl.Buffered(1)
    return pl.pallas_call(
        functools.partial(_flash_kernel, blk=blk, unit=unit),
        grid=(n_heads // g, s // blk),
        in_specs=[pl.BlockSpec((blk, g * AUG), lambda h, i: (i, h)),
                  pl.BlockSpec((g, s // blk, AUG, blk), lambda h, i: (h, 0, 0, 0), pipeline_mode=once),
                  pl.BlockSpec((s, g * dv), lambda h, i: (0, h), pipeline_mode=once)],
        out_specs=pl.BlockSpec((blk, g * dv), lambda h, i: (i, h)),
        out_shape=jax.ShapeDtypeStruct((s, n_heads * dv), BF16),
        scratch_shapes=[pltpu.VMEM((g, blk, LANES), F32), pltpu.VMEM((g, blk, LANES), F32),
                        pltpu.VMEM((g, blk, dv), F32)],
        compiler_params=_cparams(("parallel", "arbitrary"), 56), name=f"flash_u{unit}",
    )(q, kt, v)


A_PREV_BLOCKS = A_PREV_CHUNKS * CHUNK // BAND_Q


def _attn_a_kernel(q_ref, k0_ref, k1_ref, k2_ref, v0_ref, v1_ref, v2_ref, t_ref, mk_ref, o_ref):
    rc = FLASH_ROW_CHUNK

    def scores(h):
        cols = slice(h * HEAD_DIM, (h + 1) * HEAD_DIM)
        k = jnp.concatenate([k0_ref[:, cols], k1_ref[:, cols], k2_ref[:, cols]], axis=0)
        return _dot_nt(q_ref[:, cols], k)

    s_next = scores(0)
    for h in range(A_HEADS):
        cols = slice(h * HEAD_DIM, (h + 1) * HEAD_DIM)
        v = jnp.concatenate([v0_ref[:, cols], v1_ref[:, cols], v2_ref[:, cols]], axis=0)
        s = s_next
        if h + 1 < A_HEADS:
            s_next = scores(h + 1)
        ps, ls = [], []
        for r in range(BAND_Q // rc):
            rows = slice(r * rc, (r + 1) * rc)
            sc = s[rows, :] + t_ref[h, rows, :] + mk_ref[rows, :]
            m = jnp.max(sc, axis=-1, keepdims=True)
            pc = [jnp.exp2(sc[:, c * LANES:(c + 1) * LANES] - m) for c in range(A_WIN // LANES)]
            lsum = pc[0]
            for c in range(1, len(pc)):
                lsum = lsum + pc[c]
            ls.append(lsum)
            ps.append(jnp.concatenate(pc, axis=1).astype(BF16))
        l = jnp.sum(jnp.concatenate(ls, axis=0), axis=-1, keepdims=True)
        o = _dot(jnp.concatenate(ps, axis=0), v) / l
        o_ref[:, cols] = o.astype(o_ref.dtype)


def _attn_a(q, k, v, table, mask):
    s, w = q.shape
    row = pl.BlockSpec((BAND_Q, w), lambda i: (i, 0))
    prev = [pl.BlockSpec((BAND_Q, w), functools.partial(
        lambda i, back: (jnp.maximum(i - back, 0), 0), back=back))
        for back in range(A_PREV_BLOCKS, -1, -1)]
    return pl.pallas_call(
        _attn_a_kernel, grid=(s // BAND_Q,),
        in_specs=[row] + prev + prev + [
            pl.BlockSpec((A_HEADS, BAND_Q, A_WIN), lambda i: (0, 0, 0)),
            pl.BlockSpec((None, BAND_Q, A_WIN), lambda i: (jnp.minimum(i, A_PREV_BLOCKS), 0, 0))],
        out_specs=row,
        out_shape=jax.ShapeDtypeStruct((s, w), BF16),
        compiler_params=_cparams(("arbitrary",), 40), name="attn_a",
    )(q, k, k, k, v, v, v, table, mask)


def _a_mask():
    b = np.arange(A_PREV_BLOCKS + 1)[:, None, None]
    q = np.arange(BAND_Q)[None, :, None]
    w = np.arange(A_WIN)[None, None, :]
    dchunk = (q + A_PREV_BLOCKS * BAND_Q) // CHUNK - w // CHUNK
    valid = (dchunk >= 0) & (dchunk <= A_PREV_CHUNKS) & (w >= (A_PREV_BLOCKS - b) * BAND_Q)
    return jnp.asarray(np.where(valid, 0.0, NEG_INF), F32)


def _a_table(rel_bias):
    n, m = BAND_Q, A_WIN
    u = np.arange(n + m)
    dist = A_PREV_BLOCKS * BAND_Q + (n - 1) - u
    idx = np.clip(dist, -REL_CLIP, REL_CLIP) + REL_CLIP
    z = rel_bias.astype(F32).T[:, jnp.asarray(idx)] * LOG2E
    skew = jnp.broadcast_to(z[:, None, :], (A_HEADS, n, n + m))
    skew = skew.reshape(A_HEADS, n * (n + m))[:, :n * (n + m - 1)]
    return skew.reshape(A_HEADS, n, n + m - 1)[:, :, n - 1:n - 1 + m]


B_KROWS = B_PREV_CHUNKS * CHUNK


def _attn_b_kernel(sink_ref, q_ref, k0_ref, k1_ref, k2_ref, v0_ref, v1_ref, v2_ref, o_ref):
    i = pl.program_id(0)
    group = B_Q_HEADS // B_KV_HEADS
    rows = group * BAND_Q
    shift = CHUNK.bit_length() - 1
    t = ((lax.broadcasted_iota(jnp.int32, (rows, B_WIN), 0) & (BAND_Q - 1)) + B_KROWS) >> shift
    w = lax.broadcasted_iota(jnp.int32, (rows, B_WIN), 1)
    first = jnp.where(i == 0, B_KROWS, 0)
    valid = jnp.where(w >= first, jnp.abs(t - (w >> shift) - 1), 2) <= 1
    head = lax.broadcasted_iota(jnp.int32, (rows, 1), 0) >> (BAND_Q.bit_length() - 1)
    for g in range(B_KV_HEADS):
        kc = slice(g * HEAD_DIM, (g + 1) * HEAD_DIM)
        k = jnp.concatenate([k0_ref[:, kc], k1_ref[:, kc], k2_ref[:, kc]], axis=0)
        v = jnp.concatenate([v0_ref[:, kc], v1_ref[:, kc], v2_ref[:, kc]], axis=0)
        q = jnp.concatenate([q_ref[:, (g * group + hi) * HEAD_DIM:(g * group + hi + 1) * HEAD_DIM]
                             for hi in range(group)], axis=0)
        s = jnp.where(valid, _dot_nt(q, k), NEG_INF)
        sink = jnp.zeros((rows, 1), F32)
        for hi in range(group):
            sink = jnp.where(head == hi, sink_ref[g * group + hi] * LOG2E, sink)
        m = jnp.maximum(jnp.max(s, axis=-1, keepdims=True), sink)
        p = jnp.exp2(s - m)
        l = jnp.sum(p, axis=-1, keepdims=True) + jnp.exp2(sink - m)
        o = (_dot(p.astype(BF16), v) / l).astype(o_ref.dtype)
        for hi in range(group):
            c = (g * group + hi) * HEAD_DIM
            o_ref[:, c:c + HEAD_DIM] = o[hi * BAND_Q:(hi + 1) * BAND_Q, :]


def _attn_b(q, k, v, sinks):
    s, w = q.shape
    kw = k.shape[1]
    per = BAND_Q // B_KROWS
    row = pl.BlockSpec((BAND_Q, w), lambda i: (i, 0))
    kv = [pl.BlockSpec((B_KROWS, kw), functools.partial(
        lambda i, off: (jnp.maximum(i * per + off, 0), 0), off=off)) for off in (-1, 0, 1)]
    return pl.pallas_call(
        _attn_b_kernel, grid=(s // BAND_Q,),
        in_specs=[pl.BlockSpec(memory_space=pltpu.SMEM), row] + kv + kv,
        out_specs=row,
        out_shape=jax.ShapeDtypeStruct((s, w), BF16),
        compiler_params=_cparams(("arbitrary",), 40), name="attn_b",
    )(sinks, q, k, k, k, v, v, v)


def _merge_kernel(h_ref, oa_ref, ob_ref, oc_ref, od_ref, wg_ref, bg_ref, wb_ref, out_ref):
    h = h_ref[...]
    acc = None
    for n, o_ref in enumerate((oa_ref, ob_ref, oc_ref, od_ref)):
        gate = jax.nn.sigmoid(_dot(h, wg_ref[n]) + bg_ref[n])
        val = gate * _dot(o_ref[...], wb_ref[n].astype(BF16))
        acc = val if acc is None else acc + val
    out_ref[...] = acc.astype(out_ref.dtype)


def _merge(h, outs, wg, bg, wb, l):
    s, d = h.shape
    tm, tn = min(1024, s), min(256, d)
    once = pl.Buffered(1)
    o_spec = pl.BlockSpec((tm, BRANCH_W), lambda i, j: (i, 0), pipeline_mode=once)
    return pl.pallas_call(
        _merge_kernel, grid=(s // tm, d // tn),
        in_specs=[pl.BlockSpec((tm, d), lambda i, j: (i, 0)),
                  o_spec, o_spec, o_spec, o_spec,
                  pl.BlockSpec((None, N_BRANCH, d, tn), lambda i, j: (l, 0, 0, j)),
                  pl.BlockSpec((N_BRANCH, 1, tn), lambda i, j: (0, 0, j)),
                  pl.BlockSpec((None, N_BRANCH, BRANCH_W, tn), lambda i, j: (l, 0, 0, j))],
        out_specs=pl.BlockSpec((tm, tn), lambda i, j: (i, j)),
        out_shape=jax.ShapeDtypeStruct((s, d), BF16),
        compiler_params=_cparams(("parallel", "arbitrary"), 60), name="merge",
    )(h, *outs, wg, bg, wb)


def _resid_kernel(a_ref, w_ref, x_ref, g_ref, o_ref):
    o_ref[...] = x_ref[...] + g_ref[...] * _dot(a_ref[...], w_ref[...].astype(BF16))


def _resid_moe_kernel(a_ref, w_ref, c_ref, bd_ref, x_ref, g_ref, o_ref):
    y = _dot(a_ref[...], w_ref[...].astype(BF16)) + _dot(c_ref[...].astype(BF16), bd_ref[...])
    o_ref[...] = x_ref[...] + g_ref[...] * y


def _resid(a, w, l, x, gate, comb=None, b_dn=None):
    s, k = a.shape
    d = w.shape[2]
    tm, tn = min(1024, s), min(512, d)
    a_spec = pl.BlockSpec((tm, k), lambda i, j: (i, 0))
    w_spec = pl.BlockSpec((None, k, tn), lambda i, j: (l, 0, j))
    x_spec = pl.BlockSpec((tm, tn), lambda i, j: (i, j))
    g_spec = pl.BlockSpec((1, tn), lambda i, j: (0, j))
    if comb is None:
        kern, ins = _resid_kernel, (a, w, x, gate)
        in_specs = [a_spec, w_spec, x_spec, g_spec]
    else:
        kern, ins = _resid_moe_kernel, (a, w, comb, b_dn, x, gate)
        in_specs = [a_spec, w_spec, pl.BlockSpec((tm, LANES), lambda i, j: (i, 0)),
                    pl.BlockSpec((LANES, tn), lambda i, j: (0, j)), x_spec, g_spec]
    return pl.pallas_call(
        kern, grid=(s // tm, d // tn), in_specs=in_specs, out_specs=x_spec,
        out_shape=jax.ShapeDtypeStruct((s, d), F32),
        compiler_params=_cparams(("parallel", "parallel"), 48), name="resid",
    )(*ins)


def _moe_up_kernel(h_ref, w_ref, b_ref, c_ref, o_ref, *, eb):
    j = pl.program_id(1)
    h = h_ref[...]
    comb = c_ref[...]
    lane = lax.broadcasted_iota(jnp.int32, comb.shape, 1)
    for e in range(eb):
        gu = _dot(h, w_ref[e].astype(BF16)) + b_ref[e]
        glu = jnp.minimum(gu[:, :D_EXPERT], SWIGLU_LIMIT)
        lin = jnp.clip(gu[:, D_EXPERT:], -SWIGLU_LIMIT, SWIGLU_LIMIT)
        act = glu * jax.nn.sigmoid(SWIGLU_ALPHA * glu) * (lin + 1.0)
        ce = jnp.sum(jnp.where(lane == j * eb + e, comb, 0.0), axis=-1, keepdims=True)
        o_ref[:, e * D_EXPERT:(e + 1) * D_EXPERT] = (act * ce).astype(o_ref.dtype)


def _moe_up(h, w_gu, l, b_gu, comb):
    s, d = h.shape
    tm, eb = min(1024, s), 2
    return pl.pallas_call(
        functools.partial(_moe_up_kernel, eb=eb), grid=(s // tm, N_EXPERTS // eb),
        in_specs=[pl.BlockSpec((tm, d), lambda i, j: (i, 0)),
                  pl.BlockSpec((None, eb, d, 2 * D_EXPERT), lambda i, j: (l, j, 0, 0)),
                  pl.BlockSpec((eb, 1, 2 * D_EXPERT), lambda i, j: (j, 0, 0)),
                  pl.BlockSpec((tm, LANES), lambda i, j: (i, 0))],
        out_specs=pl.BlockSpec((tm, eb * D_EXPERT), lambda i, j: (i, j)),
        out_shape=jax.ShapeDtypeStruct((s, N_EXPERTS * D_EXPERT), BF16),
        compiler_params=_cparams(("parallel", "parallel"), 48), name="moe_up",
    )(h, w_gu, b_gu, comb)


def _w_in_tail(w):
    d = w.shape[0]
    zeros = lambda n: jnp.zeros((d, n), BF16)
    return jnp.concatenate(
        [w[:, C_HEADS:].astype(BF16), zeros(LANES - D_ROPE), w[:, :C_HEADS].astype(BF16),
         zeros(N_P - OFF_F - C_HEADS)], axis=1)


def _reorder_w_q_b(w):
    r = w.shape[0]
    w = w.reshape(r, D_HEADS, D_QK)
    w = jnp.pad(w, ((0, 0), (0, 0), (0, AUG - D_QK)))
    return w.reshape(r, D_HEADS * AUG)


def _reorder_w_kv_b(w):
    r = w.shape[0]
    w = w.reshape(r, D_HEADS, D_NOPE + D_V)
    return jnp.concatenate([w[:, :, :D_NOPE].reshape(r, -1), w[:, :, D_NOPE:].reshape(r, -1)], axis=1)


def _pad_lanes(v, width, value=0.0):
    return jnp.pad(v, (0, width - v.shape[0]), constant_values=value).reshape(1, width)


def kernel(x, c, positions, ada_w, ada_b, ada_layer, norm1_g, norm2_g, w_in, a_q_norm, a_k_norm, a_rel_bias, b_q_norm, b_k_norm, b_sinks, c_q_norm, c_k_norm, c_f_bias, d_q_a_norm, d_w_q_b, d_kv_a_norm, d_w_kv_b, d_q_norm, d_k_norm, w_branch, w_gate, b_gate, w_out, router_w, router_b, w_gu, b_gu, w_dn, b_dn):
    batch, s, d = x.shape
    assert batch == 1, "kernels are written for a single sequence"
    depth = w_in.shape[0]
    xs = x.reshape(s, d)

    base_mod = _adaln(c.reshape(d, 1), ada_w, ada_b).reshape(6, d)
    cos_b, sin_b, cos_d, sin_d = _rope_tables(positions.reshape(s, 1).astype(F32))
    a_mask = _a_mask()
    w_gate_bf16 = w_gate.astype(BF16)
    w_in_t = jnp.swapaxes(w_in, 1, 2).astype(BF16)

    for l in range(depth):
        mod = base_mod + ada_layer[l]
        shift1, scale1, gate1, shift2, scale2, gate2 = (mod[j:j + 1] for j in range(6))

        h = _norm(xs, norm1_g[l].reshape(1, d), scale1, shift1)
        p = _in_proj(h, w_in_t, l, _w_in_tail(w_in[l, :, OFF_QLAT:]))
        gains = jnp.stack([a_q_norm[l], a_k_norm[l], b_q_norm[l], b_k_norm[l],
                           c_q_norm[l], c_k_norm[l], jnp.zeros_like(a_q_norm[l]),
                           jnp.zeros_like(a_q_norm[l])])
        (qa, ka, va, qb, kb, vb, qc, kc, vc, ql, kvl, kr) = _prep(
            p, gains, d_q_a_norm[l].reshape(1, -1), d_kv_a_norm[l].reshape(1, -1),
            _pad_lanes(c_f_bias[l], LANES), cos_b, sin_b)
        qd, kd, vd = _prep_d(
            ql, kvl, kr, _reorder_w_q_b(d_w_q_b[l]).astype(BF16),
            _reorder_w_kv_b(d_w_kv_b[l]).astype(BF16),
            _pad_lanes(d_q_norm[l], AUG), _pad_lanes(d_k_norm[l], AUG), cos_d, sin_d)

        o_a = _attn_a(qa, ka, va, _a_table(a_rel_bias[l]), a_mask)
        o_b = _attn_b(qb, kb, vb, b_sinks[l].astype(F32))
        o_c = _flash(qc, kc, vc, C_HEADS, 1)
        o_d = _flash(qd, kd, vd, D_HEADS, CHUNK)

        merged = _merge(h, (o_a, o_b, o_c, o_d), w_gate_bf16,
                        b_gate[l].reshape(N_BRANCH, 1, d), w_branch, l)
        xs = _resid(merged, w_out, l, xs, gate1)

        rw = jnp.pad(router_w[l], ((0, 0), (0, LANES - N_EXPERTS)))
        rb = _pad_lanes(router_b[l].astype(F32), LANES, NEG_INF)
        h2, comb = _norm(xs, norm2_g[l].reshape(1, d), scale2, shift2, router=(rw, rb))
        act = _moe_up(h2, w_gu, l, b_gu[l].reshape(N_EXPERTS, 1, 2 * D_EXPERT), comb)
        bdn = jnp.pad(b_dn[l], ((0, LANES - N_EXPERTS), (0, 0))).astype(BF16)
        xs = _resid(act, w_dn.reshape(depth, N_EXPERTS * D_EXPERT, d), l, xs, gate2,
                    comb=comb, b_dn=bdn)
    return xs.reshape(batch, s, d)
```

```python
import functools
import math

import numpy as np
import jax
import jax.numpy as jnp
from jax import lax
from jax.experimental import pallas as pl
from jax.experimental.pallas import tpu as pltpu

F32 = jnp.float32
BF16 = jnp.bfloat16

CHUNK = 64
HEAD_DIM = 128
N_BRANCH = 4
BRANCH_W = 1024
ROPE_THETA = 10000.0
EPS = 1e-6
NEG_INF = -1e30
A_HEADS = 8
A_PREV_CHUNKS = 8
REL_CLIP = 256
B_Q_HEADS = 8
B_KV_HEADS = 2
B_PREV_CHUNKS = 2
C_HEADS = 8
D_HEADS = 8
D_Q_LORA = 896
D_KV_LORA = 256
D_NOPE = 128
D_ROPE = 64
D_V = 128
D_QK = D_NOPE + D_ROPE
N_EXPERTS = 32
TOP_K = 4
D_EXPERT = 128
SWIGLU_LIMIT = 7.0
SWIGLU_ALPHA = 1.702

LANES = 128
V7X_VMEM_BYTES = 64 * 1024 * 1024

LOG2E = math.log2(math.e)

OFF_QA, OFF_KA, OFF_VA = 0, 1024, 2048
OFF_QB, OFF_KB, OFF_VB = 3072, 4096, 4352
OFF_QC, OFF_KC, OFF_VC = 4608, 5632, 6656
OFF_QLAT = 7680
OFF_KVLAT = OFF_QLAT + D_Q_LORA
OFF_KROPE = OFF_KVLAT + D_KV_LORA
OFF_F = OFF_KROPE + LANES
N_P = 9216

AUG = 2 * HEAD_DIM
FLASH_BLK = 1024
FLASH_ROW_CHUNK = 32
FLASH_HEADS = 2
PREP_ROWS = 256
BAND_Q = 256
A_WIN = BAND_Q + A_PREV_CHUNKS * CHUNK
B_WIN = BAND_Q + B_PREV_CHUNKS * CHUNK


def _cparams(sems, vmem_mb):
    limit = vmem_mb * 1024 * 1024
    assert limit < V7X_VMEM_BYTES
    return pltpu.CompilerParams(dimension_semantics=sems, vmem_limit_bytes=limit)


def _dot(a, b):
    return jnp.dot(a, b, preferred_element_type=F32)


def _dot_nt(a, b):
    return lax.dot_general(a, b, (((1,), (1,)), ((), ())), preferred_element_type=F32)


def _adaln_kernel(c_ref, w_ref, b_ref, o_ref):
    c = c_ref[...]
    s = c * jax.nn.sigmoid(c)
    o_ref[...] = jnp.sum(w_ref[...] * s, axis=0, keepdims=True) + b_ref[...]


def _adaln(c_col, ada_w, ada_b):
    d, n = ada_w.shape
    tn = min(512, n)
    return pl.pallas_call(
        _adaln_kernel,
        grid=(n // tn,),
        in_specs=[pl.BlockSpec((d, 1), lambda j: (0, 0)),
                  pl.BlockSpec((d, tn), lambda j: (0, j)),
                  pl.BlockSpec((1, tn), lambda j: (0, j))],
        out_specs=pl.BlockSpec((1, tn), lambda j: (0, j)),
        out_shape=jax.ShapeDtypeStruct((1, n), F32),
        compiler_params=_cparams(("parallel",), 40),
        name="adaln",
    )(c_col, ada_w, ada_b.reshape(1, n))


def _rope_table_kernel(pos_ref, fb_ref, fd_ref, cb_ref, sb_ref, cd_ref, sd_ref):
    pos = pos_ref[...]
    lane = lax.broadcasted_iota(jnp.int32, (1, LANES), 1)
    ab = pos * fb_ref[...]
    sb = jnp.sin(ab)
    cb_ref[...] = jnp.cos(ab)
    sb_ref[...] = jnp.where(lane < HEAD_DIM // 2, -sb, sb)
    ad = pos * fd_ref[...]
    sd = jnp.sin(ad)
    cd_ref[...] = jnp.where(lane < D_ROPE, jnp.cos(ad), 1.0)
    sd_ref[...] = jnp.where(lane < D_ROPE // 2, -sd, jnp.where(lane < D_ROPE, sd, 0.0))


def _rope_tables(pos_col):
    s = pos_col.shape[0]
    tm = min(512, s)
    lane = np.arange(LANES)
    half_b = HEAD_DIM // 2
    fb = ROPE_THETA ** (-(lane % half_b).astype(np.float64) / half_b)
    half_d = D_ROPE // 2
    fd = np.where(lane < D_ROPE, ROPE_THETA ** (-(lane % half_d).astype(np.float64) / half_d), 0.0)
    fb = jnp.asarray(fb, F32).reshape(1, LANES)
    fd = jnp.asarray(fd, F32).reshape(1, LANES)
    tab = jax.ShapeDtypeStruct((s, LANES), F32)
    row = pl.BlockSpec((tm, LANES), lambda i: (i, 0))
    const = pl.BlockSpec((1, LANES), lambda i: (0, 0))
    return pl.pallas_call(
        _rope_table_kernel,
        grid=(s // tm,),
        in_specs=[pl.BlockSpec((tm, 1), lambda i: (i, 0)), const, const],
        out_specs=[row, row, row, row],
        out_shape=[tab, tab, tab, tab],
        compiler_params=_cparams(("parallel",), 32),
        name="rope_tables",
    )(pos_col, fb, fd)


def _mod_norm(x, g, sc, sh):
    ms = jnp.mean(x * x, axis=-1, keepdims=True)
    y = x * lax.rsqrt(ms + EPS)
    return (y * g) * (1.0 + sc) + sh


def _norm_kernel(x_ref, g_ref, sc_ref, sh_ref, h_ref):
    h_ref[...] = _mod_norm(x_ref[...], g_ref[...], sc_ref[...], sh_ref[...]).astype(h_ref.dtype)


def _norm_router_kernel(x_ref, g_ref, sc_ref, sh_ref, rwh_ref, rwl_ref, rb_ref, h_ref, comb_ref):
    h = _mod_norm(x_ref[...], g_ref[...], sc_ref[...], sh_ref[...])
    h_hi = h.astype(BF16)
    h_ref[...] = h_hi
    h_lo = (h - h_hi.astype(F32)).astype(BF16)
    rw_hi = rwh_ref[...]
    logits = (_dot(h_hi, rw_hi) + _dot(h_lo, rw_hi) + _dot(h_hi, rwl_ref[...])
              + rb_ref[...])
    lane = lax.broadcasted_iota(jnp.int32, logits.shape, 1).astype(F32)
    work = logits
    vals, sels = [], []
    for _ in range(TOP_K):
        m = jnp.max(work, axis=-1, keepdims=True)
        idx = jnp.min(jnp.where(work == m, lane, float(LANES)), axis=-1, keepdims=True)
        sel = lane == idx
        vals.append(m)
        sels.append(sel)
        work = jnp.where(sel, -3.0e38, work)
    es = [jnp.exp(v - vals[0]) for v in vals]
    inv = 1.0 / (es[0] + es[1] + es[2] + es[3])
    comb = jnp.zeros_like(logits)
    for e, sel in zip(es, sels):
        comb = comb + jnp.where(sel, e * inv, 0.0)
    comb_ref[...] = comb


def _norm(x, g, sc, sh, router=None):
    s, d = x.shape
    tm = min(512, s)
    row = pl.BlockSpec((tm, d), lambda i: (i, 0))
    vec = pl.BlockSpec((1, d), lambda i: (0, 0))
    if router is None:
        return pl.pallas_call(
            _norm_kernel, grid=(s // tm,),
            in_specs=[row, vec, vec, vec], out_specs=row,
            out_shape=jax.ShapeDtypeStruct((s, d), BF16),
            compiler_params=_cparams(("parallel",), 48), name="norm",
        )(x, g, sc, sh)
    rw, rb = router
    rw_hi = rw.astype(BF16)
    rw_lo = (rw - rw_hi.astype(F32)).astype(BF16)
    rw_spec = pl.BlockSpec((d, LANES), lambda i: (0, 0))
    return pl.pallas_call(
        _norm_router_kernel, grid=(s // tm,),
        in_specs=[row, vec, vec, vec, rw_spec, rw_spec,
                  pl.BlockSpec((1, LANES), lambda i: (0, 0))],
        out_specs=[row, pl.BlockSpec((tm, LANES), lambda i: (i, 0))],
        out_shape=[jax.ShapeDtypeStruct((s, d), BF16), jax.ShapeDtypeStruct((s, LANES), F32)],
        compiler_params=_cparams(("parallel",), 56), name="norm_router",
    )(x, g, sc, sh, rw_hi, rw_lo, rb)


IN_TN = 768
IN_MAIN_TILES = OFF_QLAT // IN_TN


def _in_proj_kernel(a_ref, w_ref, wt_ref, o_ref):
    j = pl.program_id(1)

    @pl.when(j < IN_MAIN_TILES)
    def _():
        o_ref[...] = _dot_nt(a_ref[...], w_ref[...]).astype(o_ref.dtype)

    @pl.when(j >= IN_MAIN_TILES)
    def _():
        o_ref[...] = _dot(a_ref[...], wt_ref[...]).astype(o_ref.dtype)


def _in_proj(a, w_t, l, w_tail):
    m, k = a.shape
    tm = min(1024, m)
    n_tail = w_tail.shape[1] // IN_TN
    last = IN_MAIN_TILES - 1
    return pl.pallas_call(
        _in_proj_kernel, grid=(m // tm, IN_MAIN_TILES + n_tail),
        in_specs=[pl.BlockSpec((tm, k), lambda i, j: (i, 0)),
                  pl.BlockSpec((None, IN_TN, k), lambda i, j: (l, jnp.minimum(j, last), 0)),
                  pl.BlockSpec((k, IN_TN), lambda i, j: (0, jnp.maximum(j - IN_MAIN_TILES, 0)))],
        out_specs=pl.BlockSpec((tm, IN_TN), lambda i, j: (i, j)),
        out_shape=jax.ShapeDtypeStruct((m, N_P), BF16),
        compiler_params=_cparams(("parallel", "arbitrary"), 56), name="in_proj",
    )(a, w_t, w_tail)


def _prep_kernel(p_ref, gains_ref, gql_ref, gkvl_ref, fb_ref, cb_ref, sb_ref,
                 qa_ref, ka_ref, va_ref, qb_ref, kb_ref, vb_ref,
                 qc_ref, kc_ref, vc_ref, ql_ref, kvl_ref, kr_ref, carry_ref):
    tm = p_ref.shape[0]
    qscale = HEAD_DIM ** -0.5 * LOG2E

    @pl.when(pl.program_id(0) == 0)
    def _():
        carry_ref[...] = jnp.zeros_like(carry_ref)

    def head_norm(off, g_row, scale):
        x = p_ref[:, off:off + HEAD_DIM].astype(F32)
        ms = jnp.mean(x * x, axis=-1, keepdims=True)
        return x * lax.rsqrt(ms + EPS) * (gains_ref[g_row:g_row + 1, :] * scale)

    cb, sb = cb_ref[...], sb_ref[...]

    def rope(y):
        return y * cb + pltpu.roll(y, HEAD_DIM // 2, 1) * sb

    for h in range(A_HEADS):
        c = h * HEAD_DIM
        qa_ref[:, c:c + HEAD_DIM] = head_norm(OFF_QA + c, 0, qscale).astype(BF16)
        ka_ref[:, c:c + HEAD_DIM] = head_norm(OFF_KA + c, 1, 1.0).astype(BF16)
    va_ref[...] = p_ref[:, OFF_VA:OFF_VA + A_HEADS * HEAD_DIM]

    for h in range(B_Q_HEADS):
        c = h * HEAD_DIM
        qb_ref[:, c:c + HEAD_DIM] = rope(head_norm(OFF_QB + c, 2, qscale)).astype(BF16)
    for h in range(B_KV_HEADS):
        c = h * HEAD_DIM
        kb_ref[:, c:c + HEAD_DIM] = rope(head_norm(OFF_KB + c, 3, 1.0)).astype(BF16)
    vb_ref[...] = p_ref[:, OFF_VB:OFF_VB + B_KV_HEADS * HEAD_DIM]

    z = p_ref[:, OFF_F:OFF_F + LANES].astype(F32) + fb_ref[...]
    logf = jnp.minimum(z, 0.0) - jnp.log1p(jnp.exp(-jnp.abs(z)))
    r_i = lax.broadcasted_iota(jnp.int32, (tm, tm), 0)
    c_i = lax.broadcasted_iota(jnp.int32, (tm, tm), 1)
    tri = jnp.where(r_i >= c_i, 1.0, 0.0).astype(F32)
    cum = jnp.dot(tri, logf, precision=lax.Precision.HIGHEST,
                  preferred_element_type=F32) + carry_ref[...]
    carry_ref[...] = cum[tm - 1:tm, :]
    cum2 = cum * LOG2E
    hi = cum2.astype(BF16).astype(F32)
    r1 = cum2 - hi
    mid = r1.astype(BF16).astype(F32)
    lo = r1 - mid
    lane = lax.broadcasted_iota(jnp.int32, (tm, LANES), 1)
    for h in range(C_HEADS):
        c = h * HEAD_DIM
        a = h * AUG
        hi_h, mid_h, lo_h = hi[:, h:h + 1], mid[:, h:h + 1], lo[:, h:h + 1]
        aug_q = jnp.where(lane == 0, hi_h, jnp.where(lane == 1, mid_h, jnp.where(
            lane == 2, lo_h, jnp.where(lane < 6, 1.0, 0.0))))
        aug_k = jnp.where(lane < 3, 1.0, jnp.where(lane == 3, -hi_h, jnp.where(
            lane == 4, -mid_h, jnp.where(lane == 5, -lo_h, 0.0))))
        qc_ref[:, a:a + HEAD_DIM] = head_norm(OFF_QC + c, 4, qscale).astype(BF16)
        qc_ref[:, a + HEAD_DIM:a + AUG] = aug_q.astype(BF16)
        kc_ref[h, :HEAD_DIM, :] = head_norm(OFF_KC + c, 5, 1.0).T.astype(BF16)
        kc_ref[h, HEAD_DIM:, :] = aug_k.T.astype(BF16)
    vc_ref[...] = p_ref[:, OFF_VC:OFF_VC + C_HEADS * HEAD_DIM]

    xq = p_ref[:, OFF_QLAT:OFF_QLAT + D_Q_LORA].astype(F32)
    ms = jnp.mean(xq * xq, axis=-1, keepdims=True)
    ql_ref[...] = (xq * lax.rsqrt(ms + EPS) * gql_ref[...]).astype(BF16)
    xkv = p_ref[:, OFF_KVLAT:OFF_KVLAT + D_KV_LORA].astype(F32)
    ms = jnp.mean(xkv * xkv, axis=-1, keepdims=True)
    kvl_ref[...] = (xkv * lax.rsqrt(ms + EPS) * gkvl_ref[...]).astype(BF16)
    kr_ref[...] = p_ref[:, OFF_KROPE:OFF_KROPE + LANES].astype(F32)


def _kt_shape_spec(n_heads, s, tm):
    blk = min(FLASH_BLK, s)
    per = blk // tm
    shape = jax.ShapeDtypeStruct((n_heads, s // blk, AUG, blk), BF16)
    spec = pl.BlockSpec((n_heads, None, AUG, tm), lambda i: (0, i // per, 0, i % per))
    return shape, spec


def _prep(p, gains, gql, gkvl, fb, cos_b, sin_b):
    s = p.shape[0]
    tm = min(PREP_ROWS, s)

    def row(w):
        return pl.BlockSpec((tm, w), lambda i: (i, 0))

    def const(r, w):
        return pl.BlockSpec((r, w), lambda i: (0, 0))

    widths = [1024, 1024, 1024, 1024, 256, 256, C_HEADS * AUG, None, 1024,
              D_Q_LORA, D_KV_LORA]
    out_shape = [jax.ShapeDtypeStruct((s, w), BF16) for w in widths if w]
    out_specs = [row(w) for w in widths if w]
    kt_shape, kt_spec = _kt_shape_spec(C_HEADS, s, tm)
    out_shape.insert(7, kt_shape)
    out_specs.insert(7, kt_spec)
    out_shape.append(jax.ShapeDtypeStruct((s, LANES), F32))
    out_specs.append(row(LANES))
    return pl.pallas_call(
        _prep_kernel, grid=(s // tm,),
        in_specs=[row(N_P), const(8, LANES), const(1, D_Q_LORA), const(1, D_KV_LORA),
                  const(1, LANES), row(LANES), row(LANES)],
        out_specs=out_specs, out_shape=out_shape,
        scratch_shapes=[pltpu.VMEM((1, LANES), F32)],
        compiler_params=_cparams(("arbitrary",), 48), name="prep",
    )(p, gains, gql, gkvl, fb, cos_b, sin_b)


def _rope_d(y, cd, sd):
    lane = lax.broadcasted_iota(jnp.int32, y.shape, 1)
    half = D_ROPE // 2
    partner = jnp.where(lane < half, pltpu.roll(y, LANES - half, 1), pltpu.roll(y, half, 1))
    return y * cd + partner * sd


def _prep_d_kernel(ql_ref, kvl_ref, kr_ref, wq_ref, wkv_ref, gq_ref, gk_ref, cd_ref, sd_ref,
                   qd_ref, kd_ref, vd_ref):
    qscale = D_QK ** -0.5 * LOG2E
    cd, sd = cd_ref[...], sd_ref[...]
    q = _dot(ql_ref[...], wq_ref[...])
    kv = _dot(kvl_ref[...], wkv_ref[...])
    kr = kr_ref[...]
    ss_r = jnp.sum(kr * kr, axis=-1, keepdims=True)
    gq = gq_ref[...]
    gk = gk_ref[...]
    for h in range(D_HEADS):
        a = h * AUG
        qn, qr = q[:, a:a + HEAD_DIM], q[:, a + HEAD_DIM:a + AUG]
        ms = (jnp.sum(qn * qn, axis=-1, keepdims=True)
              + jnp.sum(qr * qr, axis=-1, keepdims=True)) * (1.0 / D_QK)
        r = lax.rsqrt(ms + EPS) * qscale
        qd_ref[:, a:a + HEAD_DIM] = (qn * r * gq[:, :HEAD_DIM]).astype(BF16)
        qd_ref[:, a + HEAD_DIM:a + AUG] = _rope_d(qr * r * gq[:, HEAD_DIM:], cd, sd).astype(BF16)
        kn = kv[:, h * D_NOPE:(h + 1) * D_NOPE]
        ms = (jnp.sum(kn * kn, axis=-1, keepdims=True) + ss_r) * (1.0 / D_QK)
        r = lax.rsqrt(ms + EPS)
        kd_ref[h, :HEAD_DIM, :] = (kn * r * gk[:, :HEAD_DIM]).T.astype(BF16)
        kd_ref[h, HEAD_DIM:, :] = _rope_d(kr * r * gk[:, HEAD_DIM:], cd, sd).T.astype(BF16)
    vd_ref[...] = kv[:, D_HEADS * D_NOPE:].astype(BF16)


def _prep_d(ql, kvl, kr, wq, wkv, gq, gk, cos_d, sin_d):
    s = ql.shape[0]
    tm = min(PREP_ROWS, s)

    def row(w):
        return pl.BlockSpec((tm, w), lambda i: (i, 0))

    def const(r, w):
        return pl.BlockSpec((r, w), lambda i: (0, 0))

    kt_shape, kt_spec = _kt_shape_spec(D_HEADS, s, tm)
    return pl.pallas_call(
        _prep_d_kernel, grid=(s // tm,),
        in_specs=[row(D_Q_LORA), row(D_KV_LORA), row(LANES),
                  const(D_Q_LORA, D_HEADS * AUG), const(D_KV_LORA, 2 * D_HEADS * D_NOPE),
                  const(1, AUG), const(1, AUG), row(LANES), row(LANES)],
        out_specs=[row(D_HEADS * AUG), kt_spec, row(D_HEADS * D_V)],
        out_shape=[jax.ShapeDtypeStruct((s, D_HEADS * AUG), BF16), kt_shape,
                   jax.ShapeDtypeStruct((s, D_HEADS * D_V), BF16)],
        compiler_params=_cparams(("parallel",), 40), name="prep_d",
    )(ql, kvl, kr, wq, wkv, gq, gk, cos_d, sin_d)


def _flash_kernel(q_ref, kt_ref, v_ref, o_ref, m_ref, l_ref, acc_ref, *, blk, unit):
    i = pl.program_id(1)
    m_ref[...] = jnp.full_like(m_ref, NEG_INF)
    l_ref[...] = jnp.zeros_like(l_ref)
    acc_ref[...] = jnp.zeros_like(acc_ref)
    rc = FLASH_ROW_CHUNK

    def block(j, row0, nrows, nkeys, masked):
        start = pl.multiple_of(j * blk, blk)
        for g in range(FLASH_HEADS):
            s = _dot(q_ref[row0:row0 + nrows, g * AUG:(g + 1) * AUG], kt_ref[g, j, :, :nkeys])
            v = v_ref[pl.ds(start, nkeys), g * LANES:(g + 1) * LANES]
            ps, alphas = [], []
            for r in range(nrows // rc):
                rows = slice(row0 + r * rc, row0 + (r + 1) * rc)
                sc = s[r * rc:(r + 1) * rc, :]
                if masked:
                    qi = lax.broadcasted_iota(jnp.int32, sc.shape, 0) + (row0 + r * rc)
                    ki = lax.broadcasted_iota(jnp.int32, sc.shape, 1)
                    if unit > 1:
                        shift = unit.bit_length() - 1
                        qi, ki = qi >> shift, ki >> shift
                    sc = jnp.where(ki <= qi, sc, NEG_INF)
                m_prev = m_ref[g, rows, :]
                m_new = jnp.maximum(m_prev, jnp.max(sc, axis=-1, keepdims=True))
                alpha = jnp.exp2(m_prev - m_new)
                pc = [jnp.exp2(sc[:, c * LANES:(c + 1) * LANES] - m_new)
                      for c in range(nkeys // LANES)]
                lsum = pc[0]
                for c in range(1, len(pc)):
                    lsum = lsum + pc[c]
                l_ref[g, rows, :] = alpha * l_ref[g, rows, :] + lsum
                m_ref[g, rows, :] = m_new
                ps.append(jnp.concatenate(pc, axis=1).astype(BF16))
                alphas.append(alpha)
            p = jnp.concatenate(ps, axis=0)
            alpha = jnp.concatenate(alphas, axis=0)
            acc_ref[g, row0:row0 + nrows, :] = (alpha * acc_ref[g, row0:row0 + nrows, :]
                                                + _dot(p, v))

    def body(j, carry):
        block(j, 0, blk, blk, False)
        return carry

    lax.fori_loop(0, i, body, 0)
    half = blk // 2
    if half % max(rc, unit, LANES) == 0:
        block(i, 0, half, half, True)
        block(i, half, half, blk, True)
    else:
        block(i, 0, blk, blk, True)
    for g in range(FLASH_HEADS):
        l = jnp.sum(l_ref[g], axis=-1, keepdims=True)
        o_ref[:, g * LANES:(g + 1) * LANES] = (acc_ref[g] / l).astype(o_ref.dtype)


def _flash(q, kt, v, n_heads, unit):
    s = q.shape[0]
    blk = kt.shape[3]
    dv = v.shape[1] // n_heads
    g = FLASH_HEADS
    assert dv == LANES and n_heads % g == 0
    once = pl.Buffered(1)
    return pl.pallas_call(
        functools.partial(_flash_kernel, blk=blk, unit=unit),
        grid=(n_heads // g, s // blk),
        in_specs=[pl.BlockSpec((blk, g * AUG), lambda h, i: (i, h)),
                  pl.BlockSpec((g, s // blk, AUG, blk), lambda h, i: (h, 0, 0, 0), pipeline_mode=once),
                  pl.BlockSpec((s, g * dv), lambda h, i: (0, h), pipeline_mode=once)],
        out_specs=pl.BlockSpec((blk, g * dv), lambda h, i: (i, h)),
        out_shape=jax.ShapeDtypeStruct((s, n_heads * dv), BF16),
        scratch_shapes=[pltpu.VMEM((g, blk, LANES), F32), pltpu.VMEM((g, blk, LANES), F32),
                        pltpu.VMEM((g, blk, dv), F32)],
        compiler_params=_cparams(("parallel", "arbitrary"), 56), name=f"flash_u{unit}",
    )(q, kt, v)


A_PREV_BLOCKS = A_PREV_CHUNKS * CHUNK // BAND_Q


def _attn_a_kernel(q_ref, k0_ref, k1_ref, k2_ref, v0_ref, v1_ref, v2_ref, t_ref, mk_ref, o_ref):
    rc = FLASH_ROW_CHUNK

    def scores(h):
        cols = slice(h * HEAD_DIM, (h + 1) * HEAD_DIM)
        k = jnp.concatenate([k0_ref[:, cols], k1_ref[:, cols], k2_ref[:, cols]], axis=0)
        return _dot_nt(q_ref[:, cols], k)

    s_next = scores(0)
    for h in range(A_HEADS):
        cols = slice(h * HEAD_DIM, (h + 1) * HEAD_DIM)
        v = jnp.concatenate([v0_ref[:, cols], v1_ref[:, cols], v2_ref[:, cols]], axis=0)
        s = s_next
        if h + 1 < A_HEADS:
            s_next = scores(h + 1)
        ps, ls = [], []
        for r in range(BAND_Q // rc):
            rows = slice(r * rc, (r + 1) * rc)
            sc = s[rows, :] + t_ref[h, rows, :] + mk_ref[rows, :]
            m = jnp.max(sc, axis=-1, keepdims=True)
            pc = [jnp.exp2(sc[:, c * LANES:(c + 1) * LANES] - m) for c in range(A_WIN // LANES)]
            lsum = pc[0]
            for c in range(1, len(pc)):
                lsum = lsum + pc[c]
            ls.append(lsum)
            ps.append(jnp.concatenate(pc, axis=1).astype(BF16))
        l = jnp.sum(jnp.concatenate(ls, axis=0), axis=-1, keepdims=True)
        o = _dot(jnp.concatenate(ps, axis=0), v) / l
        o_ref[:, cols] = o.astype(o_ref.dtype)


def _attn_a(q, k, v, table, mask):
    s, w = q.shape
    row = pl.BlockSpec((BAND_Q, w), lambda i: (i, 0))
    prev = [pl.BlockSpec((BAND_Q, w), functools.partial(
        lambda i, back: (jnp.maximum(i - back, 0), 0), back=back))
        for back in range(A_PREV_BLOCKS, -1, -1)]
    return pl.pallas_call(
        _attn_a_kernel, grid=(s // BAND_Q,),
        in_specs=[row] + prev + prev + [
            pl.BlockSpec((A_HEADS, BAND_Q, A_WIN), lambda i: (0, 0, 0)),
            pl.BlockSpec((None, BAND_Q, A_WIN), lambda i: (jnp.minimum(i, A_PREV_BLOCKS), 0, 0))],
        out_specs=row,
        out_shape=jax.ShapeDtypeStruct((s, w), BF16),
        compiler_params=_cparams(("arbitrary",), 40), name="attn_a",
    )(q, k, k, k, v, v, v, table, mask)


def _a_mask():
    b = np.arange(A_PREV_BLOCKS + 1)[:, None, None]
    q = np.arange(BAND_Q)[None, :, None]
    w = np.arange(A_WIN)[None, None, :]
    dchunk = (q + A_PREV_BLOCKS * BAND_Q) // CHUNK - w // CHUNK
    valid = (dchunk >= 0) & (dchunk <= A_PREV_CHUNKS) & (w >= (A_PREV_BLOCKS - b) * BAND_Q)
    return jnp.asarray(np.where(valid, 0.0, NEG_INF), F32)


def _a_table(rel_bias):
    n, m = BAND_Q, A_WIN
    u = np.arange(n + m)
    dist = A_PREV_BLOCKS * BAND_Q + (n - 1) - u
    idx = np.clip(dist, -REL_CLIP, REL_CLIP) + REL_CLIP
    z = rel_bias.astype(F32).T[:, jnp.asarray(idx)] * LOG2E
    skew = jnp.broadcast_to(z[:, None, :], (A_HEADS, n, n + m))
    skew = skew.reshape(A_HEADS, n * (n + m))[:, :n * (n + m - 1)]
    return skew.reshape(A_HEADS, n, n + m - 1)[:, :, n - 1:n - 1 + m]


B_KROWS = B_PREV_CHUNKS * CHUNK


def _attn_b_kernel(sink_ref, q_ref, k0_ref, k1_ref, k2_ref, v0_ref, v1_ref, v2_ref, o_ref):
    i = pl.program_id(0)
    group = B_Q_HEADS // B_KV_HEADS
    rows = group * BAND_Q
    shift = CHUNK.bit_length() - 1
    t = ((lax.broadcasted_iota(jnp.int32, (rows, B_WIN), 0) & (BAND_Q - 1)) + B_KROWS) >> shift
    w = lax.broadcasted_iota(jnp.int32, (rows, B_WIN), 1)
    first = jnp.where(i == 0, B_KROWS, 0)
    valid = jnp.where(w >= first, jnp.abs(t - (w >> shift) - 1), 2) <= 1
    head = lax.broadcasted_iota(jnp.int32, (rows, 1), 0) >> (BAND_Q.bit_length() - 1)
    for g in range(B_KV_HEADS):
        kc = slice(g * HEAD_DIM, (g + 1) * HEAD_DIM)
        k = jnp.concatenate([k0_ref[:, kc], k1_ref[:, kc], k2_ref[:, kc]], axis=0)
        v = jnp.concatenate([v0_ref[:, kc], v1_ref[:, kc], v2_ref[:, kc]], axis=0)
        q = jnp.concatenate([q_ref[:, (g * group + hi) * HEAD_DIM:(g * group + hi + 1) * HEAD_DIM]
                             for hi in range(group)], axis=0)
        s = jnp.where(valid, _dot_nt(q, k), NEG_INF)
        sink = jnp.zeros((rows, 1), F32)
        for hi in range(group):
            sink = jnp.where(head == hi, sink_ref[g * group + hi] * LOG2E, sink)
        m = jnp.maximum(jnp.max(s, axis=-1, keepdims=True), sink)
        p = jnp.exp2(s - m)
        l = jnp.sum(p, axis=-1, keepdims=True) + jnp.exp2(sink - m)
        o = (_dot(p.astype(BF16), v) / l).astype(o_ref.dtype)
        for hi in range(group):
            c = (g * group + hi) * HEAD_DIM
            o_ref[:, c:c + HEAD_DIM] = o[hi * BAND_Q:(hi + 1) * BAND_Q, :]


def _attn_b(q, k, v, sinks):
    s, w = q.shape
    kw = k.shape[1]
    per = BAND_Q // B_KROWS
    row = pl.BlockSpec((BAND_Q, w), lambda i: (i, 0))
    kv = [pl.BlockSpec((B_KROWS, kw), functools.partial(
        lambda i, off: (jnp.maximum(i * per + off, 0), 0), off=off)) for off in (-1, 0, 1)]
    return pl.pallas_call(
        _attn_b_kernel, grid=(s // BAND_Q,),
        in_specs=[pl.BlockSpec(memory_space=pltpu.SMEM), row] + kv + kv,
        out_specs=row,
        out_shape=jax.ShapeDtypeStruct((s, w), BF16),
        compiler_params=_cparams(("arbitrary",), 40), name="attn_b",
    )(sinks, q, k, k, k, v, v, v)


def _merge_kernel(h_ref, oa_ref, ob_ref, oc_ref, od_ref, wg_ref, bg_ref, wb_ref, out_ref):
    h = h_ref[...]
    acc = None
    for n, o_ref in enumerate((oa_ref, ob_ref, oc_ref, od_ref)):
        gate = jax.nn.sigmoid(_dot(h, wg_ref[n]) + bg_ref[n])
        val = gate * _dot(o_ref[...], wb_ref[n].astype(BF16))
        acc = val if acc is None else acc + val
    out_ref[...] = acc.astype(out_ref.dtype)


def _merge(h, outs, wg, bg, wb, l):
    s, d = h.shape
    tm, tn = min(1024, s), min(256, d)
    once = pl.Buffered(1)
    o_spec = pl.BlockSpec((tm, BRANCH_W), lambda i, j: (i, 0), pipeline_mode=once)
    return pl.pallas_call(
        _merge_kernel, grid=(s // tm, d // tn),
        in_specs=[pl.BlockSpec((tm, d), lambda i, j: (i, 0)),
                  o_spec, o_spec, o_spec, o_spec,
                  pl.BlockSpec((None, N_BRANCH, d, tn), lambda i, j: (l, 0, 0, j)),
                  pl.BlockSpec((N_BRANCH, 1, tn), lambda i, j: (0, 0, j)),
                  pl.BlockSpec((None, N_BRANCH, BRANCH_W, tn), lambda i, j: (l, 0, 0, j))],
        out_specs=pl.BlockSpec((tm, tn), lambda i, j: (i, j)),
        out_shape=jax.ShapeDtypeStruct((s, d), BF16),
        compiler_params=_cparams(("parallel", "arbitrary"), 60), name="merge",
    )(h, *outs, wg, bg, wb)


def _resid_kernel(a_ref, w_ref, x_ref, g_ref, o_ref):
    o_ref[...] = x_ref[...] + g_ref[...] * _dot(a_ref[...], w_ref[...].astype(BF16))


def _resid_moe_kernel(a_ref, w_ref, c_ref, bd_ref, x_ref, g_ref, o_ref):
    y = _dot(a_ref[...], w_ref[...].astype(BF16)) + _dot(c_ref[...].astype(BF16), bd_ref[...])
    o_ref[...] = x_ref[...] + g_ref[...] * y


def _resid(a, w, l, x, gate, comb=None, b_dn=None):
    s, k = a.shape
    d = w.shape[2]
    tm, tn = min(1024, s), min(512, d)
    a_spec = pl.BlockSpec((tm, k), lambda i, j: (i, 0))
    w_spec = pl.BlockSpec((None, k, tn), lambda i, j: (l, 0, j))
    x_spec = pl.BlockSpec((tm, tn), lambda i, j: (i, j))
    g_spec = pl.BlockSpec((1, tn), lambda i, j: (0, j))
    if comb is None:
        kern, ins = _resid_kernel, (a, w, x, gate)
        in_specs = [a_spec, w_spec, x_spec, g_spec]
    else:
        kern, ins = _resid_moe_kernel, (a, w, comb, b_dn, x, gate)
        in_specs = [a_spec, w_spec, pl.BlockSpec((tm, LANES), lambda i, j: (i, 0)),
                    pl.BlockSpec((LANES, tn), lambda i, j: (0, j)), x_spec, g_spec]
    return pl.pallas_call(
        kern, grid=(s // tm, d // tn), in_specs=in_specs, out_specs=x_spec,
        out_shape=jax.ShapeDtypeStruct((s, d), F32),
        compiler_params=_cparams(("parallel", "parallel"), 48), name="resid",
    )(*ins)


def _moe_up_kernel(h_ref, w_ref, b_ref, c_ref, o_ref, *, eb):
    j = pl.program_id(1)
    h = h_ref[...]
    comb = c_ref[...]
    lane = lax.broadcasted_iota(jnp.int32, comb.shape, 1)
    for e in range(eb):
        gu = _dot(h, w_ref[e].astype(BF16)) + b_ref[e]
        glu = jnp.minimum(gu[:, :D_EXPERT], SWIGLU_LIMIT)
        lin = jnp.clip(gu[:, D_EXPERT:], -SWIGLU_LIMIT, SWIGLU_LIMIT)
        act = glu * jax.nn.sigmoid(SWIGLU_ALPHA * glu) * (lin + 1.0)
        ce = jnp.sum(jnp.where(lane == j * eb + e, comb, 0.0), axis=-1, keepdims=True)
        o_ref[:, e * D_EXPERT:(e + 1) * D_EXPERT] = (act * ce).astype(o_ref.dtype)


def _moe_up(h, w_gu, l, b_gu, comb):
    s, d = h.shape
    tm, eb = min(1024, s), 2
    return pl.pallas_call(
        functools.partial(_moe_up_kernel, eb=eb), grid=(s // tm, N_EXPERTS // eb),
        in_specs=[pl.BlockSpec((tm, d), lambda i, j: (i, 0)),
                  pl.BlockSpec((None, eb, d, 2 * D_EXPERT), lambda i, j: (l, j, 0, 0)),
                  pl.BlockSpec((eb, 1, 2 * D_EXPERT), lambda i, j: (j, 0, 0)),
                  pl.BlockSpec((tm, LANES), lambda i, j: (i, 0))],
        out_specs=pl.BlockSpec((tm, eb * D_EXPERT), lambda i, j: (i, j)),
        out_shape=jax.ShapeDtypeStruct((s, N_EXPERTS * D_EXPERT), BF16),
        compiler_params=_cparams(("parallel", "parallel"), 48), name="moe_up",
    )(h, w_gu, b_gu, comb)


def _w_in_tail(w):
    d = w.shape[0]
    zeros = lambda n: jnp.zeros((d, n), BF16)
    return jnp.concatenate(
        [w[:, C_HEADS:].astype(BF16), zeros(LANES - D_ROPE), w[:, :C_HEADS].astype(BF16),
         zeros(N_P - OFF_F - C_HEADS)], axis=1)


def _reorder_w_q_b(w):
    r = w.shape[0]
    w = w.reshape(r, D_HEADS, D_QK)
    w = jnp.pad(w, ((0, 0), (0, 0), (0, AUG - D_QK)))
    return w.reshape(r, D_HEADS * AUG)


def _reorder_w_kv_b(w):
    r = w.shape[0]
    w = w.reshape(r, D_HEADS, D_NOPE + D_V)
    return jnp.concatenate([w[:, :, :D_NOPE].reshape(r, -1), w[:, :, D_NOPE:].reshape(r, -1)], axis=1)


def _pad_lanes(v, width, value=0.0):
    return jnp.pad(v, (0, width - v.shape[0]), constant_values=value).reshape(1, width)


def kernel(x, c, positions, ada_w, ada_b, ada_layer, norm1_g, norm2_g, w_in, a_q_norm, a_k_norm, a_rel_bias, b_q_norm, b_k_norm, b_sinks, c_q_norm, c_k_norm, c_f_bias, d_q_a_norm, d_w_q_b, d_kv_a_norm, d_w_kv_b, d_q_norm, d_k_norm, w_branch, w_gate, b_gate, w_out, router_w, router_b, w_gu, b_gu, w_dn, b_dn):
    batch, s, d = x.shape
    assert batch == 1, "kernels are written for a single sequence"
    depth = w_in.shape[0]
    xs = x.reshape(s, d)

    base_mod = _adaln(c.reshape(d, 1), ada_w, ada_b).reshape(6, d)
    cos_b, sin_b, cos_d, sin_d = _rope_tables(positions.reshape(s, 1).astype(F32))
    a_mask = _a_mask()
    w_gate_bf16 = w_gate.astype(BF16)
    w_in_t = jnp.swapaxes(w_in, 1, 2).astype(BF16)

    for l in range(depth):
        mod = base_mod + ada_layer[l]
        shift1, scale1, gate1, shift2, scale2, gate2 = (mod[j:j + 1] for j in range(6))

        h = _norm(xs, norm1_g[l].reshape(1, d), scale1, shift1)
        p = _in_proj(h, w_in_t, l, _w_in_tail(w_in[l, :, OFF_QLAT:]))
        gains = jnp.stack([a_q_norm[l], a_k_norm[l], b_q_norm[l], b_k_norm[l],
                           c_q_norm[l], c_k_norm[l], jnp.zeros_like(a_q_norm[l]),
                           jnp.zeros_like(a_q_norm[l])])
        (qa, ka, va, qb, kb, vb, qc, kc, vc, ql, kvl, kr) = _prep(
            p, gains, d_q_a_norm[l].reshape(1, -1), d_kv_a_norm[l].reshape(1, -1),
            _pad_lanes(c_f_bias[l], LANES), cos_b, sin_b)
        qd, kd, vd = _prep_d(
            ql, kvl, kr, _reorder_w_q_b(d_w_q_b[l]).astype(BF16),
            _reorder_w_kv_b(d_w_kv_b[l]).astype(BF16),
            _pad_lanes(d_q_norm[l], AUG), _pad_lanes(d_k_norm[l], AUG), cos_d, sin_d)

        o_a = _attn_a(qa, ka, va, _a_table(a_rel_bias[l]), a_mask)
        o_b = _attn_b(qb, kb, vb, b_sinks[l].astype(F32))
        o_c = _flash(qc, kc, vc, C_HEADS, 1)
        o_d = _flash(qd, kd, vd, D_HEADS, CHUNK)

        merged = _merge(h, (o_a, o_b, o_c, o_d), w_gate_bf16,
                        b_gate[l].reshape(N_BRANCH, 1, d), w_branch, l)
        xs = _resid(merged, w_out, l, xs, gate1)

        rw = jnp.pad(router_w[l], ((0, 0), (0, LANES - N_EXPERTS)))
        rb = _pad_lanes(router_b[l].astype(F32), LANES, NEG_INF)
        h2, comb = _norm(xs, norm2_g[l].reshape(1, d), scale2, shift2, router=(rw, rb))
        act = _moe_up(h2, w_gu, l, b_gu[l].reshape(N_EXPERTS, 1, 2 * D_EXPERT), comb)
        bdn = jnp.pad(b_dn[l], ((0, LANES - N_EXPERTS), (0, 0))).astype(BF16)
        xs = _resid(act, w_dn.reshape(depth, N_EXPERTS * D_EXPERT, d), l, xs, gate2,
                    comb=comb, b_dn=bdn)
    return xs.reshape(batch, s, d)
```
